```python
import math
import jax, jax.numpy as jnp
from jax import lax
import numpy as np

D_MODEL = 1024
BATCH = 4
SEQ = 4096
DEPTH = 4
DEC_BATCH = 128
DEC_SEQ = 4
PAST_LEN = 2048
PAGE_SIZE = 128

HEAD_DIM = 64
N_HEADS = D_MODEL // 128
KV_HEADS = N_HEADS // 2
GROUP = N_HEADS // KV_HEADS
ATT_WIDTH = N_HEADS * HEAD_DIM
KV_WIDTH = KV_HEADS * HEAD_DIM
ATT_SCALE = HEAD_DIM ** -0.5
POOL_WINDOWS = (2, 4, 8, 16)
POOL_WIDTH = ATT_WIDTH
POOL_GROUP = POOL_WIDTH // len(POOL_WINDOWS)
POOL_HIST = max(POOL_WINDOWS) - 1
D_FF = 4 * D_MODEL
PLE_DIM = 256
ROPE_THETA = 500000.0
ROT_DIM = HEAD_DIM // 4
SB_Q_BLOCK = 128
MOBA_BLOCK = 256
MOBA_TOPK = 3
MOBA_Q_BLOCK = 64
N_BRANCH = 3
IN_SPLITS = (POOL_WIDTH, ATT_WIDTH, KV_WIDTH, KV_WIDTH, ATT_WIDTH, KV_WIDTH, KV_WIDTH, N_BRANCH * D_MODEL)
IN_WIDTH = sum(IN_SPLITS)
RMS_EPS = 1e-6

kernel_name = 'hybrid_pool_stickbreak_moba_decoder_step'


def rms_norm(x, g):
    xf = x.astype(jnp.float32)
    y = xf * lax.rsqrt(jnp.mean(xf * xf, axis=-1, keepdims=True) + RMS_EPS) * g.astype(jnp.float32)
    return y.astype(x.dtype)


def rope(x, pos):
    half = ROT_DIM // 2
    inv = ROPE_THETA ** (-jnp.arange(half, dtype=jnp.float32) / half)
    ang = pos.astype(jnp.float32)[:, None] * inv[None, :]
    cos = jnp.cos(ang)[None, :, None, :]
    sin = jnp.sin(ang)[None, :, None, :]
    xf = x.astype(jnp.float32)
    x1, x2, rest = xf[..., :half], xf[..., half:ROT_DIM], xf[..., ROT_DIM:]
    out = jnp.concatenate([x1 * cos - x2 * sin, x2 * cos + x1 * sin, rest], axis=-1)
    return out.astype(x.dtype)


def mixer_inputs(x, norm_g, w_in, q_norm_g, k_norm_g, pos):
    B, T, _ = x.shape
    h = rms_norm(x, norm_g)
    z = jnp.einsum('btd,de->bte', h, w_in)
    offs = np.cumsum(IN_SPLITS)[:-1].tolist()
    u, q_sb, k_sb, v_sb, q_mb, k_mb, v_mb, g = jnp.split(z, offs, axis=-1)
    heads = lambda a, n: a.reshape(B, T, n, HEAD_DIM)
    q_sb, k_sb, v_sb = heads(q_sb, N_HEADS), heads(k_sb, KV_HEADS), heads(v_sb, KV_HEADS)
    q_mb = rope(rms_norm(heads(q_mb, N_HEADS), q_norm_g), pos)
    k_mb = rope(rms_norm(heads(k_mb, KV_HEADS), k_norm_g), pos)
    v_mb = heads(v_mb, KV_HEADS)
    return u, q_sb, k_sb, v_sb, q_mb, k_mb, v_mb, g


def pool_mix(u_ext, start_pos, w_group, scale):
    B, L, C = u_ext.shape
    T = L - POOL_HIST
    uf = u_ext.astype(jnp.float32)
    cs = jnp.concatenate([jnp.zeros((B, 1, C), jnp.float32), lax.cumsum(uf, axis=1)], axis=1)
    pos = start_pos + jnp.arange(T)
    u_new = uf[:, POOL_HIST:]
    parts = []
    for g, w in enumerate(POOL_WINDOWS):
        lo, hi = g * POOL_GROUP, (g + 1) * POOL_GROUP
        win_sum = cs[:, POOL_HIST + 1:POOL_HIST + 1 + T, lo:hi] - cs[:, POOL_HIST + 1 - w:POOL_HIST + 1 - w + T, lo:hi]
        count = jnp.minimum(pos + 1, w).astype(jnp.float32)[None, :, None]
        parts.append(win_sum / count - u_new[..., lo:hi])
    d = jnp.stack(parts, axis=2)
    y = jnp.einsum('btgc,gce->btge', d, w_group.astype(jnp.float32)).reshape(B, T, C)
    return (y * scale.astype(jnp.float32)).astype(u_ext.dtype)


def stick_breaking(q, k, v, q_pos, k_pos):
    B, Tq = q.shape[:2]
    qg = q.reshape(B, Tq, KV_HEADS, GROUP, HEAD_DIM).astype(jnp.float32)
    z = jnp.einsum('bqkgd,bskd->bkgqs', qg, k.astype(jnp.float32)) * ATT_SCALE
    mask = k_pos[None, :] < q_pos[:, None]
    log_keep = jnp.where(mask, jax.nn.log_sigmoid(-z), 0.0)
    suffix = lax.cumsum(log_keep, axis=4, reverse=True)
    after = jnp.concatenate([suffix[..., 1:], jnp.zeros_like(suffix[..., :1])], axis=-1)
    w = jnp.where(mask, jnp.exp(jax.nn.log_sigmoid(z) + after), 0.0)
    out = jnp.einsum('bkgqs,bskd->bqkgd', w, v.astype(jnp.float32))
    return out.reshape(B, Tq, ATT_WIDTH).astype(q.dtype)


def sb_prompt(q, k, v):
    B, S = q.shape[:2]
    nqb = S // SB_Q_BLOCK
    q_blocks = q.reshape(B, nqb, SB_Q_BLOCK, N_HEADS, HEAD_DIM).transpose(1, 0, 2, 3, 4)
    k_pos = jnp.arange(S)

    def one(args):
        q_blk, start = args
        return stick_breaking(q_blk, k, v, start + jnp.arange(SB_Q_BLOCK), k_pos)

    out = lax.map(one, (q_blocks, jnp.arange(nqb) * SB_Q_BLOCK))
    return out.transpose(1, 0, 2, 3).reshape(B, S, ATT_WIDTH)


def moba_prepare(k, v):
    B, L = k.shape[:2]
    nb = -(-L // MOBA_BLOCK)
    pad = ((0, 0), (0, nb * MOBA_BLOCK - L), (0, 0), (0, 0))
    kb = jnp.pad(k, pad).reshape(B, nb, MOBA_BLOCK, KV_HEADS, HEAD_DIM)
    vb = jnp.pad(v, pad).reshape(B, nb, MOBA_BLOCK, KV_HEADS, HEAD_DIM)
    kmean = jnp.mean(kb.astype(jnp.float32), axis=2)
    return kb.transpose(0, 3, 1, 2, 4), vb.transpose(0, 3, 1, 2, 4), kmean


def moba_one(q, q_pos, kt, vt, kmean):
    Tq = q.shape[0]
    nb = kt.shape[1]
    n_sel = min(MOBA_TOPK, nb)
    head_kv = jnp.arange(N_HEADS) // GROUP
    qf = q.astype(jnp.float32)
    blk_scores = jnp.einsum('thd,nhd->thn', qf, kmean[:, head_kv])
    q_blk = q_pos // MOBA_BLOCK
    fully_past = jnp.arange(nb)[None, None, :] < q_blk[:, None, None]
    blk_scores = jnp.where(fully_past, blk_scores, -jnp.inf)
    _, top = lax.top_k(blk_scores, n_sel)
    own = jnp.broadcast_to(q_blk[:, None, None], (Tq, N_HEADS, 1))
    idx = jnp.concatenate([top, own], axis=-1)
    slot_ok = jnp.concatenate([top < q_blk[:, None, None], jnp.ones((Tq, N_HEADS, 1), bool)], axis=-1)
    hsel = head_kv[None, :, None]
    kg = kt[hsel, idx].astype(jnp.float32)
    vg = vt[hsel, idx].astype(jnp.float32)
    s = jnp.einsum('thd,thnkd->thnk', qf, kg) * ATT_SCALE
    k_pos = idx[..., None] * MOBA_BLOCK + jnp.arange(MOBA_BLOCK)
    ok = slot_ok[..., None] & (k_pos <= q_pos[:, None, None, None])
    s = jnp.where(ok, s, -jnp.inf)
    p = jax.nn.softmax(s.reshape(Tq, N_HEADS, -1), axis=-1).reshape(s.shape)
    o = jnp.einsum('thnk,thnkd->thd', p, vg)
    return o.reshape(Tq, ATT_WIDTH).astype(q.dtype)


def moba_prompt(q, k, v):
    B, S = q.shape[:2]
    kt, vt, kmean = moba_prepare(k, v)
    nqb = S // MOBA_Q_BLOCK
    q_blocks = q.reshape(B * nqb, MOBA_Q_BLOCK, N_HEADS, HEAD_DIM)
    seq_id = jnp.repeat(jnp.arange(B), nqb)
    start = jnp.tile(jnp.arange(nqb) * MOBA_Q_BLOCK, B)

    def one(args):
        q_blk, b, st = args
        return moba_one(q_blk, st + jnp.arange(MOBA_Q_BLOCK), kt[b], vt[b], kmean[b])

    out = lax.map(one, (q_blocks, seq_id, start))
    return out.reshape(B, S, ATT_WIDTH)


def moba_sample(q, k_all, v_all, q_pos):
    kt, vt, kmean = moba_prepare(k_all, v_all)
    return lax.map(lambda a: moba_one(a[0], q_pos, a[1], a[2], a[3]), (q, kt, vt, kmean))


def merge_branches(x, a, b, c, g, w_branch, w_out):
    gate = jax.nn.sigmoid(g.astype(jnp.float32)).astype(x.dtype)
    gate = gate.reshape(g.shape[0], g.shape[1], N_BRANCH, D_MODEL)
    branches = jnp.stack([a, b, c], axis=2)
    proj = jnp.einsum('btnc,ncd->btnd', branches, w_branch)
    m = jnp.sum(gate * proj, axis=2)
    return x + jnp.einsum('btd,de->bte', m, w_out)


def channel_and_ple(x, p, norm_mlp, w_up, w_down, norm_ple, w_ple_gate, w_ple):
    h = rms_norm(x, norm_mlp)
    x = x + jnp.einsum('btf,fd->btd', jnp.square(jax.nn.relu(jnp.einsum('btd,df->btf', h, w_up))), w_down)
    gate = jax.nn.sigmoid(jnp.einsum('btd,de->bte', rms_norm(x, norm_ple), w_ple_gate).astype(jnp.float32)).astype(x.dtype)
    return x + gate * jnp.einsum('btp,pd->btd', p, w_ple)


def gather_past(cache, layer, page_table):
    rows = cache[layer, page_table]
    return rows.reshape(page_table.shape[0], -1, KV_HEADS, HEAD_DIM)


def setup_inputs(seed: int = 0) -> dict:
    key = jax.random.key(seed)
    ks = jax.random.split(key, 24)
    n_pages = PAST_LEN // PAGE_SIZE
    n_used = DEC_BATCH * n_pages
    n_phys = n_used + n_used // 4
    cache_shape = (DEPTH, n_phys, PAGE_SIZE, KV_HEADS, HEAD_DIM)

    def nrm(k, shape, scale=1.0):
        return scale * jax.random.normal(k, shape, jnp.float32)

    page_table = jax.random.permutation(ks[9], n_phys)[:n_used].reshape(DEC_BATCH, n_pages).astype(jnp.int32)
    return {
        'x_prompt': nrm(ks[0], (BATCH, SEQ, D_MODEL)),
        'x_sample': nrm(ks[1], (DEC_BATCH, DEC_SEQ, D_MODEL)),
        'p_prompt': nrm(ks[2], (DEPTH, BATCH, SEQ, PLE_DIM)),
        'p_sample': nrm(ks[3], (DEPTH, DEC_BATCH, DEC_SEQ, PLE_DIM)),
        'cache_sb_k': nrm(ks[4], cache_shape),
        'cache_sb_v': nrm(ks[5], cache_shape),
        'cache_moba_k': nrm(ks[6], cache_shape),
        'cache_moba_v': nrm(ks[7], cache_shape),
        'state_pool': nrm(ks[8], (DEPTH, DEC_BATCH, POOL_HIST, POOL_WIDTH)),
        'page_table': page_table,
        'norm_mix': 1.0 + nrm(ks[10], (DEPTH, D_MODEL), 0.02),
        'w_in': nrm(ks[11], (DEPTH, D_MODEL, IN_WIDTH), D_MODEL ** -0.5),
        'pool_w': nrm(ks[12], (DEPTH, len(POOL_WINDOWS), POOL_GROUP, POOL_GROUP), POOL_GROUP ** -0.5),
        'pool_scale': 1.0 + nrm(ks[13], (DEPTH, POOL_WIDTH), 0.02),
        'moba_q_norm': 1.0 + nrm(ks[14], (DEPTH, HEAD_DIM), 0.02),
        'moba_k_norm': 1.0 + nrm(ks[15], (DEPTH, HEAD_DIM), 0.02),
        'w_branch': nrm(ks[16], (DEPTH, N_BRANCH, ATT_WIDTH, D_MODEL), ATT_WIDTH ** -0.5),
        'w_out': nrm(ks[17], (DEPTH, D_MODEL, D_MODEL), D_MODEL ** -0.5),
        'norm_mlp': 1.0 + nrm(ks[18], (DEPTH, D_MODEL), 0.02),
        'w_up': nrm(ks[19], (DEPTH, D_MODEL, D_FF), D_MODEL ** -0.5),
        'w_down': nrm(ks[20], (DEPTH, D_FF, D_MODEL), D_FF ** -0.5),
        'norm_ple': 1.0 + nrm(ks[21], (DEPTH, D_MODEL), 0.02),
        'w_ple_gate': nrm(ks[22], (DEPTH, D_MODEL, D_MODEL), D_MODEL ** -0.5),
        'w_ple': nrm(ks[23], (DEPTH, PLE_DIM, D_MODEL), PLE_DIM ** -0.5),
    }


def reference(x_prompt, x_sample, p_prompt, p_sample, cache_sb_k, cache_sb_v, cache_moba_k, cache_moba_v,
              state_pool, page_table, norm_mix, w_in, pool_w, pool_scale, moba_q_norm, moba_k_norm,
              w_branch, w_out, norm_mlp, w_up, w_down, norm_ple, w_ple_gate, w_ple):
    pos_p = jnp.arange(SEQ)
    pos_s = PAST_LEN + jnp.arange(DEC_SEQ)
    k_pos_s = jnp.arange(PAST_LEN + DEC_SEQ)
    xp, xs = x_prompt, x_sample
    sbk_p, sbv_p, mbk_p, mbv_p, pool_p = [], [], [], [], []
    sbk_s, sbv_s, mbk_s, mbv_s, pool_s = [], [], [], [], []
    for i in range(DEPTH):
        u, q_sb, k_sb, v_sb, q_mb, k_mb, v_mb, g = mixer_inputs(xp, norm_mix[i], w_in[i], moba_q_norm[i], moba_k_norm[i], pos_p)
        u_ext = jnp.pad(u, ((0, 0), (POOL_HIST, 0), (0, 0)))
        a = pool_mix(u_ext, 0, pool_w[i], pool_scale[i])
        b = sb_prompt(q_sb, k_sb, v_sb)
        c = moba_prompt(q_mb, k_mb, v_mb)
        xp = merge_branches(xp, a, b, c, g, w_branch[i], w_out[i])
        xp = channel_and_ple(xp, p_prompt[i], norm_mlp[i], w_up[i], w_down[i], norm_ple[i], w_ple_gate[i], w_ple[i])
        sbk_p.append(k_sb); sbv_p.append(v_sb); mbk_p.append(k_mb); mbv_p.append(v_mb)
        pool_p.append(u_ext[:, -POOL_HIST:])

        u, q_sb, k_sb, v_sb, q_mb, k_mb, v_mb, g = mixer_inputs(xs, norm_mix[i], w_in[i], moba_q_norm[i], moba_k_norm[i], pos_s)
        u_ext = jnp.concatenate([state_pool[i].astype(u.dtype), u], axis=1)
        a = pool_mix(u_ext, PAST_LEN, pool_w[i], pool_scale[i])
        k_all = jnp.concatenate([gather_past(cache_sb_k, i, page_table).astype(k_sb.dtype), k_sb], axis=1)
        v_all = jnp.concatenate([gather_past(cache_sb_v, i, page_table).astype(v_sb.dtype), v_sb], axis=1)
        b = stick_breaking(q_sb, k_all, v_all, pos_s, k_pos_s)
        mk_all = jnp.concatenate([gather_past(cache_moba_k, i, page_table).astype(k_mb.dtype), k_mb], axis=1)
        mv_all = jnp.concatenate([gather_past(cache_moba_v, i, page_table).astype(v_mb.dtype), v_mb], axis=1)
        c = moba_sample(q_mb, mk_all, mv_all, pos_s)
        xs = merge_branches(xs, a, b, c, g, w_branch[i], w_out[i])
        xs = channel_and_ple(xs, p_sample[i], norm_mlp[i], w_up[i], w_down[i], norm_ple[i], w_ple_gate[i], w_ple[i])
        sbk_s.append(k_sb); sbv_s.append(v_sb); mbk_s.append(k_mb); mbv_s.append(v_mb)
        pool_s.append(u_ext[:, -POOL_HIST:])

    y_prompt, y_sample = xp, xs
    sb_k_prompt, sb_v_prompt = jnp.stack(sbk_p), jnp.stack(sbv_p)
    moba_k_prompt, moba_v_prompt = jnp.stack(mbk_p), jnp.stack(mbv_p)
    pool_prompt = jnp.stack(pool_p)
    sb_k_sample, sb_v_sample = jnp.stack(sbk_s), jnp.stack(sbv_s)
    moba_k_sample, moba_v_sample = jnp.stack(mbk_s), jnp.stack(mbv_s)
    pool_sample = jnp.stack(pool_s)
    return (y_prompt, y_sample, sb_k_prompt, sb_v_prompt, moba_k_prompt, moba_v_prompt, pool_prompt,
            sb_k_sample, sb_v_sample, moba_k_sample, moba_v_sample, pool_sample)
```

```python
import functools

import jax
import jax.numpy as jnp
import numpy as np
from jax import lax
from jax.experimental import pallas as pl
from jax.experimental.pallas import tpu as pltpu

F32 = jnp.float32
BF16 = jnp.bfloat16

HEAD_DIM = 64
KV_HEADS = 4
GROUP = 2
N_HEADS = KV_HEADS * GROUP
ATT_WIDTH = N_HEADS * HEAD_DIM
KV_WIDTH = KV_HEADS * HEAD_DIM
ATT_SCALE = HEAD_DIM ** -0.5
POOL_WINDOWS = (2, 4, 8, 16)
POOL_GROUP = 128
POOL_HIST = max(POOL_WINDOWS) - 1
ROPE_THETA = 500000.0
ROT_DIM = HEAD_DIM // 4
MOBA_BLOCK = 256
MOBA_TOPK = 3
N_BRANCH = 3
RMS_EPS = 1e-6
NEG_INF = float("-inf")

LANES = 128
SB_BLOCK = 128
KMEAN_ROWS = 128
VMEM_LIMIT = 56 * 1024 * 1024


def _cparams(n_axes, vmem=None):
    return pltpu.CompilerParams(dimension_semantics=("arbitrary",) * n_axes, vmem_limit_bytes=vmem)


def _mm(a, b):
    return jnp.dot(a, b, preferred_element_type=F32)


def _mm_nt(a, b):
    return lax.dot_general(a, b, (((1,), (1,)), ((), ())), preferred_element_type=F32)


def _split2(x):
    hi = x.astype(BF16)
    lo = (x - hi.astype(F32)).astype(BF16)
    return hi, lo


def _split3(x):
    hi = x.astype(BF16)
    r = x - hi.astype(F32)
    mid = r.astype(BF16)
    lo = (r - mid.astype(F32)).astype(BF16)
    return hi, mid, lo


def _sigmoid(x):
    return 1.0 / (1.0 + jnp.exp(-x))


def _inproj_kernel(x_ref, gn_ref, w_ref, qn_ref, kn_ref, cos_ref, sin_ref, gmat_ref,
                   u_ref, qsb_ref, ksb_ref, vsb_ref, ksbb_ref, vsbb_ref,
                   qmf_ref, qmb_ref, kmb_ref, vmb_ref, kmbb_ref, vmbb_ref, gate_ref):
    x = x_ref[...]
    ms = jnp.mean(x * x, axis=-1, keepdims=True)
    h = (x * lax.rsqrt(ms + RMS_EPS) * gn_ref[...]).astype(BF16)

    def proj(lo, hi):
        return _mm(h, w_ref[:, lo:hi])

    def head_norm_rope(y, g_ref):
        width = y.shape[1]
        gm = gmat_ref[0:width, 0:width]
        p0, p1, p2 = _split3(y * y)
        ms_h = (_mm(p0, gm) + _mm(p1, gm) + _mm(p2, gm)) * (1.0 / HEAD_DIM)
        yn = y * lax.rsqrt(ms_h + RMS_EPS) * g_ref[...]
        reps = width // LANES
        c = jnp.concatenate([cos_ref[...]] * reps, axis=1)
        s = jnp.concatenate([sin_ref[...]] * reps, axis=1)
        lane = lax.broadcasted_iota(jnp.int32, yn.shape, 1)
        first_half = (lane & (HEAD_DIM - 1)) < (ROT_DIM // 2)
        partner = jnp.where(first_half, pltpu.roll(yn, width - ROT_DIM // 2, 1), pltpu.roll(yn, ROT_DIM // 2, 1))
        return yn * c + partner * s

    o = 0
    u_ref[...] = proj(o, o + ATT_WIDTH)
    o += ATT_WIDTH
    qsb_ref[...] = (proj(o, o + ATT_WIDTH) * ATT_SCALE).astype(BF16)
    o += ATT_WIDTH
    k = proj(o, o + KV_WIDTH)
    ksb_ref[...] = k
    ksbb_ref[...] = k.astype(BF16)
    o += KV_WIDTH
    v = proj(o, o + KV_WIDTH)
    vsb_ref[...] = v
    vsbb_ref[...] = v.astype(BF16)
    o += KV_WIDTH
    q = head_norm_rope(proj(o, o + ATT_WIDTH), qn_ref)
    qmf_ref[...] = q
    qmb_ref[...] = (q * ATT_SCALE).astype(BF16)
    o += ATT_WIDTH
    k = head_norm_rope(proj(o, o + KV_WIDTH), kn_ref)
    kmb_ref[...] = k
    kmbb_ref[...] = k.astype(BF16)
    o += KV_WIDTH
    v = proj(o, o + KV_WIDTH)
    vmb_ref[...] = v
    vmbb_ref[...] = v.astype(BF16)
    o += KV_WIDTH
    gate_w = gate_ref.shape[1]
    chunk = 512
    for c0 in range(0, gate_w, chunk):
        gate_ref[:, c0:c0 + chunk] = _sigmoid(proj(o + c0, o + c0 + chunk))


def _inproj(x, gn, w, qn, kn, cos_t, sin_t, gmat, tm):
    n, d = x.shape
    in_w = w.shape[1]
    gate_w = in_w - (3 * ATT_WIDTH + 4 * KV_WIDTH)
    nper = cos_t.shape[0] // tm
    row = lambda i: (i, 0)
    const = lambda i: (0, 0)

    def out(width, dtype):
        return jax.ShapeDtypeStruct((n, width), dtype), pl.BlockSpec((tm, width), row)

    outs = [out(ATT_WIDTH, F32), out(ATT_WIDTH, BF16), out(KV_WIDTH, F32), out(KV_WIDTH, F32),
            out(KV_WIDTH, BF16), out(KV_WIDTH, BF16), out(ATT_WIDTH, F32), out(ATT_WIDTH, BF16),
            out(KV_WIDTH, F32), out(KV_WIDTH, F32), out(KV_WIDTH, BF16), out(KV_WIDTH, BF16),
            out(gate_w, F32)]
    return pl.pallas_call(
        _inproj_kernel,
        grid=(n // tm,),
        in_specs=[pl.BlockSpec((tm, d), row), pl.BlockSpec((1, d), const), pl.BlockSpec((d, in_w), const),
                  pl.BlockSpec((1, ATT_WIDTH), const), pl.BlockSpec((1, KV_WIDTH), const),
                  pl.BlockSpec((tm, LANES), lambda i: (i % nper, 0)), pl.BlockSpec((tm, LANES), lambda i: (i % nper, 0)),
                  pl.BlockSpec((ATT_WIDTH, ATT_WIDTH), const)],
        out_specs=[s for _, s in outs],
        out_shape=[s for s, _ in outs],
        compiler_params=_cparams(1, VMEM_LIMIT),
        name="inproj",
    )(x, gn, w, qn, kn, cos_t, sin_t, gmat)


def _pool_prompt_kernel(cur_ref, prev_ref, pw_ref, ps_ref, o_ref, ext_ref, *, ts):
    t = pl.program_id(1)
    hist = ext_ref.shape[0] - ts
    ext_ref[0:hist, :] = jnp.where(t == 0, 0.0, prev_ref[0])
    ext_ref[hist:, :] = cur_ref[0]
    pos = t * ts + lax.broadcasted_iota(jnp.int32, (ts, POOL_GROUP), 0)
    for g, w in enumerate(POOL_WINDOWS):
        lo, hi = g * POOL_GROUP, (g + 1) * POOL_GROUP
        ws = ext_ref[hist:hist + ts, lo:hi]
        for k in range(1, w):
            ws = ws + ext_ref[hist - k:hist - k + ts, lo:hi]
        cnt = jnp.minimum(pos + 1, w).astype(F32)
        dlt = ws / cnt - ext_ref[hist:hist + ts, lo:hi]
        o_ref[0, :, lo:hi] = _mm(dlt.astype(BF16), pw_ref[g]) * ps_ref[:, lo:hi]


def _pool_prompt(u, pw, ps, ts):
    b, s, c = u.shape
    hist = 16
    ratio = ts // hist
    return pl.pallas_call(
        functools.partial(_pool_prompt_kernel, ts=ts),
        grid=(b, s // ts),
        in_specs=[pl.BlockSpec((1, ts, c), lambda i, t: (i, t, 0)),
                  pl.BlockSpec((1, hist, c), lambda i, t: (i, jnp.maximum(t * ratio - 1, 0), 0)),
                  pl.BlockSpec((len(POOL_WINDOWS), POOL_GROUP, POOL_GROUP), lambda i, t: (0, 0, 0)),
                  pl.BlockSpec((1, c), lambda i, t: (0, 0))],
        out_specs=pl.BlockSpec((1, ts, c), lambda i, t: (i, t, 0)),
        out_shape=jax.ShapeDtypeStruct((b, s, c), F32),
        scratch_shapes=[pltpu.VMEM((ts + hist, c), F32)],
        compiler_params=_cparams(2),
        name="pool_prompt",
    )(u, u, pw, ps)


def _pool_sample_kernel(st_ref, u_ref, pw_ref, ps_ref, o_ref, *, start_pos):
    n_new = u_ref.shape[0]

    def row(idx, lo, hi):
        return st_ref[idx, :, lo:hi] if idx < POOL_HIST else u_ref[idx - POOL_HIST, :, lo:hi]

    for j in range(n_new):
        for g, w in enumerate(POOL_WINDOWS):
            lo, hi = g * POOL_GROUP, (g + 1) * POOL_GROUP
            ws = row(POOL_HIST + j, lo, hi)
            for k in range(1, w):
                ws = ws + row(POOL_HIST + j - k, lo, hi)
            cnt = float(min(start_pos + j + 1, w))
            dlt = ws / cnt - row(POOL_HIST + j, lo, hi)
            o_ref[j, :, lo:hi] = _mm(dlt.astype(BF16), pw_ref[g]) * ps_ref[:, lo:hi]


def _pool_sample(st_t, u_t, pw, ps, start_pos):
    t, b, c = u_t.shape
    full = lambda shape: pl.BlockSpec(shape, lambda i: (0,) * len(shape))
    return pl.pallas_call(
        functools.partial(_pool_sample_kernel, start_pos=start_pos),
        grid=(1,),
        in_specs=[full(st_t.shape), full(u_t.shape), full(pw.shape), full(ps.shape)],
        out_specs=full((t, b, c)),
        out_shape=jax.ShapeDtypeStruct((t, b, c), F32),
        compiler_params=_cparams(1),
        name="pool_sample",
    )(st_t, u_t, pw, ps)


def _sb_block(qz, kb, vb, umat, valid, acc_ref, carry_ref):
    tk = kb.shape[0]
    z = _mm_nt(qz, kb)
    nz = -z
    t = jnp.log(1.0 + jnp.exp(jnp.minimum(z, nz)))
    log_keep = jnp.minimum(nz, 0.0) - t
    log_take = jnp.minimum(z, 0.0) - t
    if valid is not None:
        log_keep = jnp.where(valid, log_keep, 0.0)
    hi, lo = _split2(log_keep)
    cs = _mm(hi, umat) + _mm(lo, umat)
    w = jnp.exp(log_take + cs[:, :tk] + carry_ref[...])
    if valid is not None:
        w = jnp.where(valid, w, 0.0)
    acc_ref[...] += _mm(w.astype(BF16), vb)
    carry_ref[...] += cs[:, tk:]


def _sb_prompt_kernel(q_ref, k_ref, v_ref, u_ref, o_ref, qz_ref, acc_ref, carry_ref, *, tq):
    i = pl.program_id(1)
    rows = GROUP * tq
    row = lax.broadcasted_iota(jnp.int32, (rows, tq), 0)
    col = lax.broadcasted_iota(jnp.int32, (rows, tq), 1)
    diag_valid = col < (row & (tq - 1))
    lane_head = lax.broadcasted_iota(jnp.int32, (tq, KV_WIDTH), 1) >> 6
    umat = u_ref[...]

    def kv_block(j):
        start = pl.multiple_of(j * tq, tq)
        return k_ref[0, pl.ds(start, tq), :], v_ref[0, pl.ds(start, tq), :]

    for kvh in range(KV_HEADS):
        in_head = lane_head == kvh
        for g in range(GROUP):
            qg = q_ref[0, :, g * KV_WIDTH:(g + 1) * KV_WIDTH]
            qz_ref[g * tq:(g + 1) * tq, :] = jnp.where(in_head, qg, jnp.zeros_like(qg))
        acc_ref[...] = jnp.zeros_like(acc_ref)
        carry_ref[...] = jnp.zeros_like(carry_ref)
        kb, vb = kv_block(i)
        _sb_block(qz_ref[...], kb, vb, umat, diag_valid, acc_ref, carry_ref)

        def body(t, c):
            kb, vb = kv_block(i - 1 - t)
            _sb_block(qz_ref[...], kb, vb, umat, None, acc_ref, carry_ref)
            return c

        lax.fori_loop(0, i, body, 0)
        for g in range(GROUP):
            sl = slice(g * KV_WIDTH, (g + 1) * KV_WIDTH)
            acc_g = acc_ref[g * tq:(g + 1) * tq, :]
            if kvh == 0:
                o_ref[0, :, sl] = jnp.where(in_head, acc_g, 0.0)
            else:
                o_ref[0, :, sl] = jnp.where(in_head, acc_g, o_ref[0, :, sl])


def _suffix_matrix(tk):
    r = np.arange(tk)
    strict = (r[:, None] > r[None, :]).astype(np.float32)
    return jnp.asarray(np.concatenate([strict, np.ones((tk, tk), np.float32)], axis=1), dtype=BF16)


def _sb_prompt(q, k, v):
    b, s, _ = q.shape
    tq = SB_BLOCK
    return pl.pallas_call(
        functools.partial(_sb_prompt_kernel, tq=tq),
        grid=(b, s // tq),
        in_specs=[pl.BlockSpec((1, tq, ATT_WIDTH), lambda i, j: (i, j, 0)),
                  pl.BlockSpec((1, s, KV_WIDTH), lambda i, j: (i, 0, 0)),
                  pl.BlockSpec((1, s, KV_WIDTH), lambda i, j: (i, 0, 0)),
                  pl.BlockSpec((tq, 2 * tq), lambda i, j: (0, 0))],
        out_specs=pl.BlockSpec((1, tq, ATT_WIDTH), lambda i, j: (i, j, 0)),
        out_shape=jax.ShapeDtypeStruct((b, s, ATT_WIDTH), F32),
        scratch_shapes=[pltpu.VMEM((GROUP * tq, KV_WIDTH), BF16),
                        pltpu.VMEM((GROUP * tq, KV_WIDTH), F32),
                        pltpu.VMEM((GROUP * tq, tq), F32)],
        compiler_params=_cparams(2),
        name="sb_prompt",
    )(q, k, v, _suffix_matrix(tq))


def _sample_queries(q8):
    rows = KV_HEADS * q8.shape[0]
    q = jnp.concatenate([q8] * KV_HEADS, axis=0)
    row = lax.broadcasted_iota(jnp.int32, (rows, KV_WIDTH), 0)
    lane = lax.broadcasted_iota(jnp.int32, (rows, KV_WIDTH), 1)
    return jnp.where((row >> 3) == (lane >> 6), q, 0.0)


def _sample_heads_out(acc):
    n = acc.shape[0] // KV_HEADS
    lane_head = lax.broadcasted_iota(jnp.int32, (n, KV_WIDTH), 1) >> 6
    out = jnp.zeros((n, KV_WIDTH), F32)
    for kvh in range(KV_HEADS):
        out = jnp.where(lane_head == kvh, acc[kvh * n:(kvh + 1) * n, :], out)
    return out


def _sb_sample_kernel(pt_ref, q_ref, kn_ref, vn_ref, u_ref, *refs, npages, n_new):
    del pt_ref
    kp, vp = refs[:npages], refs[npages:2 * npages]
    o_ref, padk_ref, padv_ref, acc_ref, carry_ref = refs[2 * npages:]
    page = padk_ref.shape[0]
    qz = _sample_queries(q_ref[0]).astype(BF16)
    rows = qz.shape[0]
    umat = u_ref[...]
    acc_ref[...] = jnp.zeros_like(acc_ref)
    carry_ref[...] = jnp.zeros_like(carry_ref)
    padk_ref[...] = jnp.zeros_like(padk_ref)
    padv_ref[...] = jnp.zeros_like(padv_ref)
    padk_ref[0:n_new, :] = kn_ref[0]
    padv_ref[0:n_new, :] = vn_ref[0]
    row = lax.broadcasted_iota(jnp.int32, (rows, page), 0)
    col = lax.broadcasted_iota(jnp.int32, (rows, page), 1)
    valid_new = col < (row & (n_new - 1))
    _sb_block(qz, padk_ref[...].astype(BF16), padv_ref[...].astype(BF16), umat, valid_new, acc_ref, carry_ref)
    for p in reversed(range(npages)):
        _sb_block(qz, kp[p][...].astype(BF16), vp[p][...].astype(BF16), umat, None, acc_ref, carry_ref)
    o_ref[0] = _sample_heads_out(acc_ref[...])


def _page_specs(layer, npages, page):
    def spec(p):
        return pl.BlockSpec((None, None, page, KV_WIDTH), lambda b, pt: (layer, pt[b, p], 0, 0))
    return [spec(p) for p in range(npages)]


def _sb_sample(q8, k_new, v_new, cache_k, cache_v, page_table, layer):
    bs, npages = page_table.shape
    page = cache_k.shape[2]
    n_new = k_new.shape[1]
    rows = KV_HEADS * q8.shape[1]
    grid_spec = pltpu.PrefetchScalarGridSpec(
        num_scalar_prefetch=1,
        grid=(bs,),
        in_specs=[pl.BlockSpec((1, q8.shape[1], KV_WIDTH), lambda b, pt: (b, 0, 0)),
                  pl.BlockSpec((1, n_new, KV_WIDTH), lambda b, pt: (b, 0, 0)),
                  pl.BlockSpec((1, n_new, KV_WIDTH), lambda b, pt: (b, 0, 0)),
                  pl.BlockSpec((page, 2 * page), lambda b, pt: (0, 0))]
                 + _page_specs(layer, npages, page) + _page_specs(layer, npages, page),
        out_specs=pl.BlockSpec((1, q8.shape[1], KV_WIDTH), lambda b, pt: (b, 0, 0)),
        scratch_shapes=[pltpu.VMEM((page, KV_WIDTH), F32), pltpu.VMEM((page, KV_WIDTH), F32),
                        pltpu.VMEM((rows, KV_WIDTH), F32), pltpu.VMEM((rows, page), F32)])
    return pl.pallas_call(
        functools.partial(_sb_sample_kernel, npages=npages, n_new=n_new),
        grid_spec=grid_spec,
        out_shape=jax.ShapeDtypeStruct((bs, q8.shape[1], KV_WIDTH), F32),
        compiler_params=_cparams(1),
        name="sb_sample",
    )(page_table, q8, k_new, v_new, _suffix_matrix(page), *([cache_k] * npages), *([cache_v] * npages))


def _block_scores(qf, km_hi, km_lo):
    q_hi, q_lo = _split2(qf)
    return _mm_nt(q_hi, km_hi) + (_mm_nt(q_hi, km_lo) + _mm_nt(q_lo, km_hi))


def _topk_select(scores, n_valid, nb):
    lane = lax.broadcasted_iota(jnp.int32, scores.shape, 1)
    valid = lane < n_valid
    s = jnp.where(valid, scores, NEG_INF)
    rank = jnp.zeros(scores.shape, jnp.int32)
    for m in range(nb):
        col = s[:, m:m + 1]
        beats = (col > s) | ((col == s) & (lane > m))
        rank = rank + jnp.where(beats, 1, 0)
    return jnp.where(valid & (rank < MOBA_TOPK), 1.0, 0.0)


def _kmean_kernel(k_ref, o_ref, *, nb):
    o_ref[...] = jnp.zeros_like(o_ref)
    for n in range(nb):
        blk = k_ref[0, n * MOBA_BLOCK:(n + 1) * MOBA_BLOCK, :]
        o_ref[0, n:n + 1, :] = jnp.sum(blk, axis=0, keepdims=True) * (1.0 / MOBA_BLOCK)


def _kmean(k):
    b, s, _ = k.shape
    nb = s // MOBA_BLOCK
    return pl.pallas_call(
        functools.partial(_kmean_kernel, nb=nb),
        grid=(b,),
        in_specs=[pl.BlockSpec((1, s, KV_WIDTH), lambda i: (i, 0, 0))],
        out_specs=pl.BlockSpec((1, KMEAN_ROWS, KV_WIDTH), lambda i: (i, 0, 0)),
        out_shape=jax.ShapeDtypeStruct((b, KMEAN_ROWS, KV_WIDTH), F32),
        compiler_params=_cparams(1),
        name="moba_kmean",
    )(k)


def _moba_prompt_kernel(qf_ref, qb_ref, k_ref, v_ref, km_ref, e_ref, o_ref,
                        qz_ref, sel_ref, m_ref, l_ref, acc_ref, *, nb):
    qblk = pl.program_id(1)
    tq = MOBA_BLOCK
    rows = GROUP * tq
    lane_head = lax.broadcasted_iota(jnp.int32, (tq, KV_WIDTH), 1) >> 6
    row = lax.broadcasted_iota(jnp.int32, (rows, tq), 0)
    col = lax.broadcasted_iota(jnp.int32, (rows, tq), 1)
    causal = col <= (row & (tq - 1))
    km_hi, km_lo = _split2(km_ref[0])

    def kv_block(n):
        start = pl.multiple_of(n * tq, tq)
        return k_ref[0, pl.ds(start, tq), :], v_ref[0, pl.ds(start, tq), :]

    for kvh in range(KV_HEADS):
        in_head = lane_head == kvh
        for g in range(GROUP):
            sl = slice(g * KV_WIDTH, (g + 1) * KV_WIDTH)
            scores = _block_scores(jnp.where(in_head, qf_ref[0, :, sl], 0.0), km_hi, km_lo)
            sel = _topk_select(scores, qblk, nb)
            sel_ref[g * tq:(g + 1) * tq, :] = sel.astype(BF16)
            qg = qb_ref[0, :, sl]
            qz_ref[g * tq:(g + 1) * tq, :] = jnp.where(in_head, qg, jnp.zeros_like(qg))
        kb, vb = kv_block(qblk)
        s = jnp.where(causal, _mm_nt(qz_ref[...], kb), NEG_INF)
        m0 = jnp.max(s, axis=1, keepdims=True)
        p = jnp.exp(s - m0)
        m_ref[...] = m0
        l_ref[...] = jnp.sum(p, axis=1, keepdims=True)
        acc_ref[...] = _mm(p.astype(BF16), vb)

        def body(n, c):
            kb, vb = kv_block(n)
            picked = _mm(sel_ref[...], e_ref[n]) > 0.5
            s = jnp.where(picked, _mm_nt(qz_ref[...], kb), NEG_INF)
            m_old = m_ref[...]
            m_new = jnp.maximum(m_old, jnp.max(s, axis=1, keepdims=True))
            alpha = jnp.exp(m_old - m_new)
            p = jnp.exp(s - m_new)
            l_ref[...] = alpha * l_ref[...] + jnp.sum(p, axis=1, keepdims=True)
            acc_ref[...] = alpha * acc_ref[...] + _mm(p.astype(BF16), vb)
            m_ref[...] = m_new
            return c

        lax.fori_loop(0, qblk, body, 0)
        res = acc_ref[...] / l_ref[...]
        for g in range(GROUP):
            sl = slice(g * KV_WIDTH, (g + 1) * KV_WIDTH)
            res_g = res[g * tq:(g + 1) * tq, :]
            if kvh == 0:
                o_ref[0, :, sl] = jnp.where(in_head, res_g, 0.0)
            else:
                o_ref[0, :, sl] = jnp.where(in_head, res_g, o_ref[0, :, sl])


def _block_expander(nb):
    e = np.zeros((nb, LANES, MOBA_BLOCK), np.float32)
    for n in range(nb):
        e[n, n, :] = 1.0
    return jnp.asarray(e, dtype=BF16)


def _moba_prompt(qf, qb, k, v, kmean):
    b, s, _ = qf.shape
    tq = MOBA_BLOCK
    nb = s // tq
    rows = GROUP * tq
    return pl.pallas_call(
        functools.partial(_moba_prompt_kernel, nb=nb),
        grid=(b, nb),
        in_specs=[pl.BlockSpec((1, tq, ATT_WIDTH), lambda i, j: (i, j, 0)),
                  pl.BlockSpec((1, tq, ATT_WIDTH), lambda i, j: (i, j, 0)),
                  pl.BlockSpec((1, s, KV_WIDTH), lambda i, j: (i, 0, 0)),
                  pl.BlockSpec((1, s, KV_WIDTH), lambda i, j: (i, 0, 0)),
                  pl.BlockSpec((1, KMEAN_ROWS, KV_WIDTH), lambda i, j: (i, 0, 0)),
                  pl.BlockSpec((nb, LANES, tq), lambda i, j: (0, 0, 0))],
        out_specs=pl.BlockSpec((1, tq, ATT_WIDTH), lambda i, j: (i, j, 0)),
        out_shape=jax.ShapeDtypeStruct((b, s, ATT_WIDTH), F32),
        scratch_shapes=[pltpu.VMEM((rows, KV_WIDTH), BF16), pltpu.VMEM((rows, LANES), BF16),
                        pltpu.VMEM((rows, 1), F32), pltpu.VMEM((rows, 1), F32),
                        pltpu.VMEM((rows, KV_WIDTH), F32)],
        compiler_params=_cparams(2),
        name="moba_prompt",
    )(qf, qb, k, v, kmean, _block_expander(nb))


def _moba_sample_kernel(pt_ref, q_ref, kn_ref, vn_ref, *refs, npages, n_new):
    del pt_ref
    kp, vp = refs[:npages], refs[npages:2 * npages]
    o_ref, padk_ref, padv_ref, km_ref = refs[2 * npages:]
    page = padk_ref.shape[0]
    per_blk = MOBA_BLOCK // page
    nb_past = npages // per_blk
    qf = _sample_queries(q_ref[0])
    qz = (qf * ATT_SCALE).astype(BF16)
    rows = qz.shape[0]
    km_ref[...] = jnp.zeros_like(km_ref)
    for n in range(nb_past):
        tot = jnp.sum(kp[n * per_blk][...], axis=0, keepdims=True)
        for r in range(1, per_blk):
            tot = tot + jnp.sum(kp[n * per_blk + r][...], axis=0, keepdims=True)
        km_ref[n:n + 1, :] = tot * (1.0 / MOBA_BLOCK)
    km_hi, km_lo = _split2(km_ref[...])
    sel = _topk_select(_block_scores(qf, km_hi, km_lo), nb_past, nb_past)
    padk_ref[...] = jnp.zeros_like(padk_ref)
    padv_ref[...] = jnp.zeros_like(padv_ref)
    padk_ref[0:n_new, :] = kn_ref[0]
    padv_ref[0:n_new, :] = vn_ref[0]
    row = lax.broadcasted_iota(jnp.int32, (rows, page), 0)
    col = lax.broadcasted_iota(jnp.int32, (rows, page), 1)
    visible = (col <= (row & (n_new - 1))) & (col < n_new)
    s = jnp.where(visible, _mm_nt(qz, padk_ref[...].astype(BF16)), NEG_INF)
    m = jnp.max(s, axis=1, keepdims=True)
    p = jnp.exp(s - m)
    l = jnp.sum(p, axis=1, keepdims=True)
    acc = _mm(p.astype(BF16), padv_ref[...].astype(BF16))
    for pg in range(npages):
        n = pg // per_blk
        s = jnp.where(sel[:, n:n + 1] > 0.5, _mm_nt(qz, kp[pg][...].astype(BF16)), NEG_INF)
        m_new = jnp.maximum(m, jnp.max(s, axis=1, keepdims=True))
        alpha = jnp.exp(m - m_new)
        p = jnp.exp(s - m_new)
        l = alpha * l + jnp.sum(p, axis=1, keepdims=True)
        acc = alpha * acc + _mm(p.astype(BF16), vp[pg][...].astype(BF16))
        m = m_new
    o_ref[0] = _sample_heads_out(acc / l)


def _moba_sample(q8, k_new, v_new, cache_k, cache_v, page_table, layer):
    bs, npages = page_table.shape
    page = cache_k.shape[2]
    n_new = k_new.shape[1]
    grid_spec = pltpu.PrefetchScalarGridSpec(
        num_scalar_prefetch=1,
        grid=(bs,),
        in_specs=[pl.BlockSpec((1, q8.shape[1], KV_WIDTH), lambda b, pt: (b, 0, 0)),
                  pl.BlockSpec((1, n_new, KV_WIDTH), lambda b, pt: (b, 0, 0)),
                  pl.BlockSpec((1, n_new, KV_WIDTH), lambda b, pt: (b, 0, 0))]
                 + _page_specs(layer, npages, page) + _page_specs(layer, npages, page),
        out_specs=pl.BlockSpec((1, q8.shape[1], KV_WIDTH), lambda b, pt: (b, 0, 0)),
        scratch_shapes=[pltpu.VMEM((page, KV_WIDTH), F32), pltpu.VMEM((page, KV_WIDTH), F32),
                        pltpu.VMEM((KMEAN_ROWS, KV_WIDTH), F32)])
    return pl.pallas_call(
        functools.partial(_moba_sample_kernel, npages=npages, n_new=n_new),
        grid_spec=grid_spec,
        out_shape=jax.ShapeDtypeStruct((bs, q8.shape[1], KV_WIDTH), F32),
        compiler_params=_cparams(1),
        name="moba_sample",
    )(page_table, q8, k_new, v_new, *([cache_k] * npages), *([cache_v] * npages))


def _merge_kernel(x_ref, a_ref, b_ref, c_ref, g_ref, wb_ref, wo_ref, o_ref):
    d = x_ref.shape[1]
    m = g_ref[:, 0:d] * _mm(a_ref[...].astype(BF16), wb_ref[0])
    m = m + g_ref[:, d:2 * d] * _mm(b_ref[...].astype(BF16), wb_ref[1])
    m = m + g_ref[:, 2 * d:3 * d] * _mm(c_ref[...].astype(BF16), wb_ref[2])
    o_ref[...] = x_ref[...] + _mm(m.astype(BF16), wo_ref[...])


def _merge(x, a, b, c, gate, wb, wo, tm):
    n, d = x.shape
    row = lambda i: (i, 0)
    return pl.pallas_call(
        _merge_kernel,
        grid=(n // tm,),
        in_specs=[pl.BlockSpec((tm, d), row), pl.BlockSpec((tm, ATT_WIDTH), row), pl.BlockSpec((tm, ATT_WIDTH), row),
                  pl.BlockSpec((tm, ATT_WIDTH), row), pl.BlockSpec((tm, N_BRANCH * d), row),
                  pl.BlockSpec(wb.shape, lambda i: (0, 0, 0)), pl.BlockSpec(wo.shape, lambda i: (0, 0))],
        out_specs=pl.BlockSpec((tm, d), row),
        out_shape=jax.ShapeDtypeStruct((n, d), F32),
        compiler_params=_cparams(1, VMEM_LIMIT),
        name="merge",
    )(x, a, b, c, gate, wb, wo)


def _mlp_ple_kernel(x_ref, p_ref, gm_ref, wu_ref, wd_ref, gp_ref, wg_ref, wp_ref, o_ref, *, f_chunk):
    x = x_ref[...]
    ms = jnp.mean(x * x, axis=-1, keepdims=True)
    h = (x * lax.rsqrt(ms + RMS_EPS) * gm_ref[...]).astype(BF16)
    d_ff = wu_ref.shape[1]
    y = x
    for f0 in range(0, d_ff, f_chunk):
        hid = jnp.maximum(_mm(h, wu_ref[:, f0:f0 + f_chunk]), 0.0)
        y = y + _mm((hid * hid).astype(BF16), wd_ref[f0:f0 + f_chunk, :])
    ms2 = jnp.mean(y * y, axis=-1, keepdims=True)
    h2 = (y * lax.rsqrt(ms2 + RMS_EPS) * gp_ref[...]).astype(BF16)
    gate = _sigmoid(_mm(h2, wg_ref[...]))
    o_ref[...] = y + gate * _mm(p_ref[...].astype(BF16), wp_ref[...])


def _mlp_ple(x, p, gm, wu, wd, gp, wg, wp, tm):
    n, d = x.shape
    row = lambda i: (i, 0)
    const = lambda i: (0, 0)
    return pl.pallas_call(
        functools.partial(_mlp_ple_kernel, f_chunk=512),
        grid=(n // tm,),
        in_specs=[pl.BlockSpec((tm, d), row), pl.BlockSpec((tm, p.shape[1]), row), pl.BlockSpec((1, d), const),
                  pl.BlockSpec(wu.shape, const), pl.BlockSpec(wd.shape, const), pl.BlockSpec((1, d), const),
                  pl.BlockSpec(wg.shape, const), pl.BlockSpec(wp.shape, const)],
        out_specs=pl.BlockSpec((tm, d), row),
        out_shape=jax.ShapeDtypeStruct((n, d), F32),
        compiler_params=_cparams(1, VMEM_LIMIT),
        name="mlp_ple",
    )(x, p, gm, wu, wd, gp, wg, wp)


def _rope_tables(pos):
    half = ROT_DIM // 2
    inv = ROPE_THETA ** (-jnp.arange(half, dtype=F32) / half)
    ang = pos.astype(F32)[:, None] * inv[None, :]
    cos, sin = jnp.cos(ang), jnp.sin(ang)
    t = pos.shape[0]
    c64 = jnp.concatenate([cos, cos, jnp.ones((t, HEAD_DIM - ROT_DIM), F32)], axis=1)
    s64 = jnp.concatenate([-sin, sin, jnp.zeros((t, HEAD_DIM - ROT_DIM), F32)], axis=1)
    reps = LANES // HEAD_DIM
    return jnp.tile(c64, (1, reps)), jnp.tile(s64, (1, reps))


def _head_perm():
    heads = [kvh * GROUP + g for g in range(GROUP) for kvh in range(KV_HEADS)]
    return np.concatenate([np.arange(h * HEAD_DIM, (h + 1) * HEAD_DIM) for h in heads])


def _head_group_matrix():
    r = np.arange(ATT_WIDTH) // HEAD_DIM
    return jnp.asarray((r[:, None] == r[None, :]).astype(np.float32), dtype=BF16)


def _row_tile(n):
    return 256 if n % 256 == 0 else n


def kernel(x_prompt, x_sample, p_prompt, p_sample, cache_sb_k, cache_sb_v, cache_moba_k, cache_moba_v,
           state_pool, page_table, norm_mix, w_in, pool_w, pool_scale, moba_q_norm, moba_k_norm,
           w_branch, w_out, norm_mlp, w_up, w_down, norm_ple, w_ple_gate, w_ple):
    depth = w_in.shape[0]
    bp, seq, d = x_prompt.shape
    bs, t_new, _ = x_sample.shape
    n_p, n_s = bp * seq, bs * t_new
    n_phys, page = cache_sb_k.shape[1], cache_sb_k.shape[2]
    past_len = page_table.shape[1] * page
    assert seq % MOBA_BLOCK == 0 and past_len % MOBA_BLOCK == 0 and MOBA_BLOCK % page == 0
    assert t_new & (t_new - 1) == 0 and GROUP * t_new == 8 and past_len >= POOL_HIST >= t_new
    assert seq // MOBA_BLOCK <= LANES and past_len // MOBA_BLOCK <= LANES

    perm = _head_perm()
    col = np.arange(w_in.shape[2])
    q_sb0, q_mb0 = ATT_WIDTH, 2 * ATT_WIDTH + 2 * KV_WIDTH
    col[q_sb0:q_sb0 + ATT_WIDTH] = q_sb0 + perm
    col[q_mb0:q_mb0 + ATT_WIDTH] = q_mb0 + perm
    w_in_b = w_in[:, :, col].astype(BF16)
    w_branch_b = jnp.stack([w_branch[:, 0], w_branch[:, 1][:, perm], w_branch[:, 2][:, perm]], axis=1).astype(BF16)
    w_out_b, w_up_b, w_down_b = w_out.astype(BF16), w_up.astype(BF16), w_down.astype(BF16)
    w_pg_b, w_ple_b, pool_w_b = w_ple_gate.astype(BF16), w_ple.astype(BF16), pool_w.astype(BF16)
    qn = jnp.tile(moba_q_norm, (1, N_HEADS))
    kn = jnp.tile(moba_k_norm, (1, KV_HEADS))
    gmat = _head_group_matrix()

    tm_p, tm_s = _row_tile(n_p), _row_tile(n_s)
    cos_p, sin_p = _rope_tables(jnp.arange(seq))
    cos_s, sin_s = _rope_tables(jnp.tile(past_len + jnp.arange(t_new), bs))
    caches = [c.reshape(depth, n_phys, page, KV_WIDTH) for c in (cache_sb_k, cache_sb_v, cache_moba_k, cache_moba_v)]

    xp = x_prompt.reshape(n_p, d)
    xs = x_sample.reshape(n_s, d)
    outs_p = [[] for _ in range(5)]
    outs_s = [[] for _ in range(5)]

    def to_q8(q):
        return q.reshape(bs, t_new, GROUP, KV_WIDTH).transpose(0, 2, 1, 3).reshape(bs, GROUP * t_new, KV_WIDTH)

    def from_q8(r):
        return r.reshape(bs, GROUP, t_new, KV_WIDTH).transpose(0, 2, 1, 3).reshape(n_s, ATT_WIDTH)

    for i in range(depth):
        layer_w = (norm_mix[i][None], w_in_b[i], qn[i][None], kn[i][None])
        post_w = (norm_mlp[i][None], w_up_b[i], w_down_b[i], norm_ple[i][None], w_pg_b[i], w_ple_b[i])
        ps = pool_scale[i][None]

        (u, q_sb, k_sb, v_sb, k_sb_b, v_sb_b, q_mf, q_mb, k_mb, v_mb, k_mb_b, v_mb_b, gate) = _inproj(
            xp, *layer_w, cos_p, sin_p, gmat, tm_p)
        r3 = lambda a: a.reshape(bp, seq, a.shape[-1])
        a = _pool_prompt(r3(u), pool_w_b[i], ps, min(512, seq)).reshape(n_p, ATT_WIDTH)
        b = _sb_prompt(r3(q_sb), r3(k_sb_b), r3(v_sb_b)).reshape(n_p, ATT_WIDTH)
        c = _moba_prompt(r3(q_mf), r3(q_mb), r3(k_mb_b), r3(v_mb_b), _kmean(r3(k_mb))).reshape(n_p, ATT_WIDTH)
        xp = _merge(xp, a, b, c, gate, w_branch_b[i], w_out_b[i], tm_p)
        xp = _mlp_ple(xp, p_prompt[i].reshape(n_p, -1), *post_w, tm_p)
        for lst, val in zip(outs_p, (k_sb, v_sb, k_mb, v_mb, r3(u)[:, seq - POOL_HIST:])):
            lst.append(val)

        (u, q_sb, k_sb, v_sb, _, _, q_mf, _, k_mb, v_mb, _, _, gate) = _inproj(
            xs, *layer_w, cos_s, sin_s, gmat, tm_s)
        r3 = lambda a: a.reshape(bs, t_new, a.shape[-1])
        st_t = state_pool[i].transpose(1, 0, 2)
        a = _pool_sample(st_t, r3(u).transpose(1, 0, 2), pool_w_b[i], ps, past_len).transpose(1, 0, 2).reshape(n_s, ATT_WIDTH)
        b = from_q8(_sb_sample(to_q8(q_sb.astype(F32)), r3(k_sb), r3(v_sb), caches[0], caches[1], page_table, i))
        c = from_q8(_moba_sample(to_q8(q_mf), r3(k_mb), r3(v_mb), caches[2], caches[3], page_table, i))
        xs = _merge(xs, a, b, c, gate, w_branch_b[i], w_out_b[i], tm_s)
        xs = _mlp_ple(xs, p_sample[i].reshape(n_s, -1), *post_w, tm_s)
        pool_state = jnp.concatenate([state_pool[i][:, t_new:], r3(u)], axis=1)
        for lst, val in zip(outs_s, (k_sb, v_sb, k_mb, v_mb, pool_state)):
            lst.append(val)

    def kv_stack(lst, nb, nt):
        return jnp.stack(lst).reshape(depth, nb, nt, KV_HEADS, HEAD_DIM)

    return (xp.reshape(bp, seq, d), xs.reshape(bs, t_new, d),
            kv_stack(outs_p[0], bp, seq), kv_stack(outs_p[1], bp, seq),
            kv_stack(outs_p[2], bp, seq), kv_stack(outs_p[3], bp, seq), jnp.stack(outs_p[4]),
            kv_stack(outs_s[0], bs, t_new), kv_stack(outs_s[1], bs, t_new),
            kv_stack(outs_s[2], bs, t_new), kv_stack(outs_s[3], bs, t_new), jnp.stack(outs_s[4]))
```

```python
import functools

import jax
import jax.numpy as jnp
import numpy as np
from jax import lax
from jax.experimental import pallas as pl
from jax.experimental.pallas import tpu as pltpu

F32 = jnp.float32
BF16 = jnp.bfloat16

HEAD_DIM = 64
KV_HEADS = 4
GROUP = 2
N_HEADS = KV_HEADS * GROUP
ATT_WIDTH = N_HEADS * HEAD_DIM
KV_WIDTH = KV_HEADS * HEAD_DIM
ATT_SCALE = HEAD_DIM ** -0.5
POOL_WINDOWS = (2, 4, 8, 16)
POOL_GROUP = 128
POOL_HIST = max(POOL_WINDOWS) - 1
ROPE_THETA = 500000.0
ROT_DIM = HEAD_DIM // 4
MOBA_BLOCK = 256
MOBA_TOPK = 3
N_BRANCH = 3
RMS_EPS = 1e-6
NEG_INF = float("-inf")

LANES = 128
BF16_ROWS = 16
ATT_TILE = MOBA_BLOCK
VMEM_LIMIT = 56 * 1024 * 1024


def _cparams(n_axes, vmem=None):
    return pltpu.CompilerParams(dimension_semantics=("arbitrary",) * n_axes, vmem_limit_bytes=vmem)


def _mm(a, b):
    return jnp.dot(a, b, preferred_element_type=F32)


def _mm_nt(a, b):
    return lax.dot_general(a, b, (((1,), (1,)), ((), ())), preferred_element_type=F32)


def _split2(x):
    hi = x.astype(BF16)
    lo = (x - hi.astype(F32)).astype(BF16)
    return hi, lo


def _split3(x):
    hi = x.astype(BF16)
    r = x - hi.astype(F32)
    mid = r.astype(BF16)
    lo = (r - mid.astype(F32)).astype(BF16)
    return hi, mid, lo


def _sigmoid(x):
    return 1.0 / (1.0 + jnp.exp(-x))


def _softplus(z):
    return jnp.maximum(z, 0.0) + jnp.log(1.0 + jnp.exp(jnp.minimum(z, -z)))


def _inproj_kernel(x_ref, gn_ref, w_ref, qn_ref, kn_ref, cos_ref, sin_ref, gmat_ref, *out_refs, prompt):
    x = x_ref[...]
    ms = jnp.mean(x * x, axis=-1, keepdims=True)
    h = (x * lax.rsqrt(ms + RMS_EPS) * gn_ref[...]).astype(BF16)

    def proj(lo, hi):
        return _mm(h, w_ref[:, lo:hi])

    def head_norm_rope(y, g_ref):
        width = y.shape[1]
        gm = gmat_ref[0:width, 0:width]
        p0, p1, p2 = _split3(y * y)
        ms_h = (_mm(p0, gm) + _mm(p1, gm) + _mm(p2, gm)) * (1.0 / HEAD_DIM)
        yn = y * lax.rsqrt(ms_h + RMS_EPS) * g_ref[...]
        reps = width // LANES
        c = jnp.concatenate([cos_ref[...]] * reps, axis=1)
        s = jnp.concatenate([sin_ref[...]] * reps, axis=1)
        lane = lax.broadcasted_iota(jnp.int32, yn.shape, 1)
        first_half = (lane & (HEAD_DIM - 1)) < (ROT_DIM // 2)
        partner = jnp.where(first_half, pltpu.roll(yn, width - ROT_DIM // 2, 1), pltpu.roll(yn, ROT_DIM // 2, 1))
        return yn * c + partner * s

    o = 0
    u = proj(o, o + ATT_WIDTH)
    o += ATT_WIDTH
    q_sb = proj(o, o + ATT_WIDTH) * ATT_SCALE
    o += ATT_WIDTH
    k_sb = proj(o, o + KV_WIDTH)
    o += KV_WIDTH
    v_sb = proj(o, o + KV_WIDTH)
    o += KV_WIDTH
    q_mb = head_norm_rope(proj(o, o + ATT_WIDTH), qn_ref)
    o += ATT_WIDTH
    k_mb = head_norm_rope(proj(o, o + KV_WIDTH), kn_ref)
    o += KV_WIDTH
    v_mb = proj(o, o + KV_WIDTH)
    o += KV_WIDTH
    if prompt:
        (u_ref, qsb_t, ksb_ref, vsb_ref, ksb_b, vsb_t, qmb_tf, qmb_t, kmb_ref, vmb_ref, kmb_b, vmb_t, gate_ref) = out_refs
        qsb_t[0] = q_sb.T.astype(BF16)
        ksb_b[...] = k_sb.astype(BF16)
        vsb_t[0] = v_sb.T.astype(BF16)
        q_mb_t = q_mb.T
        qmb_tf[0] = q_mb_t
        qmb_t[0] = (q_mb_t * ATT_SCALE).astype(BF16)
        kmb_b[...] = k_mb.astype(BF16)
        vmb_t[0] = v_mb.T.astype(BF16)
    else:
        (u_ref, qsb_ref, ksb_ref, vsb_ref, qmb_ref, kmb_ref, vmb_ref, gate_ref) = out_refs
        qsb_ref[...] = q_sb
        qmb_ref[...] = q_mb
    u_ref[...] = u
    ksb_ref[...] = k_sb
    vsb_ref[...] = v_sb
    kmb_ref[...] = k_mb
    vmb_ref[...] = v_mb
    gate_w = gate_ref.shape[1]
    chunk = 512
    for c0 in range(0, gate_w, chunk):
        gate_ref[:, c0:c0 + chunk] = _sigmoid(proj(o + c0, o + c0 + chunk))


def _inproj(x, gn, w, qn, kn, cos_t, sin_t, gmat, tm, prompt):
    n, d = x.shape
    in_w = w.shape[1]
    gate_w = in_w - (3 * ATT_WIDTH + 4 * KV_WIDTH)
    nper = cos_t.shape[0] // tm
    nt = n // tm
    row = lambda i: (i, 0)
    const = lambda i: (0, 0)

    def flat(width, dtype):
        return jax.ShapeDtypeStruct((n, width), dtype), pl.BlockSpec((tm, width), row)

    def transposed(width, dtype):
        return jax.ShapeDtypeStruct((nt, width, tm), dtype), pl.BlockSpec((1, width, tm), lambda i: (i, 0, 0))

    if prompt:
        outs = [flat(ATT_WIDTH, F32), transposed(ATT_WIDTH, BF16), flat(KV_WIDTH, F32), flat(KV_WIDTH, F32),
                flat(KV_WIDTH, BF16), transposed(KV_WIDTH, BF16), transposed(ATT_WIDTH, F32),
                transposed(ATT_WIDTH, BF16), flat(KV_WIDTH, F32), flat(KV_WIDTH, F32), flat(KV_WIDTH, BF16),
                transposed(KV_WIDTH, BF16), flat(gate_w, F32)]
    else:
        outs = [flat(ATT_WIDTH, F32), flat(ATT_WIDTH, F32), flat(KV_WIDTH, F32), flat(KV_WIDTH, F32),
                flat(ATT_WIDTH, F32), flat(KV_WIDTH, F32), flat(KV_WIDTH, F32), flat(gate_w, F32)]
    return pl.pallas_call(
        functools.partial(_inproj_kernel, prompt=prompt),
        grid=(nt,),
        in_specs=[pl.BlockSpec((tm, d), row), pl.BlockSpec((1, d), const), pl.BlockSpec((d, in_w), const),
                  pl.BlockSpec((1, ATT_WIDTH), const), pl.BlockSpec((1, KV_WIDTH), const),
                  pl.BlockSpec((tm, LANES), lambda i: (i % nper, 0)), pl.BlockSpec((tm, LANES), lambda i: (i % nper, 0)),
                  pl.BlockSpec((ATT_WIDTH, ATT_WIDTH), const)],
        out_specs=[s for _, s in outs],
        out_shape=[s for s, _ in outs],
        compiler_params=_cparams(1, VMEM_LIMIT),
        name="inproj_prompt" if prompt else "inproj_sample",
    )(x, gn, w, qn, kn, cos_t, sin_t, gmat)


def _pool_prompt_kernel(cur_ref, prev_ref, pw_ref, ps_ref, o_ref, ext_ref, *, ts):
    t = pl.program_id(1)
    hist = ext_ref.shape[0] - ts
    ext_ref[0:hist, :] = jnp.where(t == 0, 0.0, prev_ref[0])
    ext_ref[hist:, :] = cur_ref[0]
    pos = t * ts + lax.broadcasted_iota(jnp.int32, (ts, POOL_GROUP), 0)
    for g, w in enumerate(POOL_WINDOWS):
        lo, hi = g * POOL_GROUP, (g + 1) * POOL_GROUP
        ws = ext_ref[hist:hist + ts, lo:hi]
        for k in range(1, w):
            ws = ws + ext_ref[hist - k:hist - k + ts, lo:hi]
        cnt = jnp.minimum(pos + 1, w).astype(F32)
        dlt = ws / cnt - ext_ref[hist:hist + ts, lo:hi]
        o_ref[0, :, lo:hi] = _mm(dlt.astype(BF16), pw_ref[g]) * ps_ref[:, lo:hi]


def _pool_prompt(u, pw, ps, ts):
    b, s, c = u.shape
    hist = 16
    ratio = ts // hist
    return pl.pallas_call(
        functools.partial(_pool_prompt_kernel, ts=ts),
        grid=(b, s // ts),
        in_specs=[pl.BlockSpec((1, ts, c), lambda i, t: (i, t, 0)),
                  pl.BlockSpec((1, hist, c), lambda i, t: (i, jnp.maximum(t * ratio - 1, 0), 0)),
                  pl.BlockSpec((len(POOL_WINDOWS), POOL_GROUP, POOL_GROUP), lambda i, t: (0, 0, 0)),
                  pl.BlockSpec((1, c), lambda i, t: (0, 0))],
        out_specs=pl.BlockSpec((1, ts, c), lambda i, t: (i, t, 0)),
        out_shape=jax.ShapeDtypeStruct((b, s, c), F32),
        scratch_shapes=[pltpu.VMEM((ts + hist, c), F32)],
        compiler_params=_cparams(2),
        name="pool_prompt",
    )(u, u, pw, ps)


def _pool_sample_kernel(st_ref, u_ref, pw_ref, ps_ref, o_ref, *, start_pos):
    n_new = u_ref.shape[0]

    def row(idx, lo, hi):
        return st_ref[idx, :, lo:hi] if idx < POOL_HIST else u_ref[idx - POOL_HIST, :, lo:hi]

    for j in range(n_new):
        for g, w in enumerate(POOL_WINDOWS):
            lo, hi = g * POOL_GROUP, (g + 1) * POOL_GROUP
            ws = row(POOL_HIST + j, lo, hi)
            for k in range(1, w):
                ws = ws + row(POOL_HIST + j - k, lo, hi)
            cnt = float(min(start_pos + j + 1, w))
            dlt = ws / cnt - row(POOL_HIST + j, lo, hi)
            o_ref[j, :, lo:hi] = _mm(dlt.astype(BF16), pw_ref[g]) * ps_ref[:, lo:hi]


def _pool_sample(st_t, u_t, pw, ps, start_pos):
    t, b, c = u_t.shape
    full = lambda shape: pl.BlockSpec(shape, lambda i: (0,) * len(shape))
    return pl.pallas_call(
        functools.partial(_pool_sample_kernel, start_pos=start_pos),
        grid=(1,),
        in_specs=[full(st_t.shape), full(u_t.shape), full(pw.shape), full(ps.shape)],
        out_specs=full((t, b, c)),
        out_shape=jax.ShapeDtypeStruct((t, b, c), F32),
        compiler_params=_cparams(1),
        name="pool_sample",
    )(st_t, u_t, pw, ps)


def _load_head_queries(qt_ref, qz_ref):
    tq = qt_ref.shape[2]
    row_head = lax.broadcasted_iota(jnp.int32, (KV_WIDTH, tq), 0) >> 6
    for kvh in range(KV_HEADS):
        for g in range(GROUP):
            qg = qt_ref[0, g * KV_WIDTH:(g + 1) * KV_WIDTH, :]
            qz_ref[kvh, :, g * tq:(g + 1) * tq] = jnp.where(row_head == kvh, qg, jnp.zeros_like(qg))


def _store_heads_out(o_ref, per_head, tq):
    for g in range(GROUP):
        out_t = jnp.concatenate([p[:, g * tq:(g + 1) * tq] for p in per_head], axis=0)
        o_ref[0, :, g * KV_WIDTH:(g + 1) * KV_WIDTH] = out_t.T


def _att_specs(b_axis_tiles, tq):
    nt = b_axis_tiles
    return [pl.BlockSpec((1, ATT_WIDTH, tq), lambda i, j: (i * nt + j, 0, 0)),
            pl.BlockSpec((nt, tq, KV_WIDTH), lambda i, j: (i, 0, 0)),
            pl.BlockSpec((nt, KV_WIDTH, tq), lambda i, j: (i, 0, 0))]


def _suffix_matrix_t(tk):
    r = np.arange(tk)
    later = (r[None, :] > r[:, None]).astype(np.float32)
    return jnp.asarray(np.concatenate([later, np.ones((BF16_ROWS, tk), np.float32)], axis=0), dtype=BF16)


def _sb_prompt_kernel(qt_ref, k_ref, vt_ref, u_ref, o_ref, qz_ref, acc_ref, carry_ref):
    i = pl.program_id(1)
    tk = k_ref.shape[1]
    q2 = qz_ref.shape[2]
    tq = q2 // GROUP
    _load_head_queries(qt_ref, qz_ref)
    acc_ref[...] = jnp.zeros_like(acc_ref)
    carry_ref[...] = jnp.zeros_like(carry_ref)
    key = lax.broadcasted_iota(jnp.int32, (tk, q2), 0)
    qloc = lax.broadcasted_iota(jnp.int32, (tk, q2), 1) & (tq - 1)
    diag_valid = key < qloc
    u2 = u_ref[...]

    def block(j, valid):
        kb = k_ref[j]
        for kvh in range(KV_HEADS):
            z = _mm(kb, qz_ref[kvh])
            drop = _softplus(z)
            log_take = z - drop
            if valid is not None:
                drop = jnp.where(valid, drop, 0.0)
            hi, lo = _split2(drop)
            cs = _mm(u2, hi) + _mm(u2, lo)
            w = jnp.exp(log_take - (cs[0:tk] + carry_ref[kvh, 0:1, :]))
            if valid is not None:
                w = jnp.where(valid, w, 0.0)
            acc_ref[kvh] += _mm(vt_ref[j, kvh * HEAD_DIM:(kvh + 1) * HEAD_DIM, :], w.astype(BF16))
            carry_ref[kvh] += cs[tk:tk + 8]

    block(i, diag_valid)

    def body(t, c):
        block(i - 1 - t, None)
        return c

    lax.fori_loop(0, i, body, 0)
    _store_heads_out(o_ref, [acc_ref[kvh] for kvh in range(KV_HEADS)], tq)


def _sb_prompt(qt, k3, vt3, b, tiles):
    tq = k3.shape[1]
    q2 = GROUP * tq
    return pl.pallas_call(
        _sb_prompt_kernel,
        grid=(b, tiles),
        in_specs=_att_specs(tiles, tq) + [pl.BlockSpec((tq + BF16_ROWS, tq), lambda i, j: (0, 0))],
        out_specs=pl.BlockSpec((1, tq, ATT_WIDTH), lambda i, j: (i, j, 0)),
        out_shape=jax.ShapeDtypeStruct((b, tiles * tq, ATT_WIDTH), F32),
        scratch_shapes=[pltpu.VMEM((KV_HEADS, KV_WIDTH, q2), BF16),
                        pltpu.VMEM((KV_HEADS, HEAD_DIM, q2), F32),
                        pltpu.VMEM((KV_HEADS, 8, q2), F32)],
        compiler_params=_cparams(2),
        name="sb_prompt",
    )(qt, k3, vt3, _suffix_matrix_t(tq))


def _page_specs(layer, npages, page):
    def spec(p):
        return pl.BlockSpec((None, None, KV_WIDTH, page), lambda b, pt: (layer, pt[b, p], 0, 0))
    return [spec(p) for p in range(npages)]


def _sample_logits(qs, kt):
    return jnp.concatenate(
        [_mm(qs[h], kt[h * HEAD_DIM:(h + 1) * HEAD_DIM, :].astype(BF16)) for h in range(KV_HEADS)], axis=0)


def _sample_weighted_values(w, vt, outs):
    n = w.shape[0] // KV_HEADS
    return [outs[h] + _mm_nt(w[h * n:(h + 1) * n, :].astype(BF16), vt[h * HEAD_DIM:(h + 1) * HEAD_DIM, :].astype(BF16))
            for h in range(KV_HEADS)]


def _sb_sample_kernel(pt_ref, q_ref, ktn_ref, vtn_ref, u_ref, *refs, npages, n_new):
    del pt_ref
    kp, vp, o_ref = refs[:npages], refs[npages:2 * npages], refs[2 * npages]
    page = u_ref.shape[0]
    qs = [q_ref[0, h].astype(BF16) for h in range(KV_HEADS)]
    rows = KV_HEADS * q_ref.shape[2]
    row = lax.broadcasted_iota(jnp.int32, (rows, page), 0)
    col = lax.broadcasted_iota(jnp.int32, (rows, page), 1)
    valid_new = col < (row & (n_new - 1))
    drops, take = [], []
    for blk in range(npages + 1):
        kt = kp[blk][...] if blk < npages else ktn_ref[0]
        z = _sample_logits(qs, kt)
        drop = _softplus(z)
        take.append(z - drop)
        drops.append(jnp.where(valid_new, drop, 0.0) if blk == npages else drop)
    parts = []
    for drop in drops:
        parts.extend(_split2(drop))
    cs = _mm(jnp.concatenate(parts, axis=0), u_ref[...])
    outs = [jnp.zeros((rows // KV_HEADS, HEAD_DIM), F32)] * KV_HEADS
    carry = jnp.zeros((rows, page), F32)
    for blk in reversed(range(npages + 1)):
        c = cs[blk * 2 * rows:(blk + 1) * 2 * rows]
        both = c[0:rows] + c[rows:2 * rows]
        w = jnp.exp(take[blk] - (both[:, 0:page] + carry))
        if blk == npages:
            w = jnp.where(valid_new, w, 0.0)
        vt = vp[blk][...] if blk < npages else vtn_ref[0]
        outs = _sample_weighted_values(w, vt, outs)
        carry = carry + both[:, page:]
    for h in range(KV_HEADS):
        o_ref[0, h] = outs[h]


def _suffix_matrix(tk):
    r = np.arange(tk)
    strict = (r[:, None] > r[None, :]).astype(np.float32)
    return jnp.asarray(np.concatenate([strict, np.ones((tk, tk), np.float32)], axis=1), dtype=BF16)


def _sample_in_specs(q4, page):
    rows = q4.shape[2]
    return [pl.BlockSpec((1, KV_HEADS, rows, HEAD_DIM), lambda b, pt: (b, 0, 0, 0)),
            pl.BlockSpec((1, KV_WIDTH, page), lambda b, pt: (b, 0, 0)),
            pl.BlockSpec((1, KV_WIDTH, page), lambda b, pt: (b, 0, 0))]


def _sb_sample(q4, kt_new, vt_new, cache_k, cache_v, page_table, layer, n_new):
    bs, npages = page_table.shape
    page = cache_k.shape[3]
    grid_spec = pltpu.PrefetchScalarGridSpec(
        num_scalar_prefetch=1,
        grid=(bs,),
        in_specs=_sample_in_specs(q4, page) + [pl.BlockSpec((page, 2 * page), lambda b, pt: (0, 0))]
                 + _page_specs(layer, npages, page) + _page_specs(layer, npages, page),
        out_specs=pl.BlockSpec((1,) + q4.shape[1:], lambda b, pt: (b, 0, 0, 0)))
    return pl.pallas_call(
        functools.partial(_sb_sample_kernel, npages=npages, n_new=n_new),
        grid_spec=grid_spec,
        out_shape=jax.ShapeDtypeStruct(q4.shape, F32),
        compiler_params=_cparams(1),
        name="sb_sample",
    )(page_table, q4, kt_new, vt_new, _suffix_matrix(page), *([cache_k] * npages), *([cache_v] * npages))


def _topk_rows(scores, n_valid, nb):
    blk = lax.broadcasted_iota(jnp.int32, scores.shape, 0)
    valid = blk < n_valid
    s = jnp.where(valid, scores, NEG_INF)
    rank = jnp.zeros(scores.shape, jnp.int32)
    for m in range(nb):
        other = s[m:m + 1, :]
        beats = (other > s) | ((other == s) & (blk > m))
        rank = rank + jnp.where(beats, 1, 0)
    return jnp.where(valid & (rank < MOBA_TOPK), 1.0, 0.0)


def _topk_lanes(scores, n_valid, nb):
    blk = lax.broadcasted_iota(jnp.int32, scores.shape, 1)
    valid = blk < n_valid
    s = jnp.where(valid, scores, NEG_INF)
    rank = jnp.zeros(scores.shape, jnp.int32)
    for m in range(nb):
        other = s[:, m:m + 1]
        beats = (other > s) | ((other == s) & (blk > m))
        rank = rank + jnp.where(beats, 1, 0)
    return jnp.where(valid & (rank < MOBA_TOPK), 1.0, 0.0)


def _kmean_kernel(k_ref, o_ref, *, nb):
    o_ref[...] = jnp.zeros_like(o_ref)
    for n in range(nb):
        blk = k_ref[0, n * MOBA_BLOCK:(n + 1) * MOBA_BLOCK, :]
        o_ref[0, n:n + 1, :] = jnp.sum(blk, axis=0, keepdims=True) * (1.0 / MOBA_BLOCK)


def _kmean(k, nbp):
    b, s, _ = k.shape
    nb = s // MOBA_BLOCK
    return pl.pallas_call(
        functools.partial(_kmean_kernel, nb=nb),
        grid=(b,),
        in_specs=[pl.BlockSpec((1, s, KV_WIDTH), lambda i: (i, 0, 0))],
        out_specs=pl.BlockSpec((1, nbp, KV_WIDTH), lambda i: (i, 0, 0)),
        out_shape=jax.ShapeDtypeStruct((b, nbp, KV_WIDTH), F32),
        compiler_params=_cparams(1),
        name="moba_kmean",
    )(k)


def _moba_prompt_kernel(qtf_ref, qt_ref, k_ref, vt_ref, km_ref, o_ref,
                        qz_ref, sel_ref, m_ref, l_ref, acc_ref, *, nb):
    qblk = pl.program_id(1)
    tk = k_ref.shape[1]
    q2 = qz_ref.shape[2]
    tq = q2 // GROUP
    _load_head_queries(qt_ref, qz_ref)
    km_hi, km_lo = _split2(km_ref[0])
    row_head = lax.broadcasted_iota(jnp.int32, (KV_WIDTH, tq), 0) >> 6
    for kvh in range(KV_HEADS):
        qf = jnp.concatenate(
            [jnp.where(row_head == kvh, qtf_ref[0, g * KV_WIDTH:(g + 1) * KV_WIDTH, :], 0.0) for g in range(GROUP)],
            axis=1)
        q_hi, q_lo = _split2(qf)
        scores = _mm(km_hi, q_hi) + (_mm(km_hi, q_lo) + _mm(km_lo, q_hi))
        sel_ref[kvh] = _topk_rows(scores, qblk, nb)
    key = lax.broadcasted_iota(jnp.int32, (tk, q2), 0)
    qloc = lax.broadcasted_iota(jnp.int32, (tk, q2), 1) & (tq - 1)
    causal = key <= qloc
    kb = k_ref[qblk]
    for kvh in range(KV_HEADS):
        st = jnp.where(causal, _mm(kb, qz_ref[kvh]), NEG_INF)
        m0 = jnp.max(st, axis=0, keepdims=True)
        p = jnp.exp(st - m0)
        m_ref[kvh] = m0
        l_ref[kvh] = jnp.sum(p, axis=0, keepdims=True)
        acc_ref[kvh] = _mm(vt_ref[qblk, kvh * HEAD_DIM:(kvh + 1) * HEAD_DIM, :], p.astype(BF16))

    def body(n, c):
        kb = k_ref[n]
        for kvh in range(KV_HEADS):
            bias = jnp.where(sel_ref[kvh, pl.ds(n, 1), :] > 0.5, 0.0, NEG_INF)
            st = _mm(kb, qz_ref[kvh]) + bias
            m_old = m_ref[kvh]
            m_new = jnp.maximum(m_old, jnp.max(st, axis=0, keepdims=True))
            alpha = jnp.exp(m_old - m_new)
            p = jnp.exp(st - m_new)
            l_ref[kvh] = alpha * l_ref[kvh] + jnp.sum(p, axis=0, keepdims=True)
            acc_ref[kvh] = alpha * acc_ref[kvh] + _mm(vt_ref[n, kvh * HEAD_DIM:(kvh + 1) * HEAD_DIM, :], p.astype(BF16))
            m_ref[kvh] = m_new
        return c

    lax.fori_loop(0, qblk, body, 0)
    _store_heads_out(o_ref, [acc_ref[kvh] / l_ref[kvh] for kvh in range(KV_HEADS)], tq)


def _moba_prompt(qtf, qt, k3, vt3, kmean, b, tiles):
    tq = k3.shape[1]
    q2 = GROUP * tq
    nbp = kmean.shape[1]
    return pl.pallas_call(
        functools.partial(_moba_prompt_kernel, nb=tiles),
        grid=(b, tiles),
        in_specs=[pl.BlockSpec((1, ATT_WIDTH, tq), lambda i, j: (i * tiles + j, 0, 0))] + _att_specs(tiles, tq)
                 + [pl.BlockSpec((1, nbp, KV_WIDTH), lambda i, j: (i, 0, 0))],
        out_specs=pl.BlockSpec((1, tq, ATT_WIDTH), lambda i, j: (i, j, 0)),
        out_shape=jax.ShapeDtypeStruct((b, tiles * tq, ATT_WIDTH), F32),
        scratch_shapes=[pltpu.VMEM((KV_HEADS, KV_WIDTH, q2), BF16), pltpu.VMEM((KV_HEADS, nbp, q2), F32),
                        pltpu.VMEM((KV_HEADS, 1, q2), F32), pltpu.VMEM((KV_HEADS, 1, q2), F32),
                        pltpu.VMEM((KV_HEADS, HEAD_DIM, q2), F32)],
        compiler_params=_cparams(2),
        name="moba_prompt",
    )(qtf, qt, k3, vt3, kmean)


def _moba_sample_kernel(pt_ref, q_ref, ktn_ref, vtn_ref, *refs, npages, n_new):
    del pt_ref
    kp, vp, o_ref = refs[:npages], refs[npages:2 * npages], refs[2 * npages]
    page = ktn_ref.shape[2]
    per_blk = MOBA_BLOCK // page
    nb_past = npages // per_blk
    n = q_ref.shape[2]
    rows = KV_HEADS * n
    qf = [q_ref[0, h] for h in range(KV_HEADS)]
    qs = [(q * ATT_SCALE).astype(BF16) for q in qf]
    ones = jnp.ones((page, LANES), BF16)
    lane = lax.broadcasted_iota(jnp.int32, (KV_WIDTH, LANES), 1)
    km = jnp.zeros((KV_WIDTH, LANES), F32)
    for blk in range(nb_past):
        tot = jnp.zeros((KV_WIDTH, LANES), F32)
        for r in range(per_blk):
            k_hi, k_lo = _split2(kp[blk * per_blk + r][...])
            tot = tot + (_mm(k_hi, ones) + _mm(k_lo, ones))
        km = jnp.where(lane == blk, tot * (1.0 / MOBA_BLOCK), km)
    scores = []
    for h in range(KV_HEADS):
        q_hi, q_lo = _split2(qf[h])
        km_hi, km_lo = _split2(km[h * HEAD_DIM:(h + 1) * HEAD_DIM, :])
        scores.append(_mm(q_hi, km_hi) + (_mm(q_hi, km_lo) + _mm(q_lo, km_hi)))
    sel = _topk_lanes(jnp.concatenate(scores, axis=0), nb_past, nb_past)
    row = lax.broadcasted_iota(jnp.int32, (rows, page), 0)
    col = lax.broadcasted_iota(jnp.int32, (rows, page), 1)
    visible = (col <= (row & (n_new - 1))) & (col < n_new)
    logits = []
    for pg in range(npages):
        blk = pg // per_blk
        logits.append(jnp.where(sel[:, blk:blk + 1] > 0.5, _sample_logits(qs, kp[pg][...]), NEG_INF))
    logits.append(jnp.where(visible, _sample_logits(qs, ktn_ref[0]), NEG_INF))
    m_el = logits[0]
    for s in logits[1:]:
        m_el = jnp.maximum(m_el, s)
    m = jnp.max(m_el, axis=1, keepdims=True)
    outs = [jnp.zeros((n, HEAD_DIM), F32)] * KV_HEADS
    l_el = jnp.zeros((rows, page), F32)
    for pg in range(npages + 1):
        p = jnp.exp(logits[pg] - m)
        l_el = l_el + p
        outs = _sample_weighted_values(p, vp[pg][...] if pg < npages else vtn_ref[0], outs)
    l = jnp.sum(l_el, axis=1, keepdims=True)
    for h in range(KV_HEADS):
        o_ref[0, h] = outs[h] / l[h * n:(h + 1) * n]


def _moba_sample(q4, kt_new, vt_new, cache_k, cache_v, page_table, layer, n_new):
    bs, npages = page_table.shape
    page = cache_k.shape[3]
    grid_spec = pltpu.PrefetchScalarGridSpec(
        num_scalar_prefetch=1,
        grid=(bs,),
        in_specs=_sample_in_specs(q4, page) + _page_specs(layer, npages, page) + _page_specs(layer, npages, page),
        out_specs=pl.BlockSpec((1,) + q4.shape[1:], lambda b, pt: (b, 0, 0, 0)))
    return pl.pallas_call(
        functools.partial(_moba_sample_kernel, npages=npages, n_new=n_new),
        grid_spec=grid_spec,
        out_shape=jax.ShapeDtypeStruct(q4.shape, F32),
        compiler_params=_cparams(1),
        name="moba_sample",
    )(page_table, q4, kt_new, vt_new, *([cache_k] * npages), *([cache_v] * npages))


def _merge_kernel(x_ref, a_ref, b_ref, c_ref, g_ref, wb_ref, wo_ref, o_ref):
    d = x_ref.shape[1]
    m = g_ref[:, 0:d] * _mm(a_ref[...].astype(BF16), wb_ref[0])
    m = m + g_ref[:, d:2 * d] * _mm(b_ref[...].astype(BF16), wb_ref[1])
    m = m + g_ref[:, 2 * d:3 * d] * _mm(c_ref[...].astype(BF16), wb_ref[2])
    o_ref[...] = x_ref[...] + _mm(m.astype(BF16), wo_ref[...])


def _merge(x, a, b, c, gate, wb, wo, tm):
    n, d = x.shape
    row = lambda i: (i, 0)
    return pl.pallas_call(
        _merge_kernel,
        grid=(n // tm,),
        in_specs=[pl.BlockSpec((tm, d), row), pl.BlockSpec((tm, ATT_WIDTH), row), pl.BlockSpec((tm, ATT_WIDTH), row),
                  pl.BlockSpec((tm, ATT_WIDTH), row), pl.BlockSpec((tm, N_BRANCH * d), row),
                  pl.BlockSpec(wb.shape, lambda i: (0, 0, 0)), pl.BlockSpec(wo.shape, lambda i: (0, 0))],
        out_specs=pl.BlockSpec((tm, d), row),
        out_shape=jax.ShapeDtypeStruct((n, d), F32),
        compiler_params=_cparams(1, VMEM_LIMIT),
        name="merge",
    )(x, a, b, c, gate, wb, wo)


def _mlp_ple_kernel(x_ref, p_ref, gm_ref, wu_ref, wd_ref, gp_ref, wg_ref, wp_ref, o_ref, *, f_chunk):
    x = x_ref[...]
    ms = jnp.mean(x * x, axis=-1, keepdims=True)
    h = (x * lax.rsqrt(ms + RMS_EPS) * gm_ref[...]).astype(BF16)
    d_ff = wu_ref.shape[1]
    y = x
    for f0 in range(0, d_ff, f_chunk):
        hid = jnp.maximum(_mm(h, wu_ref[:, f0:f0 + f_chunk]), 0.0)
        y = y + _mm((hid * hid).astype(BF16), wd_ref[f0:f0 + f_chunk, :])
    ms2 = jnp.mean(y * y, axis=-1, keepdims=True)
    h2 = (y * lax.rsqrt(ms2 + RMS_EPS) * gp_ref[...]).astype(BF16)
    gate = _sigmoid(_mm(h2, wg_ref[...]))
    o_ref[...] = y + gate * _mm(p_ref[...].astype(BF16), wp_ref[...])


def _mlp_ple(x, p, gm, wu, wd, gp, wg, wp, tm):
    n, d = x.shape
    row = lambda i: (i, 0)
    const = lambda i: (0, 0)
    return pl.pallas_call(
        functools.partial(_mlp_ple_kernel, f_chunk=512),
        grid=(n // tm,),
        in_specs=[pl.BlockSpec((tm, d), row), pl.BlockSpec((tm, p.shape[1]), row), pl.BlockSpec((1, d), const),
                  pl.BlockSpec(wu.shape, const), pl.BlockSpec(wd.shape, const), pl.BlockSpec((1, d), const),
                  pl.BlockSpec(wg.shape, const), pl.BlockSpec(wp.shape, const)],
        out_specs=pl.BlockSpec((tm, d), row),
        out_shape=jax.ShapeDtypeStruct((n, d), F32),
        compiler_params=_cparams(1, VMEM_LIMIT),
        name="mlp_ple",
    )(x, p, gm, wu, wd, gp, wg, wp)


def _rope_tables(pos):
    half = ROT_DIM // 2
    inv = ROPE_THETA ** (-jnp.arange(half, dtype=F32) / half)
    ang = pos.astype(F32)[:, None] * inv[None, :]
    cos, sin = jnp.cos(ang), jnp.sin(ang)
    t = pos.shape[0]
    c64 = jnp.concatenate([cos, cos, jnp.ones((t, HEAD_DIM - ROT_DIM), F32)], axis=1)
    s64 = jnp.concatenate([-sin, sin, jnp.zeros((t, HEAD_DIM - ROT_DIM), F32)], axis=1)
    reps = LANES // HEAD_DIM
    return jnp.tile(c64, (1, reps)), jnp.tile(s64, (1, reps))


def _head_perm():
    heads = [kvh * GROUP + g for g in range(GROUP) for kvh in range(KV_HEADS)]
    return np.concatenate([np.arange(h * HEAD_DIM, (h + 1) * HEAD_DIM) for h in heads])


def _head_group_matrix():
    r = np.arange(ATT_WIDTH) // HEAD_DIM
    return jnp.asarray((r[:, None] == r[None, :]).astype(np.float32), dtype=BF16)


def kernel(x_prompt, x_sample, p_prompt, p_sample, cache_sb_k, cache_sb_v, cache_moba_k, cache_moba_v,
           state_pool, page_table, norm_mix, w_in, pool_w, pool_scale, moba_q_norm, moba_k_norm,
           w_branch, w_out, norm_mlp, w_up, w_down, norm_ple, w_ple_gate, w_ple):
    depth = w_in.shape[0]
    bp, seq, d = x_prompt.shape
    bs, t_new, _ = x_sample.shape
    n_p, n_s = bp * seq, bs * t_new
    n_phys, page = cache_sb_k.shape[1], cache_sb_k.shape[2]
    past_len = page_table.shape[1] * page
    tiles = seq // ATT_TILE
    assert seq % ATT_TILE == 0 and past_len % MOBA_BLOCK == 0 and MOBA_BLOCK % page == 0 and page == LANES
    assert t_new & (t_new - 1) == 0 and GROUP * t_new == 8 and past_len >= POOL_HIST >= t_new
    assert past_len // MOBA_BLOCK <= LANES

    perm = _head_perm()
    col = np.arange(w_in.shape[2])
    q_sb0, q_mb0 = ATT_WIDTH, 2 * ATT_WIDTH + 2 * KV_WIDTH
    col[q_sb0:q_sb0 + ATT_WIDTH] = q_sb0 + perm
    col[q_mb0:q_mb0 + ATT_WIDTH] = q_mb0 + perm
    w_in_b = w_in[:, :, col].astype(BF16)
    w_branch_b = jnp.stack([w_branch[:, 0], w_branch[:, 1][:, perm], w_branch[:, 2][:, perm]], axis=1).astype(BF16)
    w_out_b, w_up_b, w_down_b = w_out.astype(BF16), w_up.astype(BF16), w_down.astype(BF16)
    w_pg_b, w_ple_b, pool_w_b = w_ple_gate.astype(BF16), w_ple.astype(BF16), pool_w.astype(BF16)
    qn = jnp.tile(moba_q_norm, (1, N_HEADS))
    kn = jnp.tile(moba_k_norm, (1, KV_HEADS))
    gmat = _head_group_matrix()

    tm_p = ATT_TILE
    tm_s = 256 if n_s % 256 == 0 else n_s
    cos_p, sin_p = _rope_tables(jnp.arange(seq))
    cos_s, sin_s = _rope_tables(jnp.tile(past_len + jnp.arange(t_new), bs))
    caches = [c.transpose(0, 1, 3, 4, 2).reshape(depth, n_phys, KV_WIDTH, page)
              for c in (cache_sb_k, cache_sb_v, cache_moba_k, cache_moba_v)]
    nbp = -(-tiles // BF16_ROWS) * BF16_ROWS

    xp = x_prompt.reshape(n_p, d)
    xs = x_sample.reshape(n_s, d)
    outs_p = [[] for _ in range(5)]
    outs_s = [[] for _ in range(5)]

    def to_q4(q):
        return q.reshape(bs, t_new, GROUP, KV_HEADS, HEAD_DIM).transpose(0, 3, 2, 1, 4).reshape(
            bs, KV_HEADS, GROUP * t_new, HEAD_DIM)

    def from_q4(r):
        return r.reshape(bs, KV_HEADS, GROUP, t_new, HEAD_DIM).transpose(0, 3, 2, 1, 4).reshape(n_s, ATT_WIDTH)

    def new_t(a):
        at = a.reshape(bs, t_new, KV_WIDTH).transpose(0, 2, 1)
        return jnp.pad(at, ((0, 0), (0, 0), (0, page - t_new)))

    for i in range(depth):
        layer_w = (norm_mix[i][None], w_in_b[i], qn[i][None], kn[i][None])
        post_w = (norm_mlp[i][None], w_up_b[i], w_down_b[i], norm_ple[i][None], w_pg_b[i], w_ple_b[i])
        ps = pool_scale[i][None]

        (u, qt_sb, k_sb, v_sb, k_sb_b, vt_sb, qt_mf, qt_mb, k_mb, v_mb, k_mb_b, vt_mb, gate) = _inproj(
            xp, *layer_w, cos_p, sin_p, gmat, tm_p, True)
        r3 = lambda a: a.reshape(bp, seq, a.shape[-1])
        k3 = lambda a: a.reshape(n_p // ATT_TILE, ATT_TILE, KV_WIDTH)
        a = _pool_prompt(r3(u), pool_w_b[i], ps, min(512, seq)).reshape(n_p, ATT_WIDTH)
        b = _sb_prompt(qt_sb, k3(k_sb_b), vt_sb, bp, tiles).reshape(n_p, ATT_WIDTH)
        c = _moba_prompt(qt_mf, qt_mb, k3(k_mb_b), vt_mb, _kmean(r3(k_mb), nbp), bp, tiles).reshape(n_p, ATT_WIDTH)
        xp = _merge(xp, a, b, c, gate, w_branch_b[i], w_out_b[i], tm_p)
        xp = _mlp_ple(xp, p_prompt[i].reshape(n_p, -1), *post_w, tm_p)
        for lst, val in zip(outs_p, (k_sb, v_sb, k_mb, v_mb, r3(u)[:, seq - POOL_HIST:])):
            lst.append(val)

        (u, q_sb, k_sb, v_sb, q_mb, k_mb, v_mb, gate) = _inproj(xs, *layer_w, cos_s, sin_s, gmat, tm_s, False)
        r3 = lambda a: a.reshape(bs, t_new, a.shape[-1])
        st_t = state_pool[i].transpose(1, 0, 2)
        a = _pool_sample(st_t, r3(u).transpose(1, 0, 2), pool_w_b[i], ps, past_len).transpose(1, 0, 2).reshape(n_s, ATT_WIDTH)
        b = from_q4(_sb_sample(to_q4(q_sb), new_t(k_sb), new_t(v_sb), caches[0], caches[1], page_table, i, t_new))
        c = from_q4(_moba_sample(to_q4(q_mb), new_t(k_mb), new_t(v_mb), caches[2], caches[3], page_table, i, t_new))
        xs = _merge(xs, a, b, c, gate, w_branch_b[i], w_out_b[i], tm_s)
        xs = _mlp_ple(xs, p_sample[i].reshape(n_s, -1), *post_w, tm_s)
        pool_state = jnp.concatenate([state_pool[i][:, t_new:], r3(u)], axis=1)
        for lst, val in zip(outs_s, (k_sb, v_sb, k_mb, v_mb, pool_state)):
            lst.append(val)

    def kv_stack(lst, nb, nt):
        return jnp.stack(lst).reshape(depth, nb, nt, KV_HEADS, HEAD_DIM)

    return (xp.reshape(bp, seq, d), xs.reshape(bs, t_new, d),
            kv_stack(outs_p[0], bp, seq), kv_stack(outs_p[1], bp, seq),
            kv_stack(outs_p[2], bp, seq), kv_stack(outs_p[3], bp, seq), jnp.stack(outs_p[4]),
            kv_stack(outs_s[0], bs, t_new), kv_stack(outs_s[1], bs, t_new),
            kv_stack(outs_s[2], bs, t_new), kv_stack(outs_s[3], bs, t_new), jnp.stack(outs_s[4]))
```

```python
import functools

import jax
import jax.numpy as jnp
import numpy as np
from jax import lax
from jax.experimental import pallas as pl
from jax.experimental.pallas import tpu as pltpu

F32 = jnp.float32
BF16 = jnp.bfloat16

HEAD_DIM = 64
KV_HEADS = 4
GROUP = 2
N_HEADS = KV_HEADS * GROUP
ATT_WIDTH = N_HEADS * HEAD_DIM
KV_WIDTH = KV_HEADS * HEAD_DIM
ATT_SCALE = HEAD_DIM ** -0.5
POOL_WINDOWS = (2, 4, 8, 16)
POOL_GROUP = 128
POOL_HIST = max(POOL_WINDOWS) - 1
ROPE_THETA = 500000.0
ROT_DIM = HEAD_DIM // 4
MOBA_BLOCK = 256
MOBA_TOPK = 3
N_BRANCH = 3
RMS_EPS = 1e-6
NEG_INF = float("-inf")

LANES = 128
BF16_ROWS = 16
ATT_TILE = MOBA_BLOCK
VMEM_LIMIT = 56 * 1024 * 1024


def _cparams(n_axes, vmem=None):
    return pltpu.CompilerParams(dimension_semantics=("arbitrary",) * n_axes, vmem_limit_bytes=vmem)


def _mm(a, b):
    return jnp.dot(a, b, preferred_element_type=F32)


def _mm_nt(a, b):
    return lax.dot_general(a, b, (((1,), (1,)), ((), ())), preferred_element_type=F32)


def _split2(x):
    hi = x.astype(BF16)
    lo = (x - hi.astype(F32)).astype(BF16)
    return hi, lo


def _split3(x):
    hi = x.astype(BF16)
    r = x - hi.astype(F32)
    mid = r.astype(BF16)
    lo = (r - mid.astype(F32)).astype(BF16)
    return hi, mid, lo


def _sigmoid(x):
    return 1.0 / (1.0 + jnp.exp(-x))


def _softplus(z):
    return jnp.maximum(z, 0.0) + jnp.log(1.0 + jnp.exp(jnp.minimum(z, -z)))


def _inproj_kernel(x_ref, gn_ref, w_ref, qn_ref, kn_ref, cos_ref, sin_ref, gmat_ref, *out_refs, prompt):
    x = x_ref[...]
    ms = jnp.mean(x * x, axis=-1, keepdims=True)
    h = (x * lax.rsqrt(ms + RMS_EPS) * gn_ref[...]).astype(BF16)

    def proj(lo, hi):
        return _mm(h, w_ref[:, lo:hi])

    def head_norm_rope(y, g_ref):
        width = y.shape[1]
        gm = gmat_ref[0:width, 0:width]
        p0, p1, p2 = _split3(y * y)
        ms_h = (_mm(p0, gm) + _mm(p1, gm) + _mm(p2, gm)) * (1.0 / HEAD_DIM)
        yn = y * lax.rsqrt(ms_h + RMS_EPS) * g_ref[...]
        reps = width // LANES
        c = jnp.concatenate([cos_ref[...]] * reps, axis=1)
        s = jnp.concatenate([sin_ref[...]] * reps, axis=1)
        lane = lax.broadcasted_iota(jnp.int32, yn.shape, 1)
        first_half = (lane & (HEAD_DIM - 1)) < (ROT_DIM // 2)
        partner = jnp.where(first_half, pltpu.roll(yn, width - ROT_DIM // 2, 1), pltpu.roll(yn, ROT_DIM // 2, 1))
        return yn * c + partner * s

    o = 0
    u = proj(o, o + ATT_WIDTH)
    o += ATT_WIDTH
    q_sb = proj(o, o + ATT_WIDTH) * ATT_SCALE
    o += ATT_WIDTH
    k_sb = proj(o, o + KV_WIDTH)
    o += KV_WIDTH
    v_sb = proj(o, o + KV_WIDTH)
    o += KV_WIDTH
    q_mb = head_norm_rope(proj(o, o + ATT_WIDTH), qn_ref)
    o += ATT_WIDTH
    k_mb = head_norm_rope(proj(o, o + KV_WIDTH), kn_ref)
    o += KV_WIDTH
    v_mb = proj(o, o + KV_WIDTH)
    o += KV_WIDTH
    if prompt:
        (u_ref, qsb_t, ksb_ref, vsb_ref, ksb_b, vsb_t, qmb_tf, qmb_t, kmb_ref, vmb_ref, kmb_b, vmb_t, gate_ref) = out_refs
        qsb_t[0] = q_sb.T.astype(BF16)
        ksb_b[...] = k_sb.astype(BF16)
        vsb_t[0] = v_sb.T.astype(BF16)
        q_mb_t = q_mb.T
        qmb_tf[0] = q_mb_t
        qmb_t[0] = (q_mb_t * ATT_SCALE).astype(BF16)
        kmb_b[...] = k_mb.astype(BF16)
        vmb_t[0] = v_mb.T.astype(BF16)
    else:
        (u_ref, qsb_ref, ksb_ref, vsb_ref, qmb_ref, kmb_ref, vmb_ref, gate_ref) = out_refs
        qsb_ref[...] = q_sb
        qmb_ref[...] = q_mb
    u_ref[...] = u
    ksb_ref[...] = k_sb
    vsb_ref[...] = v_sb
    kmb_ref[...] = k_mb
    vmb_ref[...] = v_mb
    gate_w = gate_ref.shape[1]
    chunk = 512
    for c0 in range(0, gate_w, chunk):
        gate_ref[:, c0:c0 + chunk] = _sigmoid(proj(o + c0, o + c0 + chunk))


def _inproj(x, gn, w, qn, kn, cos_t, sin_t, gmat, tm, prompt):
    n, d = x.shape
    in_w = w.shape[1]
    gate_w = in_w - (3 * ATT_WIDTH + 4 * KV_WIDTH)
    nper = cos_t.shape[0] // tm
    nt = n // tm
    row = lambda i: (i, 0)
    const = lambda i: (0, 0)

    def flat(width, dtype):
        return jax.ShapeDtypeStruct((n, width), dtype), pl.BlockSpec((tm, width), row)

    def transposed(width, dtype):
        return jax.ShapeDtypeStruct((nt, width, tm), dtype), pl.BlockSpec((1, width, tm), lambda i: (i, 0, 0))

    if prompt:
        outs = [flat(ATT_WIDTH, F32), transposed(ATT_WIDTH, BF16), flat(KV_WIDTH, F32), flat(KV_WIDTH, F32),
                flat(KV_WIDTH, BF16), transposed(KV_WIDTH, BF16), transposed(ATT_WIDTH, F32),
                transposed(ATT_WIDTH, BF16), flat(KV_WIDTH, F32), flat(KV_WIDTH, F32), flat(KV_WIDTH, BF16),
                transposed(KV_WIDTH, BF16), flat(gate_w, F32)]
    else:
        outs = [flat(ATT_WIDTH, F32), flat(ATT_WIDTH, F32), flat(KV_WIDTH, F32), flat(KV_WIDTH, F32),
                flat(ATT_WIDTH, F32), flat(KV_WIDTH, F32), flat(KV_WIDTH, F32), flat(gate_w, F32)]
    return pl.pallas_call(
        functools.partial(_inproj_kernel, prompt=prompt),
        grid=(nt,),
        in_specs=[pl.BlockSpec((tm, d), row), pl.BlockSpec((1, d), const), pl.BlockSpec((d, in_w), const),
                  pl.BlockSpec((1, ATT_WIDTH), const), pl.BlockSpec((1, KV_WIDTH), const),
                  pl.BlockSpec((tm, LANES), lambda i: (i % nper, 0)), pl.BlockSpec((tm, LANES), lambda i: (i % nper, 0)),
                  pl.BlockSpec((ATT_WIDTH, ATT_WIDTH), const)],
        out_specs=[s for _, s in outs],
        out_shape=[s for s, _ in outs],
        compiler_params=_cparams(1, VMEM_LIMIT),
        name="inproj_prompt" if prompt else "inproj_sample",
    )(x, gn, w, qn, kn, cos_t, sin_t, gmat)


def _pool_prompt_kernel(cur_ref, prev_ref, pw_ref, ps_ref, o_ref, ext_ref, *, ts):
    t = pl.program_id(1)
    hist = ext_ref.shape[0] - ts
    ext_ref[0:hist, :] = jnp.where(t == 0, 0.0, prev_ref[0])
    ext_ref[hist:, :] = cur_ref[0]
    pos = t * ts + lax.broadcasted_iota(jnp.int32, (ts, POOL_GROUP), 0)
    for g, w in enumerate(POOL_WINDOWS):
        lo, hi = g * POOL_GROUP, (g + 1) * POOL_GROUP
        ws = ext_ref[hist:hist + ts, lo:hi]
        for k in range(1, w):
            ws = ws + ext_ref[hist - k:hist - k + ts, lo:hi]
        cnt = jnp.minimum(pos + 1, w).astype(F32)
        dlt = ws / cnt - ext_ref[hist:hist + ts, lo:hi]
        o_ref[0, :, lo:hi] = _mm(dlt.astype(BF16), pw_ref[g]) * ps_ref[:, lo:hi]


def _pool_prompt(u, pw, ps, ts):
    b, s, c = u.shape
    hist = 16
    ratio = ts // hist
    return pl.pallas_call(
        functools.partial(_pool_prompt_kernel, ts=ts),
        grid=(b, s // ts),
        in_specs=[pl.BlockSpec((1, ts, c), lambda i, t: (i, t, 0)),
                  pl.BlockSpec((1, hist, c), lambda i, t: (i, jnp.maximum(t * ratio - 1, 0), 0)),
                  pl.BlockSpec((len(POOL_WINDOWS), POOL_GROUP, POOL_GROUP), lambda i, t: (0, 0, 0)),
                  pl.BlockSpec((1, c), lambda i, t: (0, 0))],
        out_specs=pl.BlockSpec((1, ts, c), lambda i, t: (i, t, 0)),
        out_shape=jax.ShapeDtypeStruct((b, s, c), F32),
        scratch_shapes=[pltpu.VMEM((ts + hist, c), F32)],
        compiler_params=_cparams(2),
        name="pool_prompt",
    )(u, u, pw, ps)


def _pool_sample_kernel(st_ref, u_ref, pw_ref, ps_ref, o_ref, *, start_pos):
    n_new = u_ref.shape[0]

    def row(idx, lo, hi):
        return st_ref[idx, :, lo:hi] if idx < POOL_HIST else u_ref[idx - POOL_HIST, :, lo:hi]

    for j in range(n_new):
        for g, w in enumerate(POOL_WINDOWS):
            lo, hi = g * POOL_GROUP, (g + 1) * POOL_GROUP
            ws = row(POOL_HIST + j, lo, hi)
            for k in range(1, w):
                ws = ws + row(POOL_HIST + j - k, lo, hi)
            cnt = float(min(start_pos + j + 1, w))
            dlt = ws / cnt - row(POOL_HIST + j, lo, hi)
            o_ref[j, :, lo:hi] = _mm(dlt.astype(BF16), pw_ref[g]) * ps_ref[:, lo:hi]


def _pool_sample(st_t, u_t, pw, ps, start_pos):
    t, b, c = u_t.shape
    full = lambda shape: pl.BlockSpec(shape, lambda i: (0,) * len(shape))
    return pl.pallas_call(
        functools.partial(_pool_sample_kernel, start_pos=start_pos),
        grid=(1,),
        in_specs=[full(st_t.shape), full(u_t.shape), full(pw.shape), full(ps.shape)],
        out_specs=full((t, b, c)),
        out_shape=jax.ShapeDtypeStruct((t, b, c), F32),
        compiler_params=_cparams(1),
        name="pool_sample",
    )(st_t, u_t, pw, ps)


def _load_head_queries(qt_ref, qz_ref):
    tq = qt_ref.shape[2]
    row_head = lax.broadcasted_iota(jnp.int32, (KV_WIDTH, tq), 0) >> 6
    for kvh in range(KV_HEADS):
        for g in range(GROUP):
            qg = qt_ref[0, g * KV_WIDTH:(g + 1) * KV_WIDTH, :]
            qz_ref[kvh, :, g * tq:(g + 1) * tq] = jnp.where(row_head == kvh, qg, jnp.zeros_like(qg))


def _store_heads_out(o_ref, per_head, tq):
    for g in range(GROUP):
        out_t = jnp.concatenate([p[:, g * tq:(g + 1) * tq] for p in per_head], axis=0)
        o_ref[0, :, g * KV_WIDTH:(g + 1) * KV_WIDTH] = out_t.T


def _att_specs(b_axis_tiles, tq):
    nt = b_axis_tiles
    return [pl.BlockSpec((1, ATT_WIDTH, tq), lambda i, j: (i * nt + j, 0, 0)),
            pl.BlockSpec((nt, tq, KV_WIDTH), lambda i, j: (i, 0, 0)),
            pl.BlockSpec((nt, KV_WIDTH, tq), lambda i, j: (i, 0, 0))]


def _suffix_matrix_t(tk):
    r = np.arange(tk)
    later = (r[None, :] > r[:, None]).astype(np.float32)
    return jnp.asarray(np.concatenate([later, np.ones((BF16_ROWS, tk), np.float32)], axis=0), dtype=BF16)


def _sb_prompt_kernel(qt_ref, k_ref, vt_ref, u_ref, o_ref, qz_ref, acc_ref, carry_ref):
    i = pl.program_id(1)
    tk = k_ref.shape[1]
    q2 = qz_ref.shape[2]
    tq = q2 // GROUP
    _load_head_queries(qt_ref, qz_ref)
    acc_ref[...] = jnp.zeros_like(acc_ref)
    carry_ref[...] = jnp.zeros_like(carry_ref)
    key = lax.broadcasted_iota(jnp.int32, (tk, q2), 0)
    qloc = lax.broadcasted_iota(jnp.int32, (tk, q2), 1) & (tq - 1)
    diag_valid = key < qloc
    u2 = u_ref[...]

    def block(j, valid):
        kb = k_ref[j]
        heads = range(KV_HEADS)
        z = [_mm(kb, qz_ref[kvh]) for kvh in heads]
        drop = [_softplus(z[kvh]) for kvh in heads]
        log_take = [z[kvh] - drop[kvh] for kvh in heads]
        if valid is not None:
            drop = [jnp.where(valid, d, 0.0) for d in drop]
        cs = [_mm(u2, d.astype(BF16)) for d in drop]
        w = [jnp.exp(log_take[kvh] - (cs[kvh][0:tk] + carry_ref[kvh, 0:1, :])) for kvh in heads]
        if valid is not None:
            w = [jnp.where(valid, x, 0.0) for x in w]
        for kvh in heads:
            acc_ref[kvh] += _mm(vt_ref[j, kvh * HEAD_DIM:(kvh + 1) * HEAD_DIM, :], w[kvh].astype(BF16))
            carry_ref[kvh] += cs[kvh][tk:tk + 8]

    block(i, diag_valid)

    def body(t, c):
        block(i - 1 - t, None)
        return c

    lax.fori_loop(0, i, body, 0)
    _store_heads_out(o_ref, [acc_ref[kvh] for kvh in range(KV_HEADS)], tq)


def _sb_prompt(qt, k3, vt3, b, tiles):
    tq = k3.shape[1]
    q2 = GROUP * tq
    return pl.pallas_call(
        _sb_prompt_kernel,
        grid=(b, tiles),
        in_specs=_att_specs(tiles, tq) + [pl.BlockSpec((tq + BF16_ROWS, tq), lambda i, j: (0, 0))],
        out_specs=pl.BlockSpec((1, tq, ATT_WIDTH), lambda i, j: (i, j, 0)),
        out_shape=jax.ShapeDtypeStruct((b, tiles * tq, ATT_WIDTH), F32),
        scratch_shapes=[pltpu.VMEM((KV_HEADS, KV_WIDTH, q2), BF16),
                        pltpu.VMEM((KV_HEADS, HEAD_DIM, q2), F32),
                        pltpu.VMEM((KV_HEADS, 8, q2), F32)],
        compiler_params=_cparams(2),
        name="sb_prompt",
    )(qt, k3, vt3, _suffix_matrix_t(tq))


def _page_specs(layer, npages, page, seq, nseq):
    def spec(p):
        return pl.BlockSpec((None, None, KV_WIDTH, page), lambda b, pt: (layer, pt[b * nseq + seq, p], 0, 0))
    return [spec(p) for p in range(npages)]


def _sample_refs(refs, npages, nseq):
    kp = [refs[s * npages:(s + 1) * npages] for s in range(nseq)]
    vp = [refs[(nseq + s) * npages:(nseq + s + 1) * npages] for s in range(nseq)]
    return kp, vp, refs[2 * nseq * npages]


def _sample_queries(q8):
    rows = KV_HEADS * q8.shape[0]
    q = jnp.concatenate([q8] * KV_HEADS, axis=0)
    row = lax.broadcasted_iota(jnp.int32, (rows, KV_WIDTH), 0)
    lane = lax.broadcasted_iota(jnp.int32, (rows, KV_WIDTH), 1)
    return jnp.where((row >> 3) == (lane >> 6), q, 0.0)


def _sample_heads_out(acc):
    n = acc.shape[0] // KV_HEADS
    lane_head = lax.broadcasted_iota(jnp.int32, (n, KV_WIDTH), 1) >> 6
    out = jnp.zeros((n, KV_WIDTH), F32)
    for kvh in range(KV_HEADS):
        out = jnp.where(lane_head == kvh, acc[kvh * n:(kvh + 1) * n, :], out)
    return out


def _sb_sample_kernel(pt_ref, q_ref, ktn_ref, vtn_ref, u_ref, *refs, npages, n_new, nseq):
    del pt_ref
    kp, vp, o_ref = _sample_refs(refs, npages, nseq)
    page = u_ref.shape[0]
    rows = KV_HEADS * q_ref.shape[2]
    row = lax.broadcasted_iota(jnp.int32, (rows, page), 0)
    col = lax.broadcasted_iota(jnp.int32, (rows, page), 1)
    valid_new = col < (row & (n_new - 1))
    work = [(s, blk) for s in range(nseq) for blk in range(npages + 1)]
    qz = [_sample_queries(q_ref[0, s]).astype(BF16) for s in range(nseq)]
    z = [_mm(qz[s], (kp[s][blk][...] if blk < npages else ktn_ref[0, s]).astype(BF16)) for s, blk in work]
    drop = [_softplus(x) for x in z]
    take = [x - d for x, d in zip(z, drop)]
    drop = [jnp.where(valid_new, d, 0.0) if blk == npages else d for d, (s, blk) in zip(drop, work)]
    parts = []
    for d in drop:
        parts.extend(_split2(d))
    cs = _mm(jnp.concatenate(parts, axis=0), u_ref[...])
    acc = [jnp.zeros((rows, KV_WIDTH), F32) for _ in range(nseq)]
    carry = [jnp.zeros((rows, page), F32) for _ in range(nseq)]
    for blk in reversed(range(npages + 1)):
        for s in range(nseq):
            i = work.index((s, blk))
            c = cs[i * 2 * rows:(i + 1) * 2 * rows]
            both = c[0:rows] + c[rows:2 * rows]
            w = jnp.exp(take[i] - (both[:, 0:page] + carry[s]))
            if blk == npages:
                w = jnp.where(valid_new, w, 0.0)
            vt = vp[s][blk][...] if blk < npages else vtn_ref[0, s]
            acc[s] = acc[s] + _mm_nt(w.astype(BF16), vt.astype(BF16))
            carry[s] = carry[s] + both[:, page:]
    for s in range(nseq):
        o_ref[0, s] = _sample_heads_out(acc[s])


def _suffix_matrix(tk):
    r = np.arange(tk)
    strict = (r[:, None] > r[None, :]).astype(np.float32)
    return jnp.asarray(np.concatenate([strict, np.ones((tk, tk), np.float32)], axis=1), dtype=BF16)


def _sample_call(body, name, q5, kt_new, vt_new, consts, cache_k, cache_v, page_table, layer, n_new):
    nsteps, nseq, rows, _ = q5.shape
    npages = page_table.shape[1]
    page = cache_k.shape[3]
    pages = [spec for s in range(nseq) for spec in _page_specs(layer, npages, page, s, nseq)]
    grid_spec = pltpu.PrefetchScalarGridSpec(
        num_scalar_prefetch=1,
        grid=(nsteps,),
        in_specs=[pl.BlockSpec((1, nseq, rows, KV_WIDTH), lambda b, pt: (b, 0, 0, 0)),
                  pl.BlockSpec((1, nseq, KV_WIDTH, page), lambda b, pt: (b, 0, 0, 0)),
                  pl.BlockSpec((1, nseq, KV_WIDTH, page), lambda b, pt: (b, 0, 0, 0))]
                 + [pl.BlockSpec(c.shape, lambda b, pt: (0, 0)) for c in consts] + pages + pages,
        out_specs=pl.BlockSpec((1, nseq, rows, KV_WIDTH), lambda b, pt: (b, 0, 0, 0)))
    return pl.pallas_call(
        functools.partial(body, npages=npages, n_new=n_new, nseq=nseq),
        grid_spec=grid_spec,
        out_shape=jax.ShapeDtypeStruct(q5.shape, F32),
        compiler_params=_cparams(1),
        name=name,
    )(page_table, q5, kt_new, vt_new, *consts, *([cache_k] * (nseq * npages)), *([cache_v] * (nseq * npages)))


def _topk_rows(scores, n_valid, nb):
    blk = lax.broadcasted_iota(jnp.int32, scores.shape, 0)
    valid = blk < n_valid
    s = jnp.where(valid, scores, NEG_INF)
    rank = jnp.zeros(scores.shape, jnp.int32)
    for m in range(nb):
        other = s[m:m + 1, :]
        beats = (other > s) | ((other == s) & (blk > m))
        rank = rank + jnp.where(beats, 1, 0)
    return jnp.where(valid & (rank < MOBA_TOPK), 1.0, 0.0)


def _topk_lanes(scores, n_valid, nb):
    blk = lax.broadcasted_iota(jnp.int32, scores.shape, 1)
    valid = blk < n_valid
    s = jnp.where(valid, scores, NEG_INF)
    rank = jnp.zeros(scores.shape, jnp.int32)
    for m in range(nb):
        other = s[:, m:m + 1]
        beats = (other > s) | ((other == s) & (blk > m))
        rank = rank + jnp.where(beats, 1, 0)
    return jnp.where(valid & (rank < MOBA_TOPK), 1.0, 0.0)


def _kmean_kernel(k_ref, o_ref, *, nb):
    o_ref[...] = jnp.zeros_like(o_ref)
    for n in range(nb):
        blk = k_ref[0, n * MOBA_BLOCK:(n + 1) * MOBA_BLOCK, :]
        o_ref[0, n:n + 1, :] = jnp.sum(blk, axis=0, keepdims=True) * (1.0 / MOBA_BLOCK)


def _kmean(k, nbp):
    b, s, _ = k.shape
    nb = s // MOBA_BLOCK
    return pl.pallas_call(
        functools.partial(_kmean_kernel, nb=nb),
        grid=(b,),
        in_specs=[pl.BlockSpec((1, s, KV_WIDTH), lambda i: (i, 0, 0))],
        out_specs=pl.BlockSpec((1, nbp, KV_WIDTH), lambda i: (i, 0, 0)),
        out_shape=jax.ShapeDtypeStruct((b, nbp, KV_WIDTH), F32),
        compiler_params=_cparams(1),
        name="moba_kmean",
    )(k)


def _moba_prompt_kernel(qtf_ref, qt_ref, k_ref, vt_ref, km_ref, o_ref,
                        qz_ref, sel_ref, m_ref, l_ref, acc_ref, *, nb):
    qblk = pl.program_id(1)
    tk = k_ref.shape[1]
    q2 = qz_ref.shape[2]
    tq = q2 // GROUP
    _load_head_queries(qt_ref, qz_ref)
    km_hi, km_lo = _split2(km_ref[0])
    row_head = lax.broadcasted_iota(jnp.int32, (KV_WIDTH, tq), 0) >> 6
    heads = range(KV_HEADS)
    qf = [jnp.concatenate(
        [jnp.where(row_head == kvh, qtf_ref[0, g * KV_WIDTH:(g + 1) * KV_WIDTH, :], 0.0) for g in range(GROUP)],
        axis=1) for kvh in heads]
    q_parts = [_split2(q) for q in qf]
    scores = [_mm(km_hi, q_hi) + (_mm(km_hi, q_lo) + _mm(km_lo, q_hi)) for q_hi, q_lo in q_parts]
    for kvh in heads:
        sel_ref[kvh] = _topk_rows(scores[kvh], qblk, nb)
    key = lax.broadcasted_iota(jnp.int32, (tk, q2), 0)
    qloc = lax.broadcasted_iota(jnp.int32, (tk, q2), 1) & (tq - 1)
    causal = key <= qloc
    heads = range(KV_HEADS)
    kb = k_ref[qblk]
    st = [jnp.where(causal, _mm(kb, qz_ref[kvh]), NEG_INF) for kvh in heads]
    m0 = [jnp.max(s, axis=0, keepdims=True) for s in st]
    p = [jnp.exp(st[kvh] - m0[kvh]) for kvh in heads]
    for kvh in heads:
        m_ref[kvh] = m0[kvh]
        l_ref[kvh] = jnp.sum(p[kvh], axis=0, keepdims=True)
        acc_ref[kvh] = _mm(vt_ref[qblk, kvh * HEAD_DIM:(kvh + 1) * HEAD_DIM, :], p[kvh].astype(BF16))

    def body(n, c):
        kb = k_ref[n]
        bias = [jnp.where(sel_ref[kvh, pl.ds(n, 1), :] > 0.5, 0.0, NEG_INF) for kvh in heads]
        st = [_mm(kb, qz_ref[kvh]) + bias[kvh] for kvh in heads]
        m_old = [m_ref[kvh] for kvh in heads]
        m_new = [jnp.maximum(m_old[kvh], jnp.max(st[kvh], axis=0, keepdims=True)) for kvh in heads]
        p = [jnp.exp(st[kvh] - m_new[kvh]) for kvh in heads]
        alpha = [jnp.exp(m_old[kvh] - m_new[kvh]) for kvh in heads]
        pv = [_mm(vt_ref[n, kvh * HEAD_DIM:(kvh + 1) * HEAD_DIM, :], p[kvh].astype(BF16)) for kvh in heads]
        for kvh in heads:
            l_ref[kvh] = alpha[kvh] * l_ref[kvh] + jnp.sum(p[kvh], axis=0, keepdims=True)
            acc_ref[kvh] = alpha[kvh] * acc_ref[kvh] + pv[kvh]
            m_ref[kvh] = m_new[kvh]
        return c

    lax.fori_loop(0, qblk, body, 0)
    _store_heads_out(o_ref, [acc_ref[kvh] / l_ref[kvh] for kvh in range(KV_HEADS)], tq)


def _moba_prompt(qtf, qt, k3, vt3, kmean, b, tiles):
    tq = k3.shape[1]
    q2 = GROUP * tq
    nbp = kmean.shape[1]
    return pl.pallas_call(
        functools.partial(_moba_prompt_kernel, nb=tiles),
        grid=(b, tiles),
        in_specs=[pl.BlockSpec((1, ATT_WIDTH, tq), lambda i, j: (i * tiles + j, 0, 0))] + _att_specs(tiles, tq)
                 + [pl.BlockSpec((1, nbp, KV_WIDTH), lambda i, j: (i, 0, 0))],
        out_specs=pl.BlockSpec((1, tq, ATT_WIDTH), lambda i, j: (i, j, 0)),
        out_shape=jax.ShapeDtypeStruct((b, tiles * tq, ATT_WIDTH), F32),
        scratch_shapes=[pltpu.VMEM((KV_HEADS, KV_WIDTH, q2), BF16), pltpu.VMEM((KV_HEADS, nbp, q2), F32),
                        pltpu.VMEM((KV_HEADS, 1, q2), F32), pltpu.VMEM((KV_HEADS, 1, q2), F32),
                        pltpu.VMEM((KV_HEADS, HEAD_DIM, q2), F32)],
        compiler_params=_cparams(2),
        name="moba_prompt",
    )(qtf, qt, k3, vt3, kmean)


def _moba_sample_kernel(pt_ref, q_ref, ktn_ref, vtn_ref, *refs, npages, n_new, nseq):
    del pt_ref
    kp, vp, o_ref = _sample_refs(refs, npages, nseq)
    page = ktn_ref.shape[3]
    per_blk = MOBA_BLOCK // page
    nb_past = npages // per_blk
    rows = KV_HEADS * q_ref.shape[2]
    seqs = range(nseq)
    qf = [_sample_queries(q_ref[0, s]) for s in seqs]
    qz = [(q * ATT_SCALE).astype(BF16) for q in qf]
    ones = jnp.ones((page, LANES), BF16)
    lane = lax.broadcasted_iota(jnp.int32, (KV_WIDTH, LANES), 1)
    k_parts = [[_split2(kp[s][pg][...]) for pg in range(npages)] for s in seqs]
    k_sums = [[_mm(hi, ones) + _mm(lo, ones) for hi, lo in k_parts[s]] for s in seqs]
    km = []
    for s in seqs:
        km_s = jnp.zeros((KV_WIDTH, LANES), F32)
        for blk in range(nb_past):
            tot = k_sums[s][blk * per_blk]
            for r in range(1, per_blk):
                tot = tot + k_sums[s][blk * per_blk + r]
            km_s = jnp.where(lane == blk, tot * (1.0 / MOBA_BLOCK), km_s)
        km.append(km_s)
    sel = []
    for s in seqs:
        q_hi, q_lo = _split2(qf[s])
        km_hi, km_lo = _split2(km[s])
        scores = _mm(q_hi, km_hi) + (_mm(q_hi, km_lo) + _mm(q_lo, km_hi))
        sel.append(_topk_lanes(scores, nb_past, nb_past))
    row = lax.broadcasted_iota(jnp.int32, (rows, page), 0)
    col = lax.broadcasted_iota(jnp.int32, (rows, page), 1)
    visible = (col <= (row & (n_new - 1))) & (col < n_new)
    logits = [[jnp.where(sel[s][:, pg // per_blk:pg // per_blk + 1] > 0.5,
                         _mm(qz[s], kp[s][pg][...].astype(BF16)), NEG_INF) for pg in range(npages)]
              + [jnp.where(visible, _mm(qz[s], ktn_ref[0, s].astype(BF16)), NEG_INF)] for s in seqs]
    m = []
    for s in seqs:
        m_el = logits[s][0]
        for x in logits[s][1:]:
            m_el = jnp.maximum(m_el, x)
        m.append(jnp.max(m_el, axis=1, keepdims=True))
    acc = [jnp.zeros((rows, KV_WIDTH), F32) for _ in seqs]
    l_el = [jnp.zeros((rows, page), F32) for _ in seqs]
    for pg in range(npages + 1):
        for s in seqs:
            p = jnp.exp(logits[s][pg] - m[s])
            l_el[s] = l_el[s] + p
            vt = vp[s][pg][...] if pg < npages else vtn_ref[0, s]
            acc[s] = acc[s] + _mm_nt(p.astype(BF16), vt.astype(BF16))
    for s in seqs:
        o_ref[0, s] = _sample_heads_out(acc[s] / jnp.sum(l_el[s], axis=1, keepdims=True))


def _merge_kernel(x_ref, a_ref, b_ref, c_ref, g_ref, wb_ref, wo_ref, o_ref):
    d = x_ref.shape[1]
    m = g_ref[:, 0:d] * _mm(a_ref[...].astype(BF16), wb_ref[0])
    m = m + g_ref[:, d:2 * d] * _mm(b_ref[...].astype(BF16), wb_ref[1])
    m = m + g_ref[:, 2 * d:3 * d] * _mm(c_ref[...].astype(BF16), wb_ref[2])
    o_ref[...] = x_ref[...] + _mm(m.astype(BF16), wo_ref[...])


def _merge(x, a, b, c, gate, wb, wo, tm):
    n, d = x.shape
    row = lambda i: (i, 0)
    return pl.pallas_call(
        _merge_kernel,
        grid=(n // tm,),
        in_specs=[pl.BlockSpec((tm, d), row), pl.BlockSpec((tm, ATT_WIDTH), row), pl.BlockSpec((tm, ATT_WIDTH), row),
                  pl.BlockSpec((tm, ATT_WIDTH), row), pl.BlockSpec((tm, N_BRANCH * d), row),
                  pl.BlockSpec(wb.shape, lambda i: (0, 0, 0)), pl.BlockSpec(wo.shape, lambda i: (0, 0))],
        out_specs=pl.BlockSpec((tm, d), row),
        out_shape=jax.ShapeDtypeStruct((n, d), F32),
        compiler_params=_cparams(1, VMEM_LIMIT),
        name="merge",
    )(x, a, b, c, gate, wb, wo)


def _mlp_ple_kernel(x_ref, p_ref, gm_ref, wu_ref, wd_ref, gp_ref, wg_ref, wp_ref, o_ref, *, f_chunk):
    x = x_ref[...]
    ms = jnp.mean(x * x, axis=-1, keepdims=True)
    h = (x * lax.rsqrt(ms + RMS_EPS) * gm_ref[...]).astype(BF16)
    d_ff = wu_ref.shape[1]
    y = x
    for f0 in range(0, d_ff, f_chunk):
        hid = jnp.maximum(_mm(h, wu_ref[:, f0:f0 + f_chunk]), 0.0)
        y = y + _mm((hid * hid).astype(BF16), wd_ref[f0:f0 + f_chunk, :])
    ms2 = jnp.mean(y * y, axis=-1, keepdims=True)
    h2 = (y * lax.rsqrt(ms2 + RMS_EPS) * gp_ref[...]).astype(BF16)
    gate = _sigmoid(_mm(h2, wg_ref[...]))
    o_ref[...] = y + gate * _mm(p_ref[...].astype(BF16), wp_ref[...])


def _mlp_ple(x, p, gm, wu, wd, gp, wg, wp, tm):
    n, d = x.shape
    row = lambda i: (i, 0)
    const = lambda i: (0, 0)
    return pl.pallas_call(
        functools.partial(_mlp_ple_kernel, f_chunk=512),
        grid=(n // tm,),
        in_specs=[pl.BlockSpec((tm, d), row), pl.BlockSpec((tm, p.shape[1]), row), pl.BlockSpec((1, d), const),
                  pl.BlockSpec(wu.shape, const), pl.BlockSpec(wd.shape, const), pl.BlockSpec((1, d), const),
                  pl.BlockSpec(wg.shape, const), pl.BlockSpec(wp.shape, const)],
        out_specs=pl.BlockSpec((tm, d), row),
        out_shape=jax.ShapeDtypeStruct((n, d), F32),
        compiler_params=_cparams(1, VMEM_LIMIT),
        name="mlp_ple",
    )(x, p, gm, wu, wd, gp, wg, wp)


def _rope_tables(pos):
    half = ROT_DIM // 2
    inv = ROPE_THETA ** (-jnp.arange(half, dtype=F32) / half)
    ang = pos.astype(F32)[:, None] * inv[None, :]
    cos, sin = jnp.cos(ang), jnp.sin(ang)
    t = pos.shape[0]
    c64 = jnp.concatenate([cos, cos, jnp.ones((t, HEAD_DIM - ROT_DIM), F32)], axis=1)
    s64 = jnp.concatenate([-sin, sin, jnp.zeros((t, HEAD_DIM - ROT_DIM), F32)], axis=1)
    reps = LANES // HEAD_DIM
    return jnp.tile(c64, (1, reps)), jnp.tile(s64, (1, reps))


def _head_perm():
    heads = [kvh * GROUP + g for g in range(GROUP) for kvh in range(KV_HEADS)]
    return np.concatenate([np.arange(h * HEAD_DIM, (h + 1) * HEAD_DIM) for h in heads])


def _head_group_matrix():
    r = np.arange(ATT_WIDTH) // HEAD_DIM
    return jnp.asarray((r[:, None] == r[None, :]).astype(np.float32), dtype=BF16)


def kernel(x_prompt, x_sample, p_prompt, p_sample, cache_sb_k, cache_sb_v, cache_moba_k, cache_moba_v,
           state_pool, page_table, norm_mix, w_in, pool_w, pool_scale, moba_q_norm, moba_k_norm,
           w_branch, w_out, norm_mlp, w_up, w_down, norm_ple, w_ple_gate, w_ple):
    depth = w_in.shape[0]
    bp, seq, d = x_prompt.shape
    bs, t_new, _ = x_sample.shape
    n_p, n_s = bp * seq, bs * t_new
    n_phys, page = cache_sb_k.shape[1], cache_sb_k.shape[2]
    past_len = page_table.shape[1] * page
    tiles = seq // ATT_TILE
    assert seq % ATT_TILE == 0 and past_len % MOBA_BLOCK == 0 and MOBA_BLOCK % page == 0 and page == LANES
    assert t_new & (t_new - 1) == 0 and GROUP * t_new == 8 and past_len >= POOL_HIST >= t_new
    assert past_len // MOBA_BLOCK <= LANES

    perm = _head_perm()
    col = np.arange(w_in.shape[2])
    q_sb0, q_mb0 = ATT_WIDTH, 2 * ATT_WIDTH + 2 * KV_WIDTH
    col[q_sb0:q_sb0 + ATT_WIDTH] = q_sb0 + perm
    col[q_mb0:q_mb0 + ATT_WIDTH] = q_mb0 + perm
    w_in_b = w_in[:, :, col].astype(BF16)
    w_branch_b = jnp.stack([w_branch[:, 0], w_branch[:, 1][:, perm], w_branch[:, 2][:, perm]], axis=1).astype(BF16)
    w_out_b, w_up_b, w_down_b = w_out.astype(BF16), w_up.astype(BF16), w_down.astype(BF16)
    w_pg_b, w_ple_b, pool_w_b = w_ple_gate.astype(BF16), w_ple.astype(BF16), pool_w.astype(BF16)
    qn = jnp.tile(moba_q_norm, (1, N_HEADS))
    kn = jnp.tile(moba_k_norm, (1, KV_HEADS))
    gmat = _head_group_matrix()

    tm_p = ATT_TILE
    tm_s = 256 if n_s % 256 == 0 else n_s
    cos_p, sin_p = _rope_tables(jnp.arange(seq))
    cos_s, sin_s = _rope_tables(jnp.tile(past_len + jnp.arange(t_new), bs))
    caches = [c.transpose(0, 1, 3, 4, 2).reshape(depth, n_phys, KV_WIDTH, page)
              for c in (cache_sb_k, cache_sb_v, cache_moba_k, cache_moba_v)]
    nbp = -(-tiles // BF16_ROWS) * BF16_ROWS

    xp = x_prompt.reshape(n_p, d)
    xs = x_sample.reshape(n_s, d)
    outs_p = [[] for _ in range(5)]
    outs_s = [[] for _ in range(5)]

    nseq = 2 if bs % 2 == 0 else 1

    def to_q5(q):
        return q.reshape(bs, t_new, GROUP, KV_WIDTH).transpose(0, 2, 1, 3).reshape(
            bs // nseq, nseq, GROUP * t_new, KV_WIDTH)

    def from_q5(r):
        return r.reshape(bs, GROUP, t_new, KV_WIDTH).transpose(0, 2, 1, 3).reshape(n_s, ATT_WIDTH)

    def new_t(a):
        at = a.reshape(bs, t_new, KV_WIDTH).transpose(0, 2, 1)
        return jnp.pad(at, ((0, 0), (0, 0), (0, page - t_new))).reshape(bs // nseq, nseq, KV_WIDTH, page)

    sb_consts = [_suffix_matrix(page)]

    for i in range(depth):
        layer_w = (norm_mix[i][None], w_in_b[i], qn[i][None], kn[i][None])
        post_w = (norm_mlp[i][None], w_up_b[i], w_down_b[i], norm_ple[i][None], w_pg_b[i], w_ple_b[i])
        ps = pool_scale[i][None]

        (u, qt_sb, k_sb, v_sb, k_sb_b, vt_sb, qt_mf, qt_mb, k_mb, v_mb, k_mb_b, vt_mb, gate) = _inproj(
            xp, *layer_w, cos_p, sin_p, gmat, tm_p, True)
        r3 = lambda a: a.reshape(bp, seq, a.shape[-1])
        k3 = lambda a: a.reshape(n_p // ATT_TILE, ATT_TILE, KV_WIDTH)
        a = _pool_prompt(r3(u), pool_w_b[i], ps, min(512, seq)).reshape(n_p, ATT_WIDTH)
        b = _sb_prompt(qt_sb, k3(k_sb_b), vt_sb, bp, tiles).reshape(n_p, ATT_WIDTH)
        c = _moba_prompt(qt_mf, qt_mb, k3(k_mb_b), vt_mb, _kmean(r3(k_mb), nbp), bp, tiles).reshape(n_p, ATT_WIDTH)
        xp = _merge(xp, a, b, c, gate, w_branch_b[i], w_out_b[i], tm_p)
        xp = _mlp_ple(xp, p_prompt[i].reshape(n_p, -1), *post_w, tm_p)
        for lst, val in zip(outs_p, (k_sb, v_sb, k_mb, v_mb, r3(u)[:, seq - POOL_HIST:])):
            lst.append(val)

        (u, q_sb, k_sb, v_sb, q_mb, k_mb, v_mb, gate) = _inproj(xs, *layer_w, cos_s, sin_s, gmat, tm_s, False)
        r3 = lambda a: a.reshape(bs, t_new, a.shape[-1])
        st_t = state_pool[i].transpose(1, 0, 2)
        a = _pool_sample(st_t, r3(u).transpose(1, 0, 2), pool_w_b[i], ps, past_len).transpose(1, 0, 2).reshape(n_s, ATT_WIDTH)
        b = from_q5(_sample_call(_sb_sample_kernel, "sb_sample", to_q5(q_sb), new_t(k_sb), new_t(v_sb), sb_consts,
                                 caches[0], caches[1], page_table, i, t_new))
        c = from_q5(_sample_call(_moba_sample_kernel, "moba_sample", to_q5(q_mb), new_t(k_mb), new_t(v_mb), [],
                                 caches[2], caches[3], page_table, i, t_new))
        xs = _merge(xs, a, b, c, gate, w_branch_b[i], w_out_b[i], tm_s)
        xs = _mlp_ple(xs, p_sample[i].reshape(n_s, -1), *post_w, tm_s)
        pool_state = jnp.concatenate([state_pool[i][:, t_new:], r3(u)], axis=1)
        for lst, val in zip(outs_s, (k_sb, v_sb, k_mb, v_mb, pool_state)):
            lst.append(val)

    def kv_stack(lst, nb, nt):
        return jnp.stack(lst).reshape(depth, nb, nt, KV_HEADS, HEAD_DIM)

    return (xp.reshape(bp, seq, d), xs.reshape(bs, t_new, d),
            kv_stack(outs_p[0], bp, seq), kv_stack(outs_p[1], bp, seq),
            kv_stack(outs_p[2], bp, seq), kv_stack(outs_p[3], bp, seq), jnp.stack(outs_p[4]),
            kv_stack(outs_s[0], bs, t_new), kv_stack(outs_s[1], bs, t_new),
            kv_stack(outs_s[2], bs, t_new), kv_stack(outs_s[3], bs, t_new), jnp.stack(outs_s[4]))
```

```python
import functools

import jax
import jax.numpy as jnp
import numpy as np
from jax import lax
from jax.experimental import pallas as pl
from jax.experimental.pallas import tpu as pltpu

F32 = jnp.float32
BF16 = jnp.bfloat16

HEAD_DIM = 64
KV_HEADS = 4
GROUP = 2
N_HEADS = KV_HEADS * GROUP
ATT_WIDTH = N_HEADS * HEAD_DIM
KV_WIDTH = KV_HEADS * HEAD_DIM
ATT_SCALE = HEAD_DIM ** -0.5
POOL_WINDOWS = (2, 4, 8, 16)
POOL_GROUP = 128
POOL_HIST = max(POOL_WINDOWS) - 1
ROPE_THETA = 500000.0
ROT_DIM = HEAD_DIM // 4
MOBA_BLOCK = 256
MOBA_TOPK = 3
N_BRANCH = 3
RMS_EPS = 1e-6
NEG_INF = float("-inf")

LANES = 128
BF16_ROWS = 16
ATT_TILE = MOBA_BLOCK
VMEM_LIMIT = 56 * 1024 * 1024


def _cparams(n_axes, vmem=None):
    return pltpu.CompilerParams(dimension_semantics=("arbitrary",) * n_axes, vmem_limit_bytes=vmem)


def _layer_spec(arr, layer):
    zeros = (0,) * (arr.ndim - 1)
    return pl.BlockSpec((None,) + arr.shape[1:], lambda *_: (layer,) + zeros)


def _mm(a, b):
    return jnp.dot(a, b, preferred_element_type=F32)


def _mm_nt(a, b):
    return lax.dot_general(a, b, (((1,), (1,)), ((), ())), preferred_element_type=F32)


def _split2(x):
    hi = x.astype(BF16)
    lo = (x - hi.astype(F32)).astype(BF16)
    return hi, lo


def _split3(x):
    hi = x.astype(BF16)
    r = x - hi.astype(F32)
    mid = r.astype(BF16)
    lo = (r - mid.astype(F32)).astype(BF16)
    return hi, mid, lo


def _sigmoid(x):
    return 1.0 / (1.0 + jnp.exp(-x))


def _softplus(z):
    return jnp.maximum(z, 0.0) + jnp.log(1.0 + jnp.exp(jnp.minimum(z, -z)))


def _inproj_kernel(x_ref, gn_ref, w_ref, qn_ref, kn_ref, cos_ref, sin_ref, gmat_ref, *refs, prompt, n_alias):
    out_refs = refs[n_alias:]
    x = x_ref[...]
    ms = jnp.mean(x * x, axis=-1, keepdims=True)
    h = (x * lax.rsqrt(ms + RMS_EPS) * gn_ref[...]).astype(BF16)

    def proj(lo, hi):
        return _mm(h, w_ref[:, lo:hi])

    def head_norm_rope(y, g_ref):
        width = y.shape[1]
        gm = gmat_ref[0:width, 0:width]
        p0, p1, p2 = _split3(y * y)
        ms_h = (_mm(p0, gm) + _mm(p1, gm) + _mm(p2, gm)) * (1.0 / HEAD_DIM)
        yn = y * lax.rsqrt(ms_h + RMS_EPS) * g_ref[...]
        reps = width // LANES
        c = jnp.concatenate([cos_ref[...]] * reps, axis=1)
        s = jnp.concatenate([sin_ref[...]] * reps, axis=1)
        lane = lax.broadcasted_iota(jnp.int32, yn.shape, 1)
        first_half = (lane & (HEAD_DIM - 1)) < (ROT_DIM // 2)
        partner = jnp.where(first_half, pltpu.roll(yn, width - ROT_DIM // 2, 1), pltpu.roll(yn, ROT_DIM // 2, 1))
        return yn * c + partner * s

    o = 0
    u = proj(o, o + ATT_WIDTH)
    o += ATT_WIDTH
    q_sb = proj(o, o + ATT_WIDTH) * ATT_SCALE
    o += ATT_WIDTH
    k_sb = proj(o, o + KV_WIDTH)
    o += KV_WIDTH
    v_sb = proj(o, o + KV_WIDTH)
    o += KV_WIDTH
    q_mb = head_norm_rope(proj(o, o + ATT_WIDTH), qn_ref)
    o += ATT_WIDTH
    k_mb = head_norm_rope(proj(o, o + KV_WIDTH), kn_ref)
    o += KV_WIDTH
    v_mb = proj(o, o + KV_WIDTH)
    o += KV_WIDTH
    if prompt:
        (u_ref, qsb_t, ksb_b, vsb_tb, qmb_tf, qmb_t, kmb_b, vmb_tb, kmean_ref, gate_ref,
         ksb_t, vsb_t, kmb_t, vmb_t) = out_refs
        qsb_t[0] = q_sb.T.astype(BF16)
        ksb_b[...] = k_sb.astype(BF16)
        ksb_t[...] = k_sb.T
        v_t = v_sb.T
        vsb_t[...] = v_t
        vsb_tb[0] = v_t.astype(BF16)
        q_mb_t = q_mb.T
        qmb_tf[0] = q_mb_t
        qmb_t[0] = (q_mb_t * ATT_SCALE).astype(BF16)
        kmb_b[...] = k_mb.astype(BF16)
        kmb_t[...] = k_mb.T
        kmean_ref[0] = jnp.sum(k_mb, axis=0, keepdims=True) * (1.0 / k_mb.shape[0])
        v_t = v_mb.T
        vmb_t[...] = v_t
        vmb_tb[0] = v_t.astype(BF16)
    else:
        (u_ref, qsb_ref, ksb_ref, vsb_ref, qmb_ref, kmb_ref, vmb_ref, gate_ref) = out_refs
        qsb_ref[...] = q_sb
        qmb_ref[...] = q_mb
        ksb_ref[...] = k_sb
        vsb_ref[...] = v_sb
        kmb_ref[...] = k_mb
        vmb_ref[...] = v_mb
    u_ref[...] = u
    gate_w = gate_ref.shape[1]
    chunk = 512
    for c0 in range(0, gate_w, chunk):
        gate_ref[:, c0:c0 + chunk] = _sigmoid(proj(o + c0, o + c0 + chunk))


def _inproj(x, gn, w, qn, kn, cos_t, sin_t, gmat, tm, layer, kv_stacks=None, stack_shape=None):
    prompt = stack_shape is not None
    n, d = x.shape
    in_w = w.shape[2]
    gate_w = in_w - (3 * ATT_WIDTH + 4 * KV_WIDTH)
    nper = cos_t.shape[0] // tm
    nt = n // tm
    row = lambda i: (i, 0)
    const = lambda i: (0, 0)

    def flat(width, dtype):
        return jax.ShapeDtypeStruct((n, width), dtype), pl.BlockSpec((tm, width), row)

    def transposed(width, dtype):
        return jax.ShapeDtypeStruct((nt, width, tm), dtype), pl.BlockSpec((1, width, tm), lambda i: (i, 0, 0))

    alias_in, aliases = [], {}
    if prompt:
        tiles = stack_shape[3] // tm
        stacked = (jax.ShapeDtypeStruct(stack_shape, F32),
                   pl.BlockSpec((None, None, KV_WIDTH, tm), lambda i: (layer, i // tiles, 0, i % tiles)))
        outs = [flat(ATT_WIDTH, F32), transposed(ATT_WIDTH, BF16), flat(KV_WIDTH, BF16), transposed(KV_WIDTH, BF16),
                transposed(ATT_WIDTH, F32), transposed(ATT_WIDTH, BF16), flat(KV_WIDTH, BF16),
                transposed(KV_WIDTH, BF16),
                (jax.ShapeDtypeStruct((nt, 1, KV_WIDTH), F32), pl.BlockSpec((1, 1, KV_WIDTH), lambda i: (i, 0, 0))),
                flat(gate_w, F32), stacked, stacked, stacked, stacked]
        if kv_stacks is not None:
            alias_in = list(kv_stacks)
            aliases = {8 + k: len(outs) - 4 + k for k in range(4)}
    else:
        outs = [flat(ATT_WIDTH, F32), flat(ATT_WIDTH, F32), flat(KV_WIDTH, F32), flat(KV_WIDTH, F32),
                flat(ATT_WIDTH, F32), flat(KV_WIDTH, F32), flat(KV_WIDTH, F32), flat(gate_w, F32)]
    return pl.pallas_call(
        functools.partial(_inproj_kernel, prompt=prompt, n_alias=len(alias_in)),
        grid=(nt,),
        in_specs=[pl.BlockSpec((tm, d), row), _layer_spec(gn, layer), _layer_spec(w, layer),
                  _layer_spec(qn, layer), _layer_spec(kn, layer),
                  pl.BlockSpec((tm, LANES), lambda i: (i % nper, 0)), pl.BlockSpec((tm, LANES), lambda i: (i % nper, 0)),
                  pl.BlockSpec((ATT_WIDTH, ATT_WIDTH), const)]
                 + [pl.BlockSpec(memory_space=pl.ANY)] * len(alias_in),
        out_specs=[s for _, s in outs],
        out_shape=[s for s, _ in outs],
        input_output_aliases=aliases,
        compiler_params=_cparams(1, VMEM_LIMIT),
        name="inproj_prompt" if prompt else "inproj_sample",
    )(x, gn, w, qn, kn, cos_t, sin_t, gmat, *alias_in)


def _pool_prompt_kernel(cur_ref, prev_ref, pw_ref, ps_ref, o_ref, ext_ref, *, ts):
    t = pl.program_id(1)
    hist = ext_ref.shape[0] - ts
    ext_ref[0:hist, :] = jnp.where(t == 0, 0.0, prev_ref[0])
    ext_ref[hist:, :] = cur_ref[0]
    pos = t * ts + lax.broadcasted_iota(jnp.int32, (ts, POOL_GROUP), 0)
    for g, w in enumerate(POOL_WINDOWS):
        lo, hi = g * POOL_GROUP, (g + 1) * POOL_GROUP
        ws = ext_ref[hist:hist + ts, lo:hi]
        for k in range(1, w):
            ws = ws + ext_ref[hist - k:hist - k + ts, lo:hi]
        cnt = jnp.minimum(pos + 1, w).astype(F32)
        dlt = ws / cnt - ext_ref[hist:hist + ts, lo:hi]
        o_ref[0, :, lo:hi] = _mm(dlt.astype(BF16), pw_ref[g]) * ps_ref[:, lo:hi]


def _pool_prompt(u, pw, ps, ts, layer):
    b, s, c = u.shape
    hist = 16
    ratio = ts // hist
    return pl.pallas_call(
        functools.partial(_pool_prompt_kernel, ts=ts),
        grid=(b, s // ts),
        in_specs=[pl.BlockSpec((1, ts, c), lambda i, t: (i, t, 0)),
                  pl.BlockSpec((1, hist, c), lambda i, t: (i, jnp.maximum(t * ratio - 1, 0), 0)),
                  _layer_spec(pw, layer), _layer_spec(ps, layer)],
        out_specs=pl.BlockSpec((1, ts, c), lambda i, t: (i, t, 0)),
        out_shape=jax.ShapeDtypeStruct((b, s, c), F32),
        scratch_shapes=[pltpu.VMEM((ts + hist, c), F32)],
        compiler_params=_cparams(2),
        name="pool_prompt",
    )(u, u, pw, ps)


def _pool_sample_kernel(st_ref, u_ref, pw_ref, ps_ref, o_ref, *, start_pos):
    n_new = u_ref.shape[0]

    def row(idx, lo, hi):
        return st_ref[idx, :, lo:hi] if idx < POOL_HIST else u_ref[idx - POOL_HIST, :, lo:hi]

    for j in range(n_new):
        for g, w in enumerate(POOL_WINDOWS):
            lo, hi = g * POOL_GROUP, (g + 1) * POOL_GROUP
            ws = row(POOL_HIST + j, lo, hi)
            for k in range(1, w):
                ws = ws + row(POOL_HIST + j - k, lo, hi)
            cnt = float(min(start_pos + j + 1, w))
            dlt = ws / cnt - row(POOL_HIST + j, lo, hi)
            o_ref[j, :, lo:hi] = _mm(dlt.astype(BF16), pw_ref[g]) * ps_ref[:, lo:hi]


def _pool_sample(st_t, u_t, pw, ps, start_pos, layer):
    t, b, c = u_t.shape
    full = lambda shape: pl.BlockSpec(shape, lambda i: (0,) * len(shape))
    return pl.pallas_call(
        functools.partial(_pool_sample_kernel, start_pos=start_pos),
        grid=(1,),
        in_specs=[full(st_t.shape), full(u_t.shape), _layer_spec(pw, layer), _layer_spec(ps, layer)],
        out_specs=full((t, b, c)),
        out_shape=jax.ShapeDtypeStruct((t, b, c), F32),
        compiler_params=_cparams(1),
        name="pool_sample",
    )(st_t, u_t, pw, ps)


def _load_head_queries(qt_ref, qz_ref):
    tq = qt_ref.shape[2]

    @pl.when((pl.program_id(0) == 0) & (pl.program_id(1) == 0))
    def _():
        qz_ref[...] = jnp.zeros_like(qz_ref)

    for kvh in range(KV_HEADS):
        for g in range(GROUP):
            h = kvh * GROUP + g
            qz_ref[kvh, kvh * HEAD_DIM:(kvh + 1) * HEAD_DIM, g * tq:(g + 1) * tq] = (
                qt_ref[0, h * HEAD_DIM:(h + 1) * HEAD_DIM, :])


def _store_heads_out(o_ref, per_head, tq):
    out_t = jnp.concatenate(
        [per_head[kvh][:, g * tq:(g + 1) * tq] for kvh in range(KV_HEADS) for g in range(GROUP)], axis=0)
    o_ref[0] = out_t.T


def _att_specs(tiles, tq):
    return [pl.BlockSpec((1, ATT_WIDTH, tq), lambda i, j: (i * tiles + j, 0, 0)),
            pl.BlockSpec((tiles, tq, KV_WIDTH), lambda i, j: (i, 0, 0)),
            pl.BlockSpec((tiles, KV_WIDTH, tq), lambda i, j: (i, 0, 0))]


def _suffix_matrix_t(tk):
    r = np.arange(tk)
    later = (r[None, :] > r[:, None]).astype(np.float32)
    return jnp.asarray(np.concatenate([later, np.ones((BF16_ROWS, tk), np.float32)], axis=0), dtype=BF16)


def _sb_prompt_kernel(qt_ref, k_ref, vt_ref, u_ref, o_ref, qz_ref, acc_ref, carry_ref):
    i = pl.program_id(1)
    tk = k_ref.shape[1]
    q2 = qz_ref.shape[2]
    tq = q2 // GROUP
    _load_head_queries(qt_ref, qz_ref)
    acc_ref[...] = jnp.zeros_like(acc_ref)
    carry_ref[...] = jnp.zeros_like(carry_ref)
    key = lax.broadcasted_iota(jnp.int32, (tk, q2), 0)
    qloc = lax.broadcasted_iota(jnp.int32, (tk, q2), 1) & (tq - 1)
    diag_valid = key < qloc
    u2 = u_ref[...]

    def block(j, valid):
        kb = k_ref[j]
        heads = range(KV_HEADS)
        z = [_mm(kb, qz_ref[kvh]) for kvh in heads]
        drop = [_softplus(z[kvh]) for kvh in heads]
        log_take = [z[kvh] - drop[kvh] for kvh in heads]
        if valid is not None:
            drop = [jnp.where(valid, d, 0.0) for d in drop]
        cs = [_mm(u2, d.astype(BF16)) for d in drop]
        w = [jnp.exp(log_take[kvh] - (cs[kvh][0:tk] + carry_ref[kvh, 0:1, :])) for kvh in heads]
        if valid is not None:
            w = [jnp.where(valid, x, 0.0) for x in w]
        for kvh in heads:
            acc_ref[kvh] += _mm(vt_ref[j, kvh * HEAD_DIM:(kvh + 1) * HEAD_DIM, :], w[kvh].astype(BF16))
            carry_ref[kvh] += cs[kvh][tk:tk + 8]

    block(i, diag_valid)

    def body(t, c):
        block(i - 1 - t, None)
        return c

    lax.fori_loop(0, i, body, 0)
    _store_heads_out(o_ref, [acc_ref[kvh] for kvh in range(KV_HEADS)], tq)


def _sb_prompt(qt, k3, vt3, b, tiles):
    tq = k3.shape[1]
    q2 = GROUP * tq
    return pl.pallas_call(
        _sb_prompt_kernel,
        grid=(b, tiles),
        in_specs=_att_specs(tiles, tq) + [pl.BlockSpec((tq + BF16_ROWS, tq), lambda i, j: (0, 0))],
        out_specs=pl.BlockSpec((1, tq, ATT_WIDTH), lambda i, j: (i, j, 0)),
        out_shape=jax.ShapeDtypeStruct((b, tiles * tq, ATT_WIDTH), F32),
        scratch_shapes=[pltpu.VMEM((KV_HEADS, KV_WIDTH, q2), BF16),
                        pltpu.VMEM((KV_HEADS, HEAD_DIM, q2), F32),
                        pltpu.VMEM((KV_HEADS, 8, q2), F32)],
        compiler_params=_cparams(2),
        name="sb_prompt",
    )(qt, k3, vt3, _suffix_matrix_t(tq))


def _page_specs(layer, npages, page, seq, nseq):
    def spec(p):
        return pl.BlockSpec((None, None, KV_WIDTH, page), lambda b, pt: (layer, pt[b * nseq + seq, p], 0, 0))
    return [spec(p) for p in range(npages)]


def _sample_refs(refs, npages, nseq):
    kp = [refs[s * npages:(s + 1) * npages] for s in range(nseq)]
    vp = [refs[(nseq + s) * npages:(nseq + s + 1) * npages] for s in range(nseq)]
    o_ref, padk_ref, padv_ref = refs[2 * nseq * npages:]
    return kp, vp, o_ref, padk_ref, padv_ref


def _load_new_rows(kn_ref, vn_ref, padk_ref, padv_ref, nseq):
    n_new = kn_ref.shape[2]

    @pl.when(pl.program_id(0) == 0)
    def _():
        padk_ref[...] = jnp.zeros_like(padk_ref)
        padv_ref[...] = jnp.zeros_like(padv_ref)

    for s in range(nseq):
        padk_ref[s, 0:n_new, :] = kn_ref[0, s]
        padv_ref[s, 0:n_new, :] = vn_ref[0, s]


def _sample_queries(q8):
    rows = KV_HEADS * q8.shape[0]
    q = jnp.concatenate([q8] * KV_HEADS, axis=0)
    row = lax.broadcasted_iota(jnp.int32, (rows, KV_WIDTH), 0)
    lane = lax.broadcasted_iota(jnp.int32, (rows, KV_WIDTH), 1)
    return jnp.where((row >> 3) == (lane >> 6), q, 0.0)


def _sample_heads_out(acc):
    n = acc.shape[0] // KV_HEADS
    lane_head = lax.broadcasted_iota(jnp.int32, (n, KV_WIDTH), 1) >> 6
    out = jnp.zeros((n, KV_WIDTH), F32)
    for kvh in range(KV_HEADS):
        out = jnp.where(lane_head == kvh, acc[kvh * n:(kvh + 1) * n, :], out)
    return out


def _sample_call(body, name, q4, k_new, v_new, consts, cache_k, cache_v, page_table, layer):
    nsteps, nseq, rows, _ = q4.shape
    n_new = k_new.shape[2]
    npages = page_table.shape[1]
    page = cache_k.shape[3]
    pages = [spec for s in range(nseq) for spec in _page_specs(layer, npages, page, s, nseq)]
    grid_spec = pltpu.PrefetchScalarGridSpec(
        num_scalar_prefetch=1,
        grid=(nsteps,),
        in_specs=[pl.BlockSpec((1, nseq, rows, KV_WIDTH), lambda b, pt: (b, 0, 0, 0)),
                  pl.BlockSpec((1, nseq, n_new, KV_WIDTH), lambda b, pt: (b, 0, 0, 0)),
                  pl.BlockSpec((1, nseq, n_new, KV_WIDTH), lambda b, pt: (b, 0, 0, 0))]
                 + [pl.BlockSpec(c.shape, lambda b, pt: (0, 0)) for c in consts] + pages + pages,
        out_specs=pl.BlockSpec((1, nseq, rows, KV_WIDTH), lambda b, pt: (b, 0, 0, 0)),
        scratch_shapes=[pltpu.VMEM((nseq, page, KV_WIDTH), F32), pltpu.VMEM((nseq, page, KV_WIDTH), F32)])
    return pl.pallas_call(
        functools.partial(body, npages=npages, nseq=nseq),
        grid_spec=grid_spec,
        out_shape=jax.ShapeDtypeStruct(q4.shape, F32),
        compiler_params=_cparams(1),
        name=name,
    )(page_table, q4, k_new, v_new, *consts, *([cache_k] * (nseq * npages)), *([cache_v] * (nseq * npages)))


def _sb_sample_kernel(pt_ref, q_ref, kn_ref, vn_ref, u_ref, *refs, npages, nseq):
    del pt_ref
    kp, vp, o_ref, padk_ref, padv_ref = _sample_refs(refs, npages, nseq)
    _load_new_rows(kn_ref, vn_ref, padk_ref, padv_ref, nseq)
    n_new = kn_ref.shape[2]
    page = u_ref.shape[0]
    rows = KV_HEADS * q_ref.shape[2]
    row = lax.broadcasted_iota(jnp.int32, (rows, page), 0)
    col = lax.broadcasted_iota(jnp.int32, (rows, page), 1)
    valid_new = col < (row & (n_new - 1))
    work = [(s, blk) for s in range(nseq) for blk in range(npages + 1)]
    qz = [_sample_queries(q_ref[0, s]).astype(BF16) for s in range(nseq)]
    z = [_mm(qz[s], kp[s][blk][...].astype(BF16)) if blk < npages else _mm_nt(qz[s], padk_ref[s].astype(BF16))
         for s, blk in work]
    drop = [_softplus(x) for x in z]
    take = [x - d for x, d in zip(z, drop)]
    drop = [jnp.where(valid_new, d, 0.0) if blk == npages else d for d, (s, blk) in zip(drop, work)]
    parts = []
    for d in drop:
        parts.extend(_split2(d))
    cs = _mm(jnp.concatenate(parts, axis=0), u_ref[...])
    acc = [jnp.zeros((rows, KV_WIDTH), F32) for _ in range(nseq)]
    carry = [jnp.zeros((rows, page), F32) for _ in range(nseq)]
    for blk in reversed(range(npages + 1)):
        for s in range(nseq):
            i = work.index((s, blk))
            c = cs[i * 2 * rows:(i + 1) * 2 * rows]
            both = c[0:rows] + c[rows:2 * rows]
            w = jnp.exp(take[i] - (both[:, 0:page] + carry[s]))
            if blk == npages:
                w = jnp.where(valid_new, w, 0.0)
                acc[s] = acc[s] + _mm(w.astype(BF16), padv_ref[s].astype(BF16))
            else:
                acc[s] = acc[s] + _mm_nt(w.astype(BF16), vp[s][blk][...].astype(BF16))
            carry[s] = carry[s] + both[:, page:]
    for s in range(nseq):
        o_ref[0, s] = _sample_heads_out(acc[s])


def _suffix_matrix(tk):
    r = np.arange(tk)
    strict = (r[:, None] > r[None, :]).astype(np.float32)
    return jnp.asarray(np.concatenate([strict, np.ones((tk, tk), np.float32)], axis=1), dtype=BF16)


def _topk_rows(scores, n_valid, nb):
    blk = lax.broadcasted_iota(jnp.int32, scores.shape, 0)
    valid = blk < n_valid
    s = jnp.where(valid, scores, NEG_INF)
    rank = jnp.zeros(scores.shape, jnp.int32)
    for m in range(nb):
        other = s[m:m + 1, :]
        beats = (other > s) | ((other == s) & (blk > m))
        rank = rank + jnp.where(beats, 1, 0)
    return jnp.where(valid & (rank < MOBA_TOPK), 1.0, 0.0)


def _topk_lanes(scores, n_valid, nb):
    blk = lax.broadcasted_iota(jnp.int32, scores.shape, 1)
    valid = blk < n_valid
    s = jnp.where(valid, scores, NEG_INF)
    rank = jnp.zeros(scores.shape, jnp.int32)
    for m in range(nb):
        other = s[:, m:m + 1]
        beats = (other > s) | ((other == s) & (blk > m))
        rank = rank + jnp.where(beats, 1, 0)
    return jnp.where(valid & (rank < MOBA_TOPK), 1.0, 0.0)


def _moba_prompt_kernel(qtf_ref, qt_ref, k_ref, vt_ref, km_ref, o_ref,
                        qz_ref, sel_ref, m_ref, l_ref, acc_ref, *, nb):
    qblk = pl.program_id(1)
    tk = k_ref.shape[1]
    q2 = qz_ref.shape[2]
    tq = q2 // GROUP
    _load_head_queries(qt_ref, qz_ref)
    heads = range(KV_HEADS)
    qf = jnp.concatenate(
        [jnp.concatenate([qtf_ref[0, (kvh * GROUP + g) * HEAD_DIM:(kvh * GROUP + g + 1) * HEAD_DIM, :]
                          for kvh in heads], axis=0) for g in range(GROUP)], axis=1)
    q_hi, q_lo = _split2(qf)
    km = km_ref[0]
    lane_head = lax.broadcasted_iota(jnp.int32, km.shape, 1) >> 6
    km_parts = [_split2(jnp.where(lane_head == kvh, km, 0.0)) for kvh in heads]
    scores = [_mm(km_hi, q_hi) + (_mm(km_hi, q_lo) + _mm(km_lo, q_hi)) for km_hi, km_lo in km_parts]
    for kvh in heads:
        sel_ref[kvh] = _topk_rows(scores[kvh], qblk, nb)
    key = lax.broadcasted_iota(jnp.int32, (tk, q2), 0)
    qloc = lax.broadcasted_iota(jnp.int32, (tk, q2), 1) & (tq - 1)
    causal = key <= qloc
    kb = k_ref[qblk]
    st = [jnp.where(causal, _mm(kb, qz_ref[kvh]), NEG_INF) for kvh in heads]
    m0 = [jnp.max(s, axis=0, keepdims=True) for s in st]
    p = [jnp.exp(st[kvh] - m0[kvh]) for kvh in heads]
    for kvh in heads:
        m_ref[kvh] = m0[kvh]
        l_ref[kvh] = jnp.sum(p[kvh], axis=0, keepdims=True)
        acc_ref[kvh] = _mm(vt_ref[qblk, kvh * HEAD_DIM:(kvh + 1) * HEAD_DIM, :], p[kvh].astype(BF16))

    def body(n, c):
        kb = k_ref[n]
        bias = [jnp.where(sel_ref[kvh, pl.ds(n, 1), :] > 0.5, 0.0, NEG_INF) for kvh in heads]
        st = [_mm(kb, qz_ref[kvh]) + bias[kvh] for kvh in heads]
        m_old = [m_ref[kvh] for kvh in heads]
        m_new = [jnp.maximum(m_old[kvh], jnp.max(st[kvh], axis=0, keepdims=True)) for kvh in heads]
        p = [jnp.exp(st[kvh] - m_new[kvh]) for kvh in heads]
        alpha = [jnp.exp(m_old[kvh] - m_new[kvh]) for kvh in heads]
        pv = [_mm(vt_ref[n, kvh * HEAD_DIM:(kvh + 1) * HEAD_DIM, :], p[kvh].astype(BF16)) for kvh in heads]
        for kvh in heads:
            l_ref[kvh] = alpha[kvh] * l_ref[kvh] + jnp.sum(p[kvh], axis=0, keepdims=True)
            acc_ref[kvh] = alpha[kvh] * acc_ref[kvh] + pv[kvh]
            m_ref[kvh] = m_new[kvh]
        return c

    lax.fori_loop(0, qblk, body, 0)
    _store_heads_out(o_ref, [acc_ref[kvh] / l_ref[kvh] for kvh in heads], tq)


def _moba_prompt(qtf, qt, k3, vt3, kmean, b, tiles):
    tq = k3.shape[1]
    q2 = GROUP * tq
    return pl.pallas_call(
        functools.partial(_moba_prompt_kernel, nb=tiles),
        grid=(b, tiles),
        in_specs=[pl.BlockSpec((1, ATT_WIDTH, tq), lambda i, j: (i * tiles + j, 0, 0))] + _att_specs(tiles, tq)
                 + [pl.BlockSpec((1, tiles, KV_WIDTH), lambda i, j: (i, 0, 0))],
        out_specs=pl.BlockSpec((1, tq, ATT_WIDTH), lambda i, j: (i, j, 0)),
        out_shape=jax.ShapeDtypeStruct((b, tiles * tq, ATT_WIDTH), F32),
        scratch_shapes=[pltpu.VMEM((KV_HEADS, KV_WIDTH, q2), BF16), pltpu.VMEM((KV_HEADS, tiles, q2), F32),
                        pltpu.VMEM((KV_HEADS, 1, q2), F32), pltpu.VMEM((KV_HEADS, 1, q2), F32),
                        pltpu.VMEM((KV_HEADS, HEAD_DIM, q2), F32)],
        compiler_params=_cparams(2),
        name="moba_prompt",
    )(qtf, qt, k3, vt3, kmean)


def _moba_sample_kernel(pt_ref, q_ref, kn_ref, vn_ref, *refs, npages, nseq):
    del pt_ref
    kp, vp, o_ref, padk_ref, padv_ref = _sample_refs(refs, npages, nseq)
    _load_new_rows(kn_ref, vn_ref, padk_ref, padv_ref, nseq)
    n_new = kn_ref.shape[2]
    page = padk_ref.shape[1]
    per_blk = MOBA_BLOCK // page
    nb_past = npages // per_blk
    rows = KV_HEADS * q_ref.shape[2]
    seqs = range(nseq)
    qf = [_sample_queries(q_ref[0, s]) for s in seqs]
    qz = [(q * ATT_SCALE).astype(BF16) for q in qf]
    ones = jnp.ones((page, LANES), BF16)
    lane = lax.broadcasted_iota(jnp.int32, (KV_WIDTH, LANES), 1)
    k_parts = [[_split2(kp[s][pg][...]) for pg in range(npages)] for s in seqs]
    k_sums = [[_mm(hi, ones) + _mm(lo, ones) for hi, lo in k_parts[s]] for s in seqs]
    km = []
    for s in seqs:
        km_s = jnp.zeros((KV_WIDTH, LANES), F32)
        for blk in range(nb_past):
            tot = k_sums[s][blk * per_blk]
            for r in range(1, per_blk):
                tot = tot + k_sums[s][blk * per_blk + r]
            km_s = jnp.where(lane == blk, tot * (1.0 / MOBA_BLOCK), km_s)
        km.append(km_s)
    sel = []
    for s in seqs:
        q_hi, q_lo = _split2(qf[s])
        km_hi, km_lo = _split2(km[s])
        scores = _mm(q_hi, km_hi) + (_mm(q_hi, km_lo) + _mm(q_lo, km_hi))
        sel.append(_topk_lanes(scores, nb_past, nb_past))
    row = lax.broadcasted_iota(jnp.int32, (rows, page), 0)
    col = lax.broadcasted_iota(jnp.int32, (rows, page), 1)
    visible = (col <= (row & (n_new - 1))) & (col < n_new)
    logits = [[jnp.where(sel[s][:, pg // per_blk:pg // per_blk + 1] > 0.5,
                         _mm(qz[s], kp[s][pg][...].astype(BF16)), NEG_INF) for pg in range(npages)]
              + [jnp.where(visible, _mm_nt(qz[s], padk_ref[s].astype(BF16)), NEG_INF)] for s in seqs]
    m = []
    for s in seqs:
        m_el = logits[s][0]
        for x in logits[s][1:]:
            m_el = jnp.maximum(m_el, x)
        m.append(jnp.max(m_el, axis=1, keepdims=True))
    acc = [jnp.zeros((rows, KV_WIDTH), F32) for _ in seqs]
    l_el = [jnp.zeros((rows, page), F32) for _ in seqs]
    for pg in range(npages + 1):
        for s in seqs:
            p = jnp.exp(logits[s][pg] - m[s])
            l_el[s] = l_el[s] + p
            if pg < npages:
                acc[s] = acc[s] + _mm_nt(p.astype(BF16), vp[s][pg][...].astype(BF16))
            else:
                acc[s] = acc[s] + _mm(p.astype(BF16), padv_ref[s].astype(BF16))
    for s in seqs:
        o_ref[0, s] = _sample_heads_out(acc[s] / jnp.sum(l_el[s], axis=1, keepdims=True))


def _merge_kernel(x_ref, a_ref, b_ref, c_ref, g_ref, wb_ref, wo_ref, o_ref):
    d = x_ref.shape[1]
    m = g_ref[:, 0:d] * _mm(a_ref[...].astype(BF16), wb_ref[0])
    m = m + g_ref[:, d:2 * d] * _mm(b_ref[...].astype(BF16), wb_ref[1])
    m = m + g_ref[:, 2 * d:3 * d] * _mm(c_ref[...].astype(BF16), wb_ref[2])
    o_ref[...] = x_ref[...] + _mm(m.astype(BF16), wo_ref[...])


def _merge(x, a, b, c, gate, wb, wo, tm, layer):
    n, d = x.shape
    row = lambda i: (i, 0)
    return pl.pallas_call(
        _merge_kernel,
        grid=(n // tm,),
        in_specs=[pl.BlockSpec((tm, d), row), pl.BlockSpec((tm, ATT_WIDTH), row), pl.BlockSpec((tm, ATT_WIDTH), row),
                  pl.BlockSpec((tm, ATT_WIDTH), row), pl.BlockSpec((tm, N_BRANCH * d), row),
                  _layer_spec(wb, layer), _layer_spec(wo, layer)],
        out_specs=pl.BlockSpec((tm, d), row),
        out_shape=jax.ShapeDtypeStruct((n, d), F32),
        compiler_params=_cparams(1, VMEM_LIMIT),
        name="merge",
    )(x, a, b, c, gate, wb, wo)


def _mlp_ple_kernel(x_ref, p_ref, gm_ref, wu_ref, wd_ref, gp_ref, wg_ref, wp_ref, o_ref, *, f_chunk):
    x = x_ref[...]
    ms = jnp.mean(x * x, axis=-1, keepdims=True)
    h = (x * lax.rsqrt(ms + RMS_EPS) * gm_ref[...]).astype(BF16)
    d_ff = wu_ref.shape[1]
    y = x
    for f0 in range(0, d_ff, f_chunk):
        hid = jnp.maximum(_mm(h, wu_ref[:, f0:f0 + f_chunk]), 0.0)
        y = y + _mm((hid * hid).astype(BF16), wd_ref[f0:f0 + f_chunk, :])
    ms2 = jnp.mean(y * y, axis=-1, keepdims=True)
    h2 = (y * lax.rsqrt(ms2 + RMS_EPS) * gp_ref[...]).astype(BF16)
    gate = _sigmoid(_mm(h2, wg_ref[...]))
    o_ref[...] = y + gate * _mm(p_ref[...].astype(BF16), wp_ref[...])


def _mlp_ple(x, p, gm, wu, wd, gp, wg, wp, tm, layer):
    n, d = x.shape
    row = lambda i: (i, 0)
    return pl.pallas_call(
        functools.partial(_mlp_ple_kernel, f_chunk=512),
        grid=(n // tm,),
        in_specs=[pl.BlockSpec((tm, d), row), pl.BlockSpec((None, tm, p.shape[2]), lambda i: (layer, i, 0)),
                  _layer_spec(gm, layer), _layer_spec(wu, layer), _layer_spec(wd, layer),
                  _layer_spec(gp, layer), _layer_spec(wg, layer), _layer_spec(wp, layer)],
        out_specs=pl.BlockSpec((tm, d), row),
        out_shape=jax.ShapeDtypeStruct((n, d), F32),
        compiler_params=_cparams(1, VMEM_LIMIT),
        name="mlp_ple",
    )(x, p, gm, wu, wd, gp, wg, wp)


def _rope_tables(pos):
    half = ROT_DIM // 2
    inv = ROPE_THETA ** (-jnp.arange(half, dtype=F32) / half)
    ang = pos.astype(F32)[:, None] * inv[None, :]
    cos, sin = jnp.cos(ang), jnp.sin(ang)
    t = pos.shape[0]
    c64 = jnp.concatenate([cos, cos, jnp.ones((t, HEAD_DIM - ROT_DIM), F32)], axis=1)
    s64 = jnp.concatenate([-sin, sin, jnp.zeros((t, HEAD_DIM - ROT_DIM), F32)], axis=1)
    reps = LANES // HEAD_DIM
    return jnp.tile(c64, (1, reps)), jnp.tile(s64, (1, reps))


def _head_group_matrix():
    r = np.arange(ATT_WIDTH) // HEAD_DIM
    return jnp.asarray((r[:, None] == r[None, :]).astype(np.float32), dtype=BF16)


def kernel(x_prompt, x_sample, p_prompt, p_sample, cache_sb_k, cache_sb_v, cache_moba_k, cache_moba_v,
           state_pool, page_table, norm_mix, w_in, pool_w, pool_scale, moba_q_norm, moba_k_norm,
           w_branch, w_out, norm_mlp, w_up, w_down, norm_ple, w_ple_gate, w_ple):
    depth = w_in.shape[0]
    bp, seq, d = x_prompt.shape
    bs, t_new, _ = x_sample.shape
    n_p, n_s = bp * seq, bs * t_new
    n_phys, page = cache_sb_k.shape[1], cache_sb_k.shape[2]
    past_len = page_table.shape[1] * page
    tiles = seq // ATT_TILE
    assert seq % ATT_TILE == 0 and past_len % MOBA_BLOCK == 0 and MOBA_BLOCK % page == 0 and page == LANES
    assert t_new & (t_new - 1) == 0 and GROUP * t_new == 8 and past_len >= POOL_HIST >= t_new
    assert past_len // MOBA_BLOCK <= LANES

    w_in_b, w_branch_b = w_in.astype(BF16), w_branch.astype(BF16)
    w_out_b, w_up_b, w_down_b = w_out.astype(BF16), w_up.astype(BF16), w_down.astype(BF16)
    w_pg_b, w_ple_b, pool_w_b = w_ple_gate.astype(BF16), w_ple.astype(BF16), pool_w.astype(BF16)
    vec = lambda a: a[:, None, :]
    norm_mix3, norm_mlp3, norm_ple3, pool_scale3 = vec(norm_mix), vec(norm_mlp), vec(norm_ple), vec(pool_scale)
    qn = vec(jnp.tile(moba_q_norm, (1, N_HEADS)))
    kn = vec(jnp.tile(moba_k_norm, (1, KV_HEADS)))
    gmat = _head_group_matrix()

    tm_p = ATT_TILE
    tm_s = 256 if n_s % 256 == 0 else n_s
    cos_p, sin_p = _rope_tables(jnp.arange(seq))
    cos_s, sin_s = _rope_tables(jnp.tile(past_len + jnp.arange(t_new), bs))
    caches = [c.transpose(0, 1, 3, 4, 2).reshape(depth, n_phys, KV_WIDTH, page)
              for c in (cache_sb_k, cache_sb_v, cache_moba_k, cache_moba_v)]
    pp = p_prompt.reshape(depth, n_p, -1)
    psm = p_sample.reshape(depth, n_s, -1)

    xp = x_prompt.reshape(n_p, d)
    xs = x_sample.reshape(n_s, d)
    kv_stacks = None
    stack_shape = (depth, bp, KV_WIDTH, seq)
    pool_p = []
    outs_s = [[] for _ in range(5)]
    nseq = 2 if bs % 2 == 0 else 1

    def to_q4(q):
        return q.reshape(bs, t_new, KV_HEADS, GROUP, HEAD_DIM).transpose(0, 3, 1, 2, 4).reshape(
            bs // nseq, nseq, GROUP * t_new, KV_WIDTH)

    def from_q4(r):
        return r.reshape(bs, GROUP, t_new, KV_HEADS, HEAD_DIM).transpose(0, 2, 3, 1, 4).reshape(n_s, ATT_WIDTH)

    def new_rows(a):
        return a.reshape(bs // nseq, nseq, t_new, KV_WIDTH)

    sb_consts = [_suffix_matrix(page)]

    for i in range(depth):
        layer_w = (norm_mix3, w_in_b, qn, kn)
        post_w = (norm_mlp3, w_up_b, w_down_b, norm_ple3, w_pg_b, w_ple_b)

        (u, qt_sb, k_sb_b, vt_sb, qt_mf, qt_mb, k_mb_b, vt_mb, kmean, gate, *kv_stacks) = _inproj(
            xp, *layer_w, cos_p, sin_p, gmat, tm_p, i, kv_stacks, stack_shape)
        k3 = lambda a: a.reshape(n_p // ATT_TILE, ATT_TILE, KV_WIDTH)
        u3 = u.reshape(bp, seq, ATT_WIDTH)
        a = _pool_prompt(u3, pool_w_b, pool_scale3, min(512, seq), i).reshape(n_p, ATT_WIDTH)
        b = _sb_prompt(qt_sb, k3(k_sb_b), vt_sb, bp, tiles).reshape(n_p, ATT_WIDTH)
        c = _moba_prompt(qt_mf, qt_mb, k3(k_mb_b), vt_mb, kmean.reshape(bp, tiles, KV_WIDTH), bp, tiles).reshape(
            n_p, ATT_WIDTH)
        xp = _merge(xp, a, b, c, gate, w_branch_b, w_out_b, tm_p, i)
        xp = _mlp_ple(xp, pp, *post_w, tm_p, i)
        pool_p.append(u3[:, seq - POOL_HIST:])

        (u, q_sb, k_sb, v_sb, q_mb, k_mb, v_mb, gate) = _inproj(xs, *layer_w, cos_s, sin_s, gmat, tm_s, i)
        r3 = lambda a: a.reshape(bs, t_new, a.shape[-1])
        st_t = state_pool[i].transpose(1, 0, 2)
        a = _pool_sample(st_t, r3(u).transpose(1, 0, 2), pool_w_b, pool_scale3, past_len, i).transpose(1, 0, 2).reshape(
            n_s, ATT_WIDTH)
        b = from_q4(_sample_call(_sb_sample_kernel, "sb_sample", to_q4(q_sb), new_rows(k_sb), new_rows(v_sb), sb_consts,
                                 caches[0], caches[1], page_table, i))
        c = from_q4(_sample_call(_moba_sample_kernel, "moba_sample", to_q4(q_mb), new_rows(k_mb), new_rows(v_mb), [],
                                 caches[2], caches[3], page_table, i))
        xs = _merge(xs, a, b, c, gate, w_branch_b, w_out_b, tm_s, i)
        xs = _mlp_ple(xs, psm, *post_w, tm_s, i)
        pool_state = jnp.concatenate([state_pool[i][:, t_new:], r3(u)], axis=1)
        for lst, val in zip(outs_s, (k_sb, v_sb, k_mb, v_mb, pool_state)):
            lst.append(val)

    def kv_prompt(stack):
        return stack.reshape(depth, bp, KV_HEADS, HEAD_DIM, seq).transpose(0, 1, 4, 2, 3)

    def kv_sample(lst):
        return jnp.stack(lst).reshape(depth, bs, t_new, KV_HEADS, HEAD_DIM)

    return (xp.reshape(bp, seq, d), xs.reshape(bs, t_new, d),
            kv_prompt(kv_stacks[0]), kv_prompt(kv_stacks[1]), kv_prompt(kv_stacks[2]), kv_prompt(kv_stacks[3]),
            jnp.stack(pool_p),
            kv_sample(outs_s[0]), kv_sample(outs_s[1]), kv_sample(outs_s[2]), kv_sample(outs_s[3]),
            jnp.stack(outs_s[4]))
```

```python
import functools

import jax
import jax.numpy as jnp
import numpy as np
from jax import lax
from jax.experimental import pallas as pl
from jax.experimental.pallas import tpu as pltpu

F32 = jnp.float32
BF16 = jnp.bfloat16

HEAD_DIM = 64
KV_HEADS = 4
GROUP = 2
N_HEADS = KV_HEADS * GROUP
ATT_WIDTH = N_HEADS * HEAD_DIM
KV_WIDTH = KV_HEADS * HEAD_DIM
ATT_SCALE = HEAD_DIM ** -0.5
POOL_WINDOWS = (2, 4, 8, 16)
POOL_GROUP = 128
POOL_HIST = max(POOL_WINDOWS) - 1
ROPE_THETA = 500000.0
ROT_DIM = HEAD_DIM // 4
MOBA_BLOCK = 256
MOBA_TOPK = 3
N_BRANCH = 3
RMS_EPS = 1e-6
NEG_INF = float("-inf")
SB_DROP_DONE = 104.0

LANES = 128
BF16_ROWS = 16
ATT_TILE = MOBA_BLOCK
VMEM_LIMIT = 56 * 1024 * 1024


def _cparams(n_axes, vmem=None):
    return pltpu.CompilerParams(dimension_semantics=("arbitrary",) * n_axes, vmem_limit_bytes=vmem)


def _layer_spec(arr, layer):
    zeros = (0,) * (arr.ndim - 1)
    return pl.BlockSpec((None,) + arr.shape[1:], lambda *_: (layer,) + zeros)


def _mm(a, b):
    return jnp.dot(a, b, preferred_element_type=F32)


def _mm_nt(a, b):
    return lax.dot_general(a, b, (((1,), (1,)), ((), ())), preferred_element_type=F32)


def _split2(x):
    hi = x.astype(BF16)
    lo = (x - hi.astype(F32)).astype(BF16)
    return hi, lo


def _split3(x):
    hi = x.astype(BF16)
    r = x - hi.astype(F32)
    mid = r.astype(BF16)
    lo = (r - mid.astype(F32)).astype(BF16)
    return hi, mid, lo


def _sigmoid(x):
    return 1.0 / (1.0 + jnp.exp(-x))


def _softplus(z):
    return jnp.maximum(z, 0.0) + jnp.log(1.0 + jnp.exp(jnp.minimum(z, -z)))


def _inproj_kernel(x_ref, gn_ref, w_ref, qn_ref, kn_ref, cos_ref, sin_ref, gmat_ref, *refs, prompt, n_alias):
    out_refs = refs[n_alias:]
    x = x_ref[...]
    ms = jnp.mean(x * x, axis=-1, keepdims=True)
    h = (x * lax.rsqrt(ms + RMS_EPS) * gn_ref[...]).astype(BF16)

    def proj(lo, hi):
        return _mm(h, w_ref[:, lo:hi])

    def head_norm_rope(y, g_ref):
        width = y.shape[1]
        gm = gmat_ref[0:width, 0:width]
        p0, p1, p2 = _split3(y * y)
        ms_h = (_mm(p0, gm) + _mm(p1, gm) + _mm(p2, gm)) * (1.0 / HEAD_DIM)
        yn = y * lax.rsqrt(ms_h + RMS_EPS) * g_ref[...]
        reps = width // LANES
        c = jnp.concatenate([cos_ref[...]] * reps, axis=1)
        s = jnp.concatenate([sin_ref[...]] * reps, axis=1)
        lane = lax.broadcasted_iota(jnp.int32, yn.shape, 1)
        first_half = (lane & (HEAD_DIM - 1)) < (ROT_DIM // 2)
        partner = jnp.where(first_half, pltpu.roll(yn, width - ROT_DIM // 2, 1), pltpu.roll(yn, ROT_DIM // 2, 1))
        return yn * c + partner * s

    o = 0
    u = proj(o, o + ATT_WIDTH)
    o += ATT_WIDTH
    q_sb = proj(o, o + ATT_WIDTH) * ATT_SCALE
    o += ATT_WIDTH
    k_sb = proj(o, o + KV_WIDTH)
    o += KV_WIDTH
    v_sb = proj(o, o + KV_WIDTH)
    o += KV_WIDTH
    q_mb = head_norm_rope(proj(o, o + ATT_WIDTH), qn_ref)
    o += ATT_WIDTH
    k_mb = head_norm_rope(proj(o, o + KV_WIDTH), kn_ref)
    o += KV_WIDTH
    v_mb = proj(o, o + KV_WIDTH)
    o += KV_WIDTH
    if prompt:
        (u_ref, qsb_t, ksb_b, vsb_tb, qmb_tf, qmb_t, kmb_b, vmb_tb, kmean_ref, gate_ref,
         ksb_t, vsb_t, kmb_t, vmb_t) = out_refs
        qsb_t[0] = q_sb.T.astype(BF16)
        ksb_b[...] = k_sb.astype(BF16)
        ksb_t[...] = k_sb.T
        v_t = v_sb.T
        vsb_t[...] = v_t
        vsb_tb[0] = v_t.astype(BF16)
        q_mb_t = q_mb.T
        qmb_tf[0] = q_mb_t
        qmb_t[0] = (q_mb_t * ATT_SCALE).astype(BF16)
        kmb_b[...] = k_mb.astype(BF16)
        kmb_t[...] = k_mb.T
        kmean_ref[0] = jnp.sum(k_mb, axis=0, keepdims=True) * (1.0 / k_mb.shape[0])
        v_t = v_mb.T
        vmb_t[...] = v_t
        vmb_tb[0] = v_t.astype(BF16)
    else:
        (u_ref, qsb_ref, ksb_ref, vsb_ref, qmb_ref, kmb_ref, vmb_ref, gate_ref) = out_refs
        qsb_ref[...] = q_sb
        qmb_ref[...] = q_mb
        ksb_ref[...] = k_sb
        vsb_ref[...] = v_sb
        kmb_ref[...] = k_mb
        vmb_ref[...] = v_mb
    u_ref[...] = u
    gate_w = gate_ref.shape[1]
    chunk = 512
    for c0 in range(0, gate_w, chunk):
        gate_ref[:, c0:c0 + chunk] = _sigmoid(proj(o + c0, o + c0 + chunk))


def _inproj(x, gn, w, qn, kn, cos_t, sin_t, gmat, tm, layer, kv_stacks=None, stack_shape=None):
    prompt = stack_shape is not None
    n, d = x.shape
    in_w = w.shape[2]
    gate_w = in_w - (3 * ATT_WIDTH + 4 * KV_WIDTH)
    nper = cos_t.shape[0] // tm
    nt = n // tm
    row = lambda i: (i, 0)
    const = lambda i: (0, 0)

    def flat(width, dtype):
        return jax.ShapeDtypeStruct((n, width), dtype), pl.BlockSpec((tm, width), row)

    def transposed(width, dtype):
        return jax.ShapeDtypeStruct((nt, width, tm), dtype), pl.BlockSpec((1, width, tm), lambda i: (i, 0, 0))

    alias_in, aliases = [], {}
    if prompt:
        tiles = stack_shape[3] // tm
        stacked = (jax.ShapeDtypeStruct(stack_shape, F32),
                   pl.BlockSpec((None, None, KV_WIDTH, tm), lambda i: (layer, i // tiles, 0, i % tiles)))
        outs = [flat(ATT_WIDTH, F32), transposed(ATT_WIDTH, BF16), flat(KV_WIDTH, BF16), transposed(KV_WIDTH, BF16),
                transposed(ATT_WIDTH, F32), transposed(ATT_WIDTH, BF16), flat(KV_WIDTH, BF16),
                transposed(KV_WIDTH, BF16),
                (jax.ShapeDtypeStruct((nt, 1, KV_WIDTH), F32), pl.BlockSpec((1, 1, KV_WIDTH), lambda i: (i, 0, 0))),
                flat(gate_w, F32), stacked, stacked, stacked, stacked]
        if kv_stacks is not None:
            alias_in = list(kv_stacks)
            aliases = {8 + k: len(outs) - 4 + k for k in range(4)}
    else:
        outs = [flat(ATT_WIDTH, F32), flat(ATT_WIDTH, F32), flat(KV_WIDTH, F32), flat(KV_WIDTH, F32),
                flat(ATT_WIDTH, F32), flat(KV_WIDTH, F32), flat(KV_WIDTH, F32), flat(gate_w, F32)]
    return pl.pallas_call(
        functools.partial(_inproj_kernel, prompt=prompt, n_alias=len(alias_in)),
        grid=(nt,),
        in_specs=[pl.BlockSpec((tm, d), row), _layer_spec(gn, layer), _layer_spec(w, layer),
                  _layer_spec(qn, layer), _layer_spec(kn, layer),
                  pl.BlockSpec((tm, LANES), lambda i: (i % nper, 0)), pl.BlockSpec((tm, LANES), lambda i: (i % nper, 0)),
                  pl.BlockSpec((ATT_WIDTH, ATT_WIDTH), const)]
                 + [pl.BlockSpec(memory_space=pl.ANY)] * len(alias_in),
        out_specs=[s for _, s in outs],
        out_shape=[s for s, _ in outs],
        input_output_aliases=aliases,
        compiler_params=_cparams(1, VMEM_LIMIT),
        name="inproj_prompt" if prompt else "inproj_sample",
    )(x, gn, w, qn, kn, cos_t, sin_t, gmat, *alias_in)


def _pool_prompt_kernel(cur_ref, prev_ref, pw_ref, ps_ref, o_ref, ext_ref, *, ts):
    t = pl.program_id(1)
    hist = ext_ref.shape[0] - ts
    ext_ref[0:hist, :] = jnp.where(t == 0, 0.0, prev_ref[0])
    ext_ref[hist:, :] = cur_ref[0]
    pos = t * ts + lax.broadcasted_iota(jnp.int32, (ts, POOL_GROUP), 0)
    for g, w in enumerate(POOL_WINDOWS):
        lo, hi = g * POOL_GROUP, (g + 1) * POOL_GROUP
        ws = ext_ref[hist:hist + ts, lo:hi]
        for k in range(1, w):
            ws = ws + ext_ref[hist - k:hist - k + ts, lo:hi]
        cnt = jnp.minimum(pos + 1, w).astype(F32)
        dlt = ws / cnt - ext_ref[hist:hist + ts, lo:hi]
        o_ref[0, :, lo:hi] = _mm(dlt.astype(BF16), pw_ref[g]) * ps_ref[:, lo:hi]


def _pool_prompt(u, pw, ps, ts, layer):
    b, s, c = u.shape
    hist = 16
    ratio = ts // hist
    return pl.pallas_call(
        functools.partial(_pool_prompt_kernel, ts=ts),
        grid=(b, s // ts),
        in_specs=[pl.BlockSpec((1, ts, c), lambda i, t: (i, t, 0)),
                  pl.BlockSpec((1, hist, c), lambda i, t: (i, jnp.maximum(t * ratio - 1, 0), 0)),
                  _layer_spec(pw, layer), _layer_spec(ps, layer)],
        out_specs=pl.BlockSpec((1, ts, c), lambda i, t: (i, t, 0)),
        out_shape=jax.ShapeDtypeStruct((b, s, c), F32),
        scratch_shapes=[pltpu.VMEM((ts + hist, c), F32)],
        compiler_params=_cparams(2),
        name="pool_prompt",
    )(u, u, pw, ps)


def _pool_sample_kernel(st_ref, u_ref, pw_ref, ps_ref, o_ref, *, start_pos):
    n_new = u_ref.shape[0]

    def row(idx, lo, hi):
        return st_ref[idx, :, lo:hi] if idx < POOL_HIST else u_ref[idx - POOL_HIST, :, lo:hi]

    for j in range(n_new):
        for g, w in enumerate(POOL_WINDOWS):
            lo, hi = g * POOL_GROUP, (g + 1) * POOL_GROUP
            ws = row(POOL_HIST + j, lo, hi)
            for k in range(1, w):
                ws = ws + row(POOL_HIST + j - k, lo, hi)
            cnt = float(min(start_pos + j + 1, w))
            dlt = ws / cnt - row(POOL_HIST + j, lo, hi)
            o_ref[j, :, lo:hi] = _mm(dlt.astype(BF16), pw_ref[g]) * ps_ref[:, lo:hi]


def _pool_sample(st_t, u_t, pw, ps, start_pos, layer):
    t, b, c = u_t.shape
    full = lambda shape: pl.BlockSpec(shape, lambda i: (0,) * len(shape))
    return pl.pallas_call(
        functools.partial(_pool_sample_kernel, start_pos=start_pos),
        grid=(1,),
        in_specs=[full(st_t.shape), full(u_t.shape), _layer_spec(pw, layer), _layer_spec(ps, layer)],
        out_specs=full((t, b, c)),
        out_shape=jax.ShapeDtypeStruct((t, b, c), F32),
        compiler_params=_cparams(1),
        name="pool_sample",
    )(st_t, u_t, pw, ps)


def _load_head_queries(qt_ref, qz_ref):
    tq = qt_ref.shape[2]

    @pl.when((pl.program_id(0) == 0) & (pl.program_id(1) == 0))
    def _():
        qz_ref[...] = jnp.zeros_like(qz_ref)

    for kvh in range(KV_HEADS):
        for g in range(GROUP):
            h = kvh * GROUP + g
            qz_ref[kvh, kvh * HEAD_DIM:(kvh + 1) * HEAD_DIM, g * tq:(g + 1) * tq] = (
                qt_ref[0, h * HEAD_DIM:(h + 1) * HEAD_DIM, :])


def _store_heads_out(o_ref, per_head, tq):
    out_t = jnp.concatenate(
        [per_head[kvh][:, g * tq:(g + 1) * tq] for kvh in range(KV_HEADS) for g in range(GROUP)], axis=0)
    o_ref[0] = out_t.T


def _att_specs(tiles, tq):
    return [pl.BlockSpec((1, ATT_WIDTH, tq), lambda i, j: (i * tiles + j, 0, 0)),
            pl.BlockSpec((tiles, tq, KV_WIDTH), lambda i, j: (i, 0, 0)),
            pl.BlockSpec((tiles, KV_WIDTH, tq), lambda i, j: (i, 0, 0))]


def _suffix_matrix_t(tk):
    r = np.arange(tk)
    later = (r[None, :] > r[:, None]).astype(np.float32)
    return jnp.asarray(np.concatenate([later, np.ones((BF16_ROWS, tk), np.float32)], axis=0), dtype=BF16)


def _sb_prompt_kernel(qt_ref, k_ref, vt_ref, u_ref, o_ref, qz_ref, acc_ref, carry_ref):
    i = pl.program_id(1)
    tk = k_ref.shape[1]
    q2 = qz_ref.shape[2]
    tq = q2 // GROUP
    _load_head_queries(qt_ref, qz_ref)
    acc_ref[...] = jnp.zeros_like(acc_ref)
    carry_ref[...] = jnp.zeros_like(carry_ref)
    key = lax.broadcasted_iota(jnp.int32, (tk, q2), 0)
    qloc = lax.broadcasted_iota(jnp.int32, (tk, q2), 1) & (tq - 1)
    diag_valid = key < qloc
    u2 = u_ref[...]

    def block(j, valid):
        kb = k_ref[j]
        heads = range(KV_HEADS)
        z = [_mm(kb, qz_ref[kvh]) for kvh in heads]
        drop = [_softplus(z[kvh]) for kvh in heads]
        log_take = [z[kvh] - drop[kvh] for kvh in heads]
        if valid is not None:
            drop = [jnp.where(valid, d, 0.0) for d in drop]
        cs = [_mm(u2, d.astype(BF16)) for d in drop]
        w = [jnp.exp(log_take[kvh] - (cs[kvh][0:tk] + carry_ref[kvh, 0:1, :])) for kvh in heads]
        if valid is not None:
            w = [jnp.where(valid, x, 0.0) for x in w]
        for kvh in heads:
            acc_ref[kvh] += _mm(vt_ref[j, kvh * HEAD_DIM:(kvh + 1) * HEAD_DIM, :], w[kvh].astype(BF16))
            carry_ref[kvh] += cs[kvh][tk:tk + 8]

    block(i, diag_valid)

    def all_dropped():
        return (jnp.min(carry_ref[...]) >= SB_DROP_DONE).astype(jnp.int32)

    def cond(state):
        t, done = state
        return jnp.logical_and(t < i, done == 0)

    def body(state):
        t, _ = state
        block(i - 1 - t, None)
        return t + 1, all_dropped()

    lax.while_loop(cond, body, (jnp.int32(0), all_dropped()))
    _store_heads_out(o_ref, [acc_ref[kvh] for kvh in range(KV_HEADS)], tq)


def _sb_prompt(qt, k3, vt3, b, tiles):
    tq = k3.shape[1]
    q2 = GROUP * tq
    return pl.pallas_call(
        _sb_prompt_kernel,
        grid=(b, tiles),
        in_specs=_att_specs(tiles, tq) + [pl.BlockSpec((tq + BF16_ROWS, tq), lambda i, j: (0, 0))],
        out_specs=pl.BlockSpec((1, tq, ATT_WIDTH), lambda i, j: (i, j, 0)),
        out_shape=jax.ShapeDtypeStruct((b, tiles * tq, ATT_WIDTH), F32),
        scratch_shapes=[pltpu.VMEM((KV_HEADS, KV_WIDTH, q2), BF16),
                        pltpu.VMEM((KV_HEADS, HEAD_DIM, q2), F32),
                        pltpu.VMEM((KV_HEADS, 8, q2), F32)],
        compiler_params=_cparams(2),
        name="sb_prompt",
    )(qt, k3, vt3, _suffix_matrix_t(tq))


def _page_specs(layer, npages, page, seq, nseq):
    def spec(p):
        return pl.BlockSpec((None, None, KV_WIDTH, page), lambda b, pt: (layer, pt[b * nseq + seq, p], 0, 0))
    return [spec(p) for p in range(npages)]


def _sample_refs(refs, npages, nseq):
    kp = [refs[s * npages:(s + 1) * npages] for s in range(nseq)]
    vp = [refs[(nseq + s) * npages:(nseq + s + 1) * npages] for s in range(nseq)]
    o_ref, padk_ref, padv_ref = refs[2 * nseq * npages:]
    return kp, vp, o_ref, padk_ref, padv_ref


def _load_new_rows(kn_ref, vn_ref, padk_ref, padv_ref, nseq):
    n_new = kn_ref.shape[2]

    @pl.when(pl.program_id(0) == 0)
    def _():
        padk_ref[...] = jnp.zeros_like(padk_ref)
        padv_ref[...] = jnp.zeros_like(padv_ref)

    for s in range(nseq):
        padk_ref[s, 0:n_new, :] = kn_ref[0, s]
        padv_ref[s, 0:n_new, :] = vn_ref[0, s]


def _sample_queries(q8):
    rows = KV_HEADS * q8.shape[0]
    q = jnp.concatenate([q8] * KV_HEADS, axis=0)
    row = lax.broadcasted_iota(jnp.int32, (rows, KV_WIDTH), 0)
    lane = lax.broadcasted_iota(jnp.int32, (rows, KV_WIDTH), 1)
    return jnp.where((row >> 3) == (lane >> 6), q, 0.0)


def _sample_heads_out(acc):
    n = acc.shape[0] // KV_HEADS
    lane_head = lax.broadcasted_iota(jnp.int32, (n, KV_WIDTH), 1) >> 6
    out = jnp.zeros((n, KV_WIDTH), F32)
    for kvh in range(KV_HEADS):
        out = jnp.where(lane_head == kvh, acc[kvh * n:(kvh + 1) * n, :], out)
    return out


def _sample_call(body, name, q4, k_new, v_new, consts, cache_k, cache_v, page_table, layer):
    nsteps, nseq, rows, _ = q4.shape
    n_new = k_new.shape[2]
    npages = page_table.shape[1]
    page = cache_k.shape[3]
    pages = [spec for s in range(nseq) for spec in _page_specs(layer, npages, page, s, nseq)]
    grid_spec = pltpu.PrefetchScalarGridSpec(
        num_scalar_prefetch=1,
        grid=(nsteps,),
        in_specs=[pl.BlockSpec((1, nseq, rows, KV_WIDTH), lambda b, pt: (b, 0, 0, 0)),
                  pl.BlockSpec((1, nseq, n_new, KV_WIDTH), lambda b, pt: (b, 0, 0, 0)),
                  pl.BlockSpec((1, nseq, n_new, KV_WIDTH), lambda b, pt: (b, 0, 0, 0))]
                 + [pl.BlockSpec(c.shape, lambda b, pt: (0, 0)) for c in consts] + pages + pages,
        out_specs=pl.BlockSpec((1, nseq, rows, KV_WIDTH), lambda b, pt: (b, 0, 0, 0)),
        scratch_shapes=[pltpu.VMEM((nseq, page, KV_WIDTH), F32), pltpu.VMEM((nseq, page, KV_WIDTH), F32)])
    return pl.pallas_call(
        functools.partial(body, npages=npages, nseq=nseq),
        grid_spec=grid_spec,
        out_shape=jax.ShapeDtypeStruct(q4.shape, F32),
        compiler_params=_cparams(1),
        name=name,
    )(page_table, q4, k_new, v_new, *consts, *([cache_k] * (nseq * npages)), *([cache_v] * (nseq * npages)))


def _sb_sample_kernel(pt_ref, q_ref, kn_ref, vn_ref, u_ref, *refs, npages, nseq):
    del pt_ref
    kp, vp, o_ref, padk_ref, padv_ref = _sample_refs(refs, npages, nseq)
    _load_new_rows(kn_ref, vn_ref, padk_ref, padv_ref, nseq)
    n_new = kn_ref.shape[2]
    page = u_ref.shape[0]
    rows = KV_HEADS * q_ref.shape[2]
    row = lax.broadcasted_iota(jnp.int32, (rows, page), 0)
    col = lax.broadcasted_iota(jnp.int32, (rows, page), 1)
    valid_new = col < (row & (n_new - 1))
    work = [(s, blk) for s in range(nseq) for blk in range(npages + 1)]
    qz = [_sample_queries(q_ref[0, s]).astype(BF16) for s in range(nseq)]
    z = [_mm(qz[s], kp[s][blk][...].astype(BF16)) if blk < npages else _mm_nt(qz[s], padk_ref[s].astype(BF16))
         for s, blk in work]
    drop = [_softplus(x) for x in z]
    take = [x - d for x, d in zip(z, drop)]
    drop = [jnp.where(valid_new, d, 0.0) if blk == npages else d for d, (s, blk) in zip(drop, work)]
    parts = []
    for d in drop:
        parts.extend(_split2(d))
    cs = _mm(jnp.concatenate(parts, axis=0), u_ref[...])
    acc = [jnp.zeros((rows, KV_WIDTH), F32) for _ in range(nseq)]
    carry = [jnp.zeros((rows, page), F32) for _ in range(nseq)]
    for blk in reversed(range(npages + 1)):
        for s in range(nseq):
            i = work.index((s, blk))
            c = cs[i * 2 * rows:(i + 1) * 2 * rows]
            both = c[0:rows] + c[rows:2 * rows]
            w = jnp.exp(take[i] - (both[:, 0:page] + carry[s]))
            if blk == npages:
                w = jnp.where(valid_new, w, 0.0)
                acc[s] = acc[s] + _mm(w.astype(BF16), padv_ref[s].astype(BF16))
            else:
                acc[s] = acc[s] + _mm_nt(w.astype(BF16), vp[s][blk][...].astype(BF16))
            carry[s] = carry[s] + both[:, page:]
    for s in range(nseq):
        o_ref[0, s] = _sample_heads_out(acc[s])


def _suffix_matrix(tk):
    r = np.arange(tk)
    strict = (r[:, None] > r[None, :]).astype(np.float32)
    return jnp.asarray(np.concatenate([strict, np.ones((tk, tk), np.float32)], axis=1), dtype=BF16)


def _topk_rows(scores, n_valid, nb):
    blk = lax.broadcasted_iota(jnp.int32, scores.shape, 0)
    valid = blk < n_valid
    s = jnp.where(valid, scores, NEG_INF)
    rank = jnp.zeros(scores.shape, jnp.int32)
    for m in range(nb):
        other = s[m:m + 1, :]
        beats = (other > s) | ((other == s) & (blk > m))
        rank = rank + jnp.where(beats, 1, 0)
    return jnp.where(valid & (rank < MOBA_TOPK), 1.0, 0.0)


def _topk_lanes(scores, n_valid, nb):
    blk = lax.broadcasted_iota(jnp.int32, scores.shape, 1)
    valid = blk < n_valid
    s = jnp.where(valid, scores, NEG_INF)
    rank = jnp.zeros(scores.shape, jnp.int32)
    for m in range(nb):
        other = s[:, m:m + 1]
        beats = (other > s) | ((other == s) & (blk > m))
        rank = rank + jnp.where(beats, 1, 0)
    return jnp.where(valid & (rank < MOBA_TOPK), 1.0, 0.0)


def _moba_prompt_kernel(qtf_ref, qt_ref, k_ref, vt_ref, km_ref, o_ref,
                        qz_ref, sel_ref, m_ref, l_ref, acc_ref, *, nb):
    qblk = pl.program_id(1)
    tk = k_ref.shape[1]
    q2 = qz_ref.shape[2]
    tq = q2 // GROUP
    _load_head_queries(qt_ref, qz_ref)
    heads = range(KV_HEADS)
    qf = jnp.concatenate(
        [jnp.concatenate([qtf_ref[0, (kvh * GROUP + g) * HEAD_DIM:(kvh * GROUP + g + 1) * HEAD_DIM, :]
                          for kvh in heads], axis=0) for g in range(GROUP)], axis=1)
    q_hi, q_lo = _split2(qf)
    km = km_ref[0]
    lane_head = lax.broadcasted_iota(jnp.int32, km.shape, 1) >> 6
    km_parts = [_split2(jnp.where(lane_head == kvh, km, 0.0)) for kvh in heads]
    scores = [_mm(km_hi, q_hi) + (_mm(km_hi, q_lo) + _mm(km_lo, q_hi)) for km_hi, km_lo in km_parts]
    for kvh in heads:
        sel_ref[kvh] = _topk_rows(scores[kvh], qblk, nb)
    key = lax.broadcasted_iota(jnp.int32, (tk, q2), 0)
    qloc = lax.broadcasted_iota(jnp.int32, (tk, q2), 1) & (tq - 1)
    causal = key <= qloc
    kb = k_ref[qblk]
    st = [jnp.where(causal, _mm(kb, qz_ref[kvh]), NEG_INF) for kvh in heads]
    m0 = [jnp.max(s, axis=0, keepdims=True) for s in st]
    p = [jnp.exp(st[kvh] - m0[kvh]) for kvh in heads]
    for kvh in heads:
        m_ref[kvh] = m0[kvh]
        l_ref[kvh] = jnp.sum(p[kvh], axis=0, keepdims=True)
        acc_ref[kvh] = _mm(vt_ref[qblk, kvh * HEAD_DIM:(kvh + 1) * HEAD_DIM, :], p[kvh].astype(BF16))

    def body(n, c):
        kb = k_ref[n]
        bias = [jnp.where(sel_ref[kvh, pl.ds(n, 1), :] > 0.5, 0.0, NEG_INF) for kvh in heads]
        st = [_mm(kb, qz_ref[kvh]) for kvh in heads]
        m_old = [m_ref[kvh] for kvh in heads]
        m_new = [jnp.maximum(m_old[kvh], jnp.max(st[kvh], axis=0, keepdims=True) + bias[kvh]) for kvh in heads]
        p = [jnp.exp(st[kvh] - (m_new[kvh] - bias[kvh])) for kvh in heads]
        alpha = [jnp.exp(m_old[kvh] - m_new[kvh]) for kvh in heads]
        pv = [_mm(vt_ref[n, kvh * HEAD_DIM:(kvh + 1) * HEAD_DIM, :], p[kvh].astype(BF16)) for kvh in heads]
        for kvh in heads:
            l_ref[kvh] = alpha[kvh] * l_ref[kvh] + jnp.sum(p[kvh], axis=0, keepdims=True)
            acc_ref[kvh] = alpha[kvh] * acc_ref[kvh] + pv[kvh]
            m_ref[kvh] = m_new[kvh]
        return c

    lax.fori_loop(0, qblk, body, 0)
    _store_heads_out(o_ref, [acc_ref[kvh] / l_ref[kvh] for kvh in heads], tq)


def _moba_prompt(qtf, qt, k3, vt3, kmean, b, tiles):
    tq = k3.shape[1]
    q2 = GROUP * tq
    return pl.pallas_call(
        functools.partial(_moba_prompt_kernel, nb=tiles),
        grid=(b, tiles),
        in_specs=[pl.BlockSpec((1, ATT_WIDTH, tq), lambda i, j: (i * tiles + j, 0, 0))] + _att_specs(tiles, tq)
                 + [pl.BlockSpec((1, tiles, KV_WIDTH), lambda i, j: (i, 0, 0))],
        out_specs=pl.BlockSpec((1, tq, ATT_WIDTH), lambda i, j: (i, j, 0)),
        out_shape=jax.ShapeDtypeStruct((b, tiles * tq, ATT_WIDTH), F32),
        scratch_shapes=[pltpu.VMEM((KV_HEADS, KV_WIDTH, q2), BF16), pltpu.VMEM((KV_HEADS, tiles, q2), F32),
                        pltpu.VMEM((KV_HEADS, 1, q2), F32), pltpu.VMEM((KV_HEADS, 1, q2), F32),
                        pltpu.VMEM((KV_HEADS, HEAD_DIM, q2), F32)],
        compiler_params=_cparams(2),
        name="moba_prompt",
    )(qtf, qt, k3, vt3, kmean)


def _moba_sample_kernel(pt_ref, q_ref, kn_ref, vn_ref, *refs, npages, nseq):
    del pt_ref
    kp, vp, o_ref, padk_ref, padv_ref = _sample_refs(refs, npages, nseq)
    _load_new_rows(kn_ref, vn_ref, padk_ref, padv_ref, nseq)
    n_new = kn_ref.shape[2]
    page = padk_ref.shape[1]
    per_blk = MOBA_BLOCK // page
    nb_past = npages // per_blk
    rows = KV_HEADS * q_ref.shape[2]
    seqs = range(nseq)
    qf = [_sample_queries(q_ref[0, s]) for s in seqs]
    qz = [(q * ATT_SCALE).astype(BF16) for q in qf]
    ones = jnp.ones((page, LANES), BF16)
    lane = lax.broadcasted_iota(jnp.int32, (KV_WIDTH, LANES), 1)
    k_parts = [[_split2(kp[s][pg][...]) for pg in range(npages)] for s in seqs]
    k_sums = [[_mm(hi, ones) + _mm(lo, ones) for hi, lo in k_parts[s]] for s in seqs]
    km = []
    for s in seqs:
        km_s = jnp.zeros((KV_WIDTH, LANES), F32)
        for blk in range(nb_past):
            tot = k_sums[s][blk * per_blk]
            for r in range(1, per_blk):
                tot = tot + k_sums[s][blk * per_blk + r]
            km_s = jnp.where(lane == blk, tot * (1.0 / MOBA_BLOCK), km_s)
        km.append(km_s)
    sel = []
    for s in seqs:
        q_hi, q_lo = _split2(qf[s])
        km_hi, km_lo = _split2(km[s])
        scores = _mm(q_hi, km_hi) + (_mm(q_hi, km_lo) + _mm(q_lo, km_hi))
        sel.append(_topk_lanes(scores, nb_past, nb_past))
    row = lax.broadcasted_iota(jnp.int32, (rows, page), 0)
    col = lax.broadcasted_iota(jnp.int32, (rows, page), 1)
    visible = (col <= (row & (n_new - 1))) & (col < n_new)
    logits = [[jnp.where(sel[s][:, pg // per_blk:pg // per_blk + 1] > 0.5,
                         _mm(qz[s], kp[s][pg][...].astype(BF16)), NEG_INF) for pg in range(npages)]
              + [jnp.where(visible, _mm_nt(qz[s], padk_ref[s].astype(BF16)), NEG_INF)] for s in seqs]
    m = []
    for s in seqs:
        m_el = logits[s][0]
        for x in logits[s][1:]:
            m_el = jnp.maximum(m_el, x)
        m.append(jnp.max(m_el, axis=1, keepdims=True))
    acc = [jnp.zeros((rows, KV_WIDTH), F32) for _ in seqs]
    l_el = [jnp.zeros((rows, page), F32) for _ in seqs]
    for pg in range(npages + 1):
        for s in seqs:
            p = jnp.exp(logits[s][pg] - m[s])
            l_el[s] = l_el[s] + p
            if pg < npages:
                acc[s] = acc[s] + _mm_nt(p.astype(BF16), vp[s][pg][...].astype(BF16))
            else:
                acc[s] = acc[s] + _mm(p.astype(BF16), padv_ref[s].astype(BF16))
    for s in seqs:
        o_ref[0, s] = _sample_heads_out(acc[s] / jnp.sum(l_el[s], axis=1, keepdims=True))


def _merge_kernel(x_ref, a_ref, b_ref, c_ref, g_ref, wb_ref, wo_ref, o_ref):
    d = x_ref.shape[1]
    m = g_ref[:, 0:d] * _mm(a_ref[...].astype(BF16), wb_ref[0])
    m = m + g_ref[:, d:2 * d] * _mm(b_ref[...].astype(BF16), wb_ref[1])
    m = m + g_ref[:, 2 * d:3 * d] * _mm(c_ref[...].astype(BF16), wb_ref[2])
    o_ref[...] = x_ref[...] + _mm(m.astype(BF16), wo_ref[...])


def _merge(x, a, b, c, gate, wb, wo, tm, layer):
    n, d = x.shape
    row = lambda i: (i, 0)
    return pl.pallas_call(
        _merge_kernel,
        grid=(n // tm,),
        in_specs=[pl.BlockSpec((tm, d), row), pl.BlockSpec((tm, ATT_WIDTH), row), pl.BlockSpec((tm, ATT_WIDTH), row),
                  pl.BlockSpec((tm, ATT_WIDTH), row), pl.BlockSpec((tm, N_BRANCH * d), row),
                  _layer_spec(wb, layer), _layer_spec(wo, layer)],
        out_specs=pl.BlockSpec((tm, d), row),
        out_shape=jax.ShapeDtypeStruct((n, d), F32),
        compiler_params=_cparams(1, VMEM_LIMIT),
        name="merge",
    )(x, a, b, c, gate, wb, wo)


def _mlp_ple_kernel(x_ref, p_ref, gm_ref, wu_ref, wd_ref, gp_ref, wg_ref, wp_ref, o_ref, *, f_chunk):
    x = x_ref[...]
    ms = jnp.mean(x * x, axis=-1, keepdims=True)
    h = (x * lax.rsqrt(ms + RMS_EPS) * gm_ref[...]).astype(BF16)
    d_ff = wu_ref.shape[1]
    y = x
    for f0 in range(0, d_ff, f_chunk):
        hid = jnp.maximum(_mm(h, wu_ref[:, f0:f0 + f_chunk]), 0.0)
        y = y + _mm((hid * hid).astype(BF16), wd_ref[f0:f0 + f_chunk, :])
    ms2 = jnp.mean(y * y, axis=-1, keepdims=True)
    h2 = (y * lax.rsqrt(ms2 + RMS_EPS) * gp_ref[...]).astype(BF16)
    gate = _sigmoid(_mm(h2, wg_ref[...]))
    o_ref[...] = y + gate * _mm(p_ref[...].astype(BF16), wp_ref[...])


def _mlp_ple(x, p, gm, wu, wd, gp, wg, wp, tm, layer):
    n, d = x.shape
    row = lambda i: (i, 0)
    return pl.pallas_call(
        functools.partial(_mlp_ple_kernel, f_chunk=512),
        grid=(n // tm,),
        in_specs=[pl.BlockSpec((tm, d), row), pl.BlockSpec((None, tm, p.shape[2]), lambda i: (layer, i, 0)),
                  _layer_spec(gm, layer), _layer_spec(wu, layer), _layer_spec(wd, layer),
                  _layer_spec(gp, layer), _layer_spec(wg, layer), _layer_spec(wp, layer)],
        out_specs=pl.BlockSpec((tm, d), row),
        out_shape=jax.ShapeDtypeStruct((n, d), F32),
        compiler_params=_cparams(1, VMEM_LIMIT),
        name="mlp_ple",
    )(x, p, gm, wu, wd, gp, wg, wp)


def _rope_tables(pos):
    half = ROT_DIM // 2
    inv = ROPE_THETA ** (-jnp.arange(half, dtype=F32) / half)
    ang = pos.astype(F32)[:, None] * inv[None, :]
    cos, sin = jnp.cos(ang), jnp.sin(ang)
    t = pos.shape[0]
    c64 = jnp.concatenate([cos, cos, jnp.ones((t, HEAD_DIM - ROT_DIM), F32)], axis=1)
    s64 = jnp.concatenate([-sin, sin, jnp.zeros((t, HEAD_DIM - ROT_DIM), F32)], axis=1)
    reps = LANES // HEAD_DIM
    return jnp.tile(c64, (1, reps)), jnp.tile(s64, (1, reps))


def _head_group_matrix():
    r = np.arange(ATT_WIDTH) // HEAD_DIM
    return jnp.asarray((r[:, None] == r[None, :]).astype(np.float32), dtype=BF16)


def kernel(x_prompt, x_sample, p_prompt, p_sample, cache_sb_k, cache_sb_v, cache_moba_k, cache_moba_v,
           state_pool, page_table, norm_mix, w_in, pool_w, pool_scale, moba_q_norm, moba_k_norm,
           w_branch, w_out, norm_mlp, w_up, w_down, norm_ple, w_ple_gate, w_ple):
    depth = w_in.shape[0]
    bp, seq, d = x_prompt.shape
    bs, t_new, _ = x_sample.shape
    n_p, n_s = bp * seq, bs * t_new
    n_phys, page = cache_sb_k.shape[1], cache_sb_k.shape[2]
    past_len = page_table.shape[1] * page
    tiles = seq // ATT_TILE
    assert seq % ATT_TILE == 0 and past_len % MOBA_BLOCK == 0 and MOBA_BLOCK % page == 0 and page == LANES
    assert t_new & (t_new - 1) == 0 and GROUP * t_new == 8 and past_len >= POOL_HIST >= t_new
    assert past_len // MOBA_BLOCK <= LANES

    w_in_b, w_branch_b = w_in.astype(BF16), w_branch.astype(BF16)
    w_out_b, w_up_b, w_down_b = w_out.astype(BF16), w_up.astype(BF16), w_down.astype(BF16)
    w_pg_b, w_ple_b, pool_w_b = w_ple_gate.astype(BF16), w_ple.astype(BF16), pool_w.astype(BF16)
    vec = lambda a: a[:, None, :]
    norm_mix3, norm_mlp3, norm_ple3, pool_scale3 = vec(norm_mix), vec(norm_mlp), vec(norm_ple), vec(pool_scale)
    qn = vec(jnp.tile(moba_q_norm, (1, N_HEADS)))
    kn = vec(jnp.tile(moba_k_norm, (1, KV_HEADS)))
    gmat = _head_group_matrix()

    tm_p = ATT_TILE
    tm_s = 256 if n_s % 256 == 0 else n_s
    cos_p, sin_p = _rope_tables(jnp.arange(seq))
    cos_s, sin_s = _rope_tables(jnp.tile(past_len + jnp.arange(t_new), bs))
    caches = [c.transpose(0, 1, 3, 4, 2).reshape(depth, n_phys, KV_WIDTH, page)
              for c in (cache_sb_k, cache_sb_v, cache_moba_k, cache_moba_v)]
    pp = p_prompt.reshape(depth, n_p, -1)
    psm = p_sample.reshape(depth, n_s, -1)

    xp = x_prompt.reshape(n_p, d)
    xs = x_sample.reshape(n_s, d)
    kv_stacks = None
    stack_shape = (depth, bp, KV_WIDTH, seq)
    pool_p = []
    outs_s = [[] for _ in range(5)]
    nseq = 2 if bs % 2 == 0 else 1

    def to_q4(q):
        return q.reshape(bs, t_new, KV_HEADS, GROUP, HEAD_DIM).transpose(0, 3, 1, 2, 4).reshape(
            bs // nseq, nseq, GROUP * t_new, KV_WIDTH)

    def from_q4(r):
        return r.reshape(bs, GROUP, t_new, KV_HEADS, HEAD_DIM).transpose(0, 2, 3, 1, 4).reshape(n_s, ATT_WIDTH)

    def new_rows(a):
        return a.reshape(bs // nseq, nseq, t_new, KV_WIDTH)

    sb_consts = [_suffix_matrix(page)]

    for i in range(depth):
        layer_w = (norm_mix3, w_in_b, qn, kn)
        post_w = (norm_mlp3, w_up_b, w_down_b, norm_ple3, w_pg_b, w_ple_b)

        (u, qt_sb, k_sb_b, vt_sb, qt_mf, qt_mb, k_mb_b, vt_mb, kmean, gate, *kv_stacks) = _inproj(
            xp, *layer_w, cos_p, sin_p, gmat, tm_p, i, kv_stacks, stack_shape)
        k3 = lambda a: a.reshape(n_p // ATT_TILE, ATT_TILE, KV_WIDTH)
        u3 = u.reshape(bp, seq, ATT_WIDTH)
        a = _pool_prompt(u3, pool_w_b, pool_scale3, min(512, seq), i).reshape(n_p, ATT_WIDTH)
        b = _sb_prompt(qt_sb, k3(k_sb_b), vt_sb, bp, tiles).reshape(n_p, ATT_WIDTH)
        c = _moba_prompt(qt_mf, qt_mb, k3(k_mb_b), vt_mb, kmean.reshape(bp, tiles, KV_WIDTH), bp, tiles).reshape(
            n_p, ATT_WIDTH)
        xp = _merge(xp, a, b, c, gate, w_branch_b, w_out_b, tm_p, i)
        xp = _mlp_ple(xp, pp, *post_w, tm_p, i)
        pool_p.append(u3[:, seq - POOL_HIST:])

        (u, q_sb, k_sb, v_sb, q_mb, k_mb, v_mb, gate) = _inproj(xs, *layer_w, cos_s, sin_s, gmat, tm_s, i)
        r3 = lambda a: a.reshape(bs, t_new, a.shape[-1])
        st_t = state_pool[i].transpose(1, 0, 2)
        a = _pool_sample(st_t, r3(u).transpose(1, 0, 2), pool_w_b, pool_scale3, past_len, i).transpose(1, 0, 2).reshape(
            n_s, ATT_WIDTH)
        b = from_q4(_sample_call(_sb_sample_kernel, "sb_sample", to_q4(q_sb), new_rows(k_sb), new_rows(v_sb), sb_consts,
                                 caches[0], caches[1], page_table, i))
        c = from_q4(_sample_call(_moba_sample_kernel, "moba_sample", to_q4(q_mb), new_rows(k_mb), new_rows(v_mb), [],
                                 caches[2], caches[3], page_table, i))
        xs = _merge(xs, a, b, c, gate, w_branch_b, w_out_b, tm_s, i)
        xs = _mlp_ple(xs, psm, *post_w, tm_s, i)
        pool_state = jnp.concatenate([state_pool[i][:, t_new:], r3(u)], axis=1)
        for lst, val in zip(outs_s, (k_sb, v_sb, k_mb, v_mb, pool_state)):
            lst.append(val)

    def kv_prompt(stack):
        return stack.reshape(depth, bp, KV_HEADS, HEAD_DIM, seq).transpose(0, 1, 4, 2, 3)

    def kv_sample(lst):
        return jnp.stack(lst).reshape(depth, bs, t_new, KV_HEADS, HEAD_DIM)

    return (xp.reshape(bp, seq, d), xs.reshape(bs, t_new, d),
            kv_prompt(kv_stacks[0]), kv_prompt(kv_stacks[1]), kv_prompt(kv_stacks[2]), kv_prompt(kv_stacks[3]),
            jnp.stack(pool_p),
            kv_sample(outs_s[0]), kv_sample(outs_s[1]), kv_sample(outs_s[2]), kv_sample(outs_s[3]),
            jnp.stack(outs_s[4]))
```

```python
import functools

import jax
import jax.numpy as jnp
import numpy as np
from jax import lax
from jax.experimental import pallas as pl
from jax.experimental.pallas import tpu as pltpu

F32 = jnp.float32
BF16 = jnp.bfloat16

HEAD_DIM = 64
KV_HEADS = 4
GROUP = 2
N_HEADS = KV_HEADS * GROUP
ATT_WIDTH = N_HEADS * HEAD_DIM
KV_WIDTH = KV_HEADS * HEAD_DIM
ATT_SCALE = HEAD_DIM ** -0.5
POOL_WINDOWS = (2, 4, 8, 16)
POOL_GROUP = 128
POOL_HIST = max(POOL_WINDOWS) - 1
ROPE_THETA = 500000.0
ROT_DIM = HEAD_DIM // 4
MOBA_BLOCK = 256
MOBA_TOPK = 3
N_BRANCH = 3
RMS_EPS = 1e-6
NEG_INF = float("-inf")
SB_DROP_DONE = 104.0
SB_EAGER_PAGES = 3

LANES = 128
BF16_ROWS = 16
ATT_TILE = MOBA_BLOCK
VMEM_LIMIT = 56 * 1024 * 1024


def _cparams(n_axes, vmem=None):
    return pltpu.CompilerParams(dimension_semantics=("arbitrary",) * n_axes, vmem_limit_bytes=vmem)


def _layer_spec(arr, layer):
    zeros = (0,) * (arr.ndim - 1)
    return pl.BlockSpec((None,) + arr.shape[1:], lambda *_: (layer,) + zeros)


def _mm(a, b):
    return jnp.dot(a, b, preferred_element_type=F32)


def _mm_nt(a, b):
    return lax.dot_general(a, b, (((1,), (1,)), ((), ())), preferred_element_type=F32)


def _split2(x):
    hi = x.astype(BF16)
    lo = (x - hi.astype(F32)).astype(BF16)
    return hi, lo


def _sigmoid(x):
    return 1.0 / (1.0 + jnp.exp(-x))


def _softplus(z):
    return jnp.maximum(z, 0.0) + jnp.log(1.0 + jnp.exp(jnp.minimum(z, -z)))


def _inproj_kernel(x_ref, gn_ref, w_ref, qn_ref, kn_ref, cos_ref, sin_ref, gmat_ref, *refs, prompt, n_alias):
    out_refs = refs[n_alias:]
    x = x_ref[...]
    ms = jnp.mean(x * x, axis=-1, keepdims=True)
    h = (x * lax.rsqrt(ms + RMS_EPS) * gn_ref[...]).astype(BF16)

    def proj(lo, hi):
        return _mm(h, w_ref[:, lo:hi])

    def head_norm_rope(y, g_ref):
        width = y.shape[1]
        gm = gmat_ref[0:width, 0:width]
        p0, p1 = _split2(y * y)
        ms_h = (_mm(p0, gm) + _mm(p1, gm)) * (1.0 / HEAD_DIM)
        yn = y * lax.rsqrt(ms_h + RMS_EPS) * g_ref[...]
        reps = width // LANES
        c = jnp.concatenate([cos_ref[...]] * reps, axis=1)
        s = jnp.concatenate([sin_ref[...]] * reps, axis=1)
        lane = lax.broadcasted_iota(jnp.int32, yn.shape, 1)
        first_half = (lane & (HEAD_DIM - 1)) < (ROT_DIM // 2)
        partner = jnp.where(first_half, pltpu.roll(yn, width - ROT_DIM // 2, 1), pltpu.roll(yn, ROT_DIM // 2, 1))
        return yn * c + partner * s

    o = 0
    u = proj(o, o + ATT_WIDTH)
    o += ATT_WIDTH
    q_sb = proj(o, o + ATT_WIDTH) * ATT_SCALE
    o += ATT_WIDTH
    k_sb = proj(o, o + KV_WIDTH)
    o += KV_WIDTH
    v_sb = proj(o, o + KV_WIDTH)
    o += KV_WIDTH
    q_mb = head_norm_rope(proj(o, o + ATT_WIDTH), qn_ref)
    o += ATT_WIDTH
    k_mb = head_norm_rope(proj(o, o + KV_WIDTH), kn_ref)
    o += KV_WIDTH
    v_mb = proj(o, o + KV_WIDTH)
    o += KV_WIDTH
    if prompt:
        (u_ref, qsb_t, ksb_b, vsb_tb, qmb_tf, qmb_t, kmb_b, vmb_tb, kmean_ref, gate_ref,
         ksb_t, vsb_t, kmb_t, vmb_t) = out_refs
        qsb_t[0] = q_sb.T.astype(BF16)
        ksb_b[...] = k_sb.astype(BF16)
        ksb_t[...] = k_sb.T
        v_t = v_sb.T
        vsb_t[...] = v_t
        vsb_tb[0] = v_t.astype(BF16)
        q_mb_t = q_mb.T
        qmb_tf[0] = q_mb_t
        qmb_t[0] = (q_mb_t * ATT_SCALE).astype(BF16)
        kmb_b[...] = k_mb.astype(BF16)
        kmb_t[...] = k_mb.T
        kmean_ref[0] = jnp.sum(k_mb, axis=0, keepdims=True) * (1.0 / k_mb.shape[0])
        v_t = v_mb.T
        vmb_t[...] = v_t
        vmb_tb[0] = v_t.astype(BF16)
    else:
        (u_ref, qsb_ref, ksb_ref, vsb_ref, qmb_ref, kmb_ref, vmb_ref, gate_ref) = out_refs
        qsb_ref[...] = q_sb
        qmb_ref[...] = q_mb
        ksb_ref[...] = k_sb
        vsb_ref[...] = v_sb
        kmb_ref[...] = k_mb
        vmb_ref[...] = v_mb
    u_ref[...] = u
    gate_w = gate_ref.shape[1]
    chunk = 512
    for c0 in range(0, gate_w, chunk):
        gate_ref[:, c0:c0 + chunk] = _sigmoid(proj(o + c0, o + c0 + chunk))


def _inproj(x, gn, w, qn, kn, cos_t, sin_t, gmat, tm, layer, kv_stacks=None, stack_shape=None):
    prompt = stack_shape is not None
    n, d = x.shape
    in_w = w.shape[2]
    gate_w = in_w - (3 * ATT_WIDTH + 4 * KV_WIDTH)
    nper = cos_t.shape[0] // tm
    nt = n // tm
    row = lambda i: (i, 0)
    const = lambda i: (0, 0)

    def flat(width, dtype):
        return jax.ShapeDtypeStruct((n, width), dtype), pl.BlockSpec((tm, width), row)

    def transposed(width, dtype):
        return jax.ShapeDtypeStruct((nt, width, tm), dtype), pl.BlockSpec((1, width, tm), lambda i: (i, 0, 0))

    alias_in, aliases = [], {}
    if prompt:
        tiles = stack_shape[3] // tm
        stacked = (jax.ShapeDtypeStruct(stack_shape, F32),
                   pl.BlockSpec((None, None, KV_WIDTH, tm), lambda i: (layer, i // tiles, 0, i % tiles)))
        outs = [flat(ATT_WIDTH, F32), transposed(ATT_WIDTH, BF16), flat(KV_WIDTH, BF16), transposed(KV_WIDTH, BF16),
                transposed(ATT_WIDTH, F32), transposed(ATT_WIDTH, BF16), flat(KV_WIDTH, BF16),
                transposed(KV_WIDTH, BF16),
                (jax.ShapeDtypeStruct((nt, 1, KV_WIDTH), F32), pl.BlockSpec((1, 1, KV_WIDTH), lambda i: (i, 0, 0))),
                flat(gate_w, F32), stacked, stacked, stacked, stacked]
        if kv_stacks is not None:
            alias_in = list(kv_stacks)
            aliases = {8 + k: len(outs) - 4 + k for k in range(4)}
    else:
        outs = [flat(ATT_WIDTH, F32), flat(ATT_WIDTH, F32), flat(KV_WIDTH, F32), flat(KV_WIDTH, F32),
                flat(ATT_WIDTH, F32), flat(KV_WIDTH, F32), flat(KV_WIDTH, F32), flat(gate_w, F32)]
    return pl.pallas_call(
        functools.partial(_inproj_kernel, prompt=prompt, n_alias=len(alias_in)),
        grid=(nt,),
        in_specs=[pl.BlockSpec((tm, d), row), _layer_spec(gn, layer), _layer_spec(w, layer),
                  _layer_spec(qn, layer), _layer_spec(kn, layer),
                  pl.BlockSpec((tm, LANES), lambda i: (i % nper, 0)), pl.BlockSpec((tm, LANES), lambda i: (i % nper, 0)),
                  pl.BlockSpec((ATT_WIDTH, ATT_WIDTH), const)]
                 + [pl.BlockSpec(memory_space=pl.ANY)] * len(alias_in),
        out_specs=[s for _, s in outs],
        out_shape=[s for s, _ in outs],
        input_output_aliases=aliases,
        compiler_params=_cparams(1, VMEM_LIMIT),
        name="inproj_prompt" if prompt else "inproj_sample",
    )(x, gn, w, qn, kn, cos_t, sin_t, gmat, *alias_in)


def _pool_prompt_kernel(cur_ref, prev_ref, pw_ref, ps_ref, o_ref, ext_ref, *, ts):
    t = pl.program_id(1)
    hist = ext_ref.shape[0] - ts
    ext_ref[0:hist, :] = jnp.where(t == 0, 0.0, prev_ref[0])
    ext_ref[hist:, :] = cur_ref[0]
    pos = t * ts + lax.broadcasted_iota(jnp.int32, (ts, POOL_GROUP), 0)
    for g, w in enumerate(POOL_WINDOWS):
        lo, hi = g * POOL_GROUP, (g + 1) * POOL_GROUP
        ws = ext_ref[hist:hist + ts, lo:hi]
        for k in range(1, w):
            ws = ws + ext_ref[hist - k:hist - k + ts, lo:hi]
        cnt = jnp.minimum(pos + 1, w).astype(F32)
        dlt = ws / cnt - ext_ref[hist:hist + ts, lo:hi]
        o_ref[0, :, lo:hi] = _mm(dlt.astype(BF16), pw_ref[g]) * ps_ref[:, lo:hi]


def _pool_prompt(u, pw, ps, ts, layer):
    b, s, c = u.shape
    hist = 16
    ratio = ts // hist
    return pl.pallas_call(
        functools.partial(_pool_prompt_kernel, ts=ts),
        grid=(b, s // ts),
        in_specs=[pl.BlockSpec((1, ts, c), lambda i, t: (i, t, 0)),
                  pl.BlockSpec((1, hist, c), lambda i, t: (i, jnp.maximum(t * ratio - 1, 0), 0)),
                  _layer_spec(pw, layer), _layer_spec(ps, layer)],
        out_specs=pl.BlockSpec((1, ts, c), lambda i, t: (i, t, 0)),
        out_shape=jax.ShapeDtypeStruct((b, s, c), F32),
        scratch_shapes=[pltpu.VMEM((ts + hist, c), F32)],
        compiler_params=_cparams(2),
        name="pool_prompt",
    )(u, u, pw, ps)


def _pool_sample_kernel(st_ref, u_ref, pw_ref, ps_ref, o_ref, *, start_pos):
    n_new = u_ref.shape[0]

    def row(idx, lo, hi):
        return st_ref[idx, :, lo:hi] if idx < POOL_HIST else u_ref[idx - POOL_HIST, :, lo:hi]

    for j in range(n_new):
        for g, w in enumerate(POOL_WINDOWS):
            lo, hi = g * POOL_GROUP, (g + 1) * POOL_GROUP
            ws = row(POOL_HIST + j, lo, hi)
            for k in range(1, w):
                ws = ws + row(POOL_HIST + j - k, lo, hi)
            cnt = float(min(start_pos + j + 1, w))
            dlt = ws / cnt - row(POOL_HIST + j, lo, hi)
            o_ref[j, :, lo:hi] = _mm(dlt.astype(BF16), pw_ref[g]) * ps_ref[:, lo:hi]


def _pool_sample(st_t, u_t, pw, ps, start_pos, layer):
    t, b, c = u_t.shape
    full = lambda shape: pl.BlockSpec(shape, lambda i: (0,) * len(shape))
    return pl.pallas_call(
        functools.partial(_pool_sample_kernel, start_pos=start_pos),
        grid=(1,),
        in_specs=[full(st_t.shape), full(u_t.shape), _layer_spec(pw, layer), _layer_spec(ps, layer)],
        out_specs=full((t, b, c)),
        out_shape=jax.ShapeDtypeStruct((t, b, c), F32),
        compiler_params=_cparams(1),
        name="pool_sample",
    )(st_t, u_t, pw, ps)


def _load_head_queries(qt_ref, qz_ref):
    tq = qt_ref.shape[2]

    @pl.when((pl.program_id(0) == 0) & (pl.program_id(1) == 0))
    def _():
        qz_ref[...] = jnp.zeros_like(qz_ref)

    for kvh in range(KV_HEADS):
        for g in range(GROUP):
            h = kvh * GROUP + g
            qz_ref[kvh, kvh * HEAD_DIM:(kvh + 1) * HEAD_DIM, g * tq:(g + 1) * tq] = (
                qt_ref[0, h * HEAD_DIM:(h + 1) * HEAD_DIM, :])


def _store_heads_out(o_ref, per_head, tq):
    out_t = jnp.concatenate(
        [per_head[kvh][:, g * tq:(g + 1) * tq] for kvh in range(KV_HEADS) for g in range(GROUP)], axis=0)
    o_ref[0] = out_t.T


def _att_specs(tiles, tq):
    return [pl.BlockSpec((1, ATT_WIDTH, tq), lambda i, j: (i * tiles + j, 0, 0)),
            pl.BlockSpec((tiles, tq, KV_WIDTH), lambda i, j: (i, 0, 0)),
            pl.BlockSpec((tiles, KV_WIDTH, tq), lambda i, j: (i, 0, 0))]


def _suffix_matrix_t(tk):
    r = np.arange(tk)
    later = (r[None, :] > r[:, None]).astype(np.float32)
    return jnp.asarray(np.concatenate([later, np.ones((BF16_ROWS, tk), np.float32)], axis=0), dtype=BF16)


def _sb_prompt_kernel(qt_ref, k_ref, vt_ref, u_ref, o_ref, qz_ref, acc_ref, carry_ref):
    i = pl.program_id(1)
    tk = k_ref.shape[1]
    q2 = qz_ref.shape[2]
    tq = q2 // GROUP
    _load_head_queries(qt_ref, qz_ref)
    acc_ref[...] = jnp.zeros_like(acc_ref)
    carry_ref[...] = jnp.zeros_like(carry_ref)
    key = lax.broadcasted_iota(jnp.int32, (tk, q2), 0)
    qloc = lax.broadcasted_iota(jnp.int32, (tk, q2), 1) & (tq - 1)
    diag_valid = key < qloc
    u2 = u_ref[...]

    def block(j, valid):
        kb = k_ref[j]
        heads = range(KV_HEADS)
        z = [_mm(kb, qz_ref[kvh]) for kvh in heads]
        drop = [_softplus(z[kvh]) for kvh in heads]
        log_take = [z[kvh] - drop[kvh] for kvh in heads]
        if valid is not None:
            drop = [jnp.where(valid, d, 0.0) for d in drop]
        cs = [_mm(u2, d.astype(BF16)) for d in drop]
        w = [jnp.exp(log_take[kvh] - (cs[kvh][0:tk] + carry_ref[kvh, 0:1, :])) for kvh in heads]
        if valid is not None:
            w = [jnp.where(valid, x, 0.0) for x in w]
        for kvh in heads:
            acc_ref[kvh] += _mm(vt_ref[j, kvh * HEAD_DIM:(kvh + 1) * HEAD_DIM, :], w[kvh].astype(BF16))
            carry_ref[kvh] += cs[kvh][tk:tk + 8]

    block(i, diag_valid)

    def all_dropped():
        return (jnp.min(carry_ref[...]) >= SB_DROP_DONE).astype(jnp.int32)

    def cond(state):
        t, done = state
        return jnp.logical_and(t < i, done == 0)

    def body(state):
        t, _ = state
        block(i - 1 - t, None)
        return t + 1, all_dropped()

    lax.while_loop(cond, body, (jnp.int32(0), all_dropped()))
    _store_heads_out(o_ref, [acc_ref[kvh] for kvh in range(KV_HEADS)], tq)


def _sb_prompt(qt, k3, vt3, b, tiles):
    tq = k3.shape[1]
    q2 = GROUP * tq
    return pl.pallas_call(
        _sb_prompt_kernel,
        grid=(b, tiles),
        in_specs=_att_specs(tiles, tq) + [pl.BlockSpec((tq + BF16_ROWS, tq), lambda i, j: (0, 0))],
        out_specs=pl.BlockSpec((1, tq, ATT_WIDTH), lambda i, j: (i, j, 0)),
        out_shape=jax.ShapeDtypeStruct((b, tiles * tq, ATT_WIDTH), F32),
        scratch_shapes=[pltpu.VMEM((KV_HEADS, KV_WIDTH, q2), BF16),
                        pltpu.VMEM((KV_HEADS, HEAD_DIM, q2), F32),
                        pltpu.VMEM((KV_HEADS, 8, q2), F32)],
        compiler_params=_cparams(2),
        name="sb_prompt",
    )(qt, k3, vt3, _suffix_matrix_t(tq))


def _page_specs(layer, slots, page, seq, nseq):
    def spec(p):
        return pl.BlockSpec((None, None, KV_WIDTH, page), lambda b, pt: (layer, pt[b * nseq + seq, p], 0, 0))
    return [spec(p) for p in slots]


def _sample_refs(refs, npages, nseq):
    kp = [refs[s * npages:(s + 1) * npages] for s in range(nseq)]
    vp = [refs[(nseq + s) * npages:(nseq + s + 1) * npages] for s in range(nseq)]
    o_ref, padk_ref, padv_ref = refs[2 * nseq * npages:]
    return kp, vp, o_ref, padk_ref, padv_ref


def _load_new_rows(kn_ref, vn_ref, padk_ref, padv_ref, nseq):
    n_new = kn_ref.shape[2]

    @pl.when(pl.program_id(0) == 0)
    def _():
        padk_ref[...] = jnp.zeros_like(padk_ref)
        padv_ref[...] = jnp.zeros_like(padv_ref)

    for s in range(nseq):
        padk_ref[s, 0:n_new, :] = kn_ref[0, s]
        padv_ref[s, 0:n_new, :] = vn_ref[0, s]


def _sample_queries(q8):
    rows = KV_HEADS * q8.shape[0]
    q = jnp.concatenate([q8] * KV_HEADS, axis=0)
    row = lax.broadcasted_iota(jnp.int32, (rows, KV_WIDTH), 0)
    lane = lax.broadcasted_iota(jnp.int32, (rows, KV_WIDTH), 1)
    return jnp.where((row >> 3) == (lane >> 6), q, 0.0)


def _sample_heads_out(acc):
    n = acc.shape[0] // KV_HEADS
    lane_head = lax.broadcasted_iota(jnp.int32, (n, KV_WIDTH), 1) >> 6
    out = jnp.zeros((n, KV_WIDTH), F32)
    for kvh in range(KV_HEADS):
        out = jnp.where(lane_head == kvh, acc[kvh * n:(kvh + 1) * n, :], out)
    return out


def _sample_call(body, name, q4, k_new, v_new, consts, cache_k, cache_v, page_table, layer):
    nsteps, nseq, rows, _ = q4.shape
    n_new = k_new.shape[2]
    npages = page_table.shape[1]
    page = cache_k.shape[3]
    pages = [spec for s in range(nseq) for spec in _page_specs(layer, range(npages), page, s, nseq)]
    grid_spec = pltpu.PrefetchScalarGridSpec(
        num_scalar_prefetch=1,
        grid=(nsteps,),
        in_specs=[pl.BlockSpec((1, nseq, rows, KV_WIDTH), lambda b, pt: (b, 0, 0, 0)),
                  pl.BlockSpec((1, nseq, n_new, KV_WIDTH), lambda b, pt: (b, 0, 0, 0)),
                  pl.BlockSpec((1, nseq, n_new, KV_WIDTH), lambda b, pt: (b, 0, 0, 0))]
                 + [pl.BlockSpec(c.shape, lambda b, pt: (0, 0)) for c in consts] + pages + pages,
        out_specs=pl.BlockSpec((1, nseq, rows, KV_WIDTH), lambda b, pt: (b, 0, 0, 0)),
        scratch_shapes=[pltpu.VMEM((nseq, page, KV_WIDTH), F32), pltpu.VMEM((nseq, page, KV_WIDTH), F32)])
    return pl.pallas_call(
        functools.partial(body, npages=npages, nseq=nseq),
        grid_spec=grid_spec,
        out_shape=jax.ShapeDtypeStruct(q4.shape, F32),
        compiler_params=_cparams(1),
        name=name,
    )(page_table, q4, k_new, v_new, *consts, *([cache_k] * (nseq * npages)), *([cache_v] * (nseq * npages)))


def _sb_sample_blocks(qz, kt, vt, new_kv, valid_new, umat, acc, carry):
    nseq, ncached = len(kt), len(kt[0])
    nblk = ncached + (new_kv is not None)
    rows, page = carry[0].shape
    work = [(s, i) for s in range(nseq) for i in range(nblk)]
    z = [_mm(qz[s], kt[s][i][...].astype(BF16)) if i < ncached else _mm_nt(qz[s], new_kv[s][0].astype(BF16))
         for s, i in work]
    drop = [_softplus(x) for x in z]
    take = [x - d for x, d in zip(z, drop)]
    drop = [d if i < ncached else jnp.where(valid_new, d, 0.0) for d, (s, i) in zip(drop, work)]
    parts = []
    for d in drop:
        parts.extend(_split2(d))
    cs = _mm(jnp.concatenate(parts, axis=0), umat)
    acc, carry = list(acc), list(carry)
    for i in reversed(range(nblk)):
        for s in range(nseq):
            j = work.index((s, i))
            c = cs[j * 2 * rows:(j + 1) * 2 * rows]
            both = c[0:rows] + c[rows:2 * rows]
            w = jnp.exp(take[j] - (both[:, 0:page] + carry[s]))
            if i < ncached:
                acc[s] = acc[s] + _mm_nt(w.astype(BF16), vt[s][i][...].astype(BF16))
            else:
                w = jnp.where(valid_new, w, 0.0)
                acc[s] = acc[s] + _mm(w.astype(BF16), new_kv[s][1].astype(BF16))
            carry[s] = carry[s] + both[:, page:]
    return acc, carry


def _sb_sample_head_kernel(pt_ref, q_ref, kn_ref, vn_ref, u_ref, *refs, npages, nseq):
    del pt_ref
    kp = [refs[s * npages:(s + 1) * npages] for s in range(nseq)]
    vp = [refs[(nseq + s) * npages:(nseq + s + 1) * npages] for s in range(nseq)]
    acc_ref, carry_ref, padk_ref, padv_ref = refs[2 * nseq * npages:]
    _load_new_rows(kn_ref, vn_ref, padk_ref, padv_ref, nseq)
    n_new = kn_ref.shape[2]
    page = u_ref.shape[0]
    rows = KV_HEADS * q_ref.shape[2]
    row = lax.broadcasted_iota(jnp.int32, (rows, page), 0)
    col = lax.broadcasted_iota(jnp.int32, (rows, page), 1)
    valid_new = col < (row & (n_new - 1))
    qz = [_sample_queries(q_ref[0, s]).astype(BF16) for s in range(nseq)]
    acc, carry = _sb_sample_blocks(
        qz, kp, vp, [(padk_ref[s], padv_ref[s]) for s in range(nseq)], valid_new, u_ref[...],
        [jnp.zeros((rows, KV_WIDTH), F32)] * nseq, [jnp.zeros((rows, page), F32)] * nseq)
    for s in range(nseq):
        acc_ref[0, s] = acc[s]
        carry_ref[0, s] = carry[s]


def _sb_sample_tail_kernel(pt_ref, q_ref, acc_in_ref, carry_in_ref, u_ref, *refs, npages, nseq):
    del pt_ref
    kp = [refs[s * npages:(s + 1) * npages] for s in range(nseq)]
    vp = [refs[(nseq + s) * npages:(nseq + s + 1) * npages] for s in range(nseq)]
    acc_ref = refs[2 * nseq * npages]
    qz = [_sample_queries(q_ref[0, s]).astype(BF16) for s in range(nseq)]
    acc, _ = _sb_sample_blocks(qz, kp, vp, None, None, u_ref[...],
                               [acc_in_ref[0, s] for s in range(nseq)], [carry_in_ref[0, s] for s in range(nseq)])
    for s in range(nseq):
        acc_ref[0, s] = acc[s]


def _sb_sample(q4, k_new, v_new, umat, cache_k, cache_v, page_table, layer, n_eager):
    nsteps, nseq, n, _ = q4.shape
    rows = KV_HEADS * n
    n_new = k_new.shape[2]
    npages = page_table.shape[1]
    page = cache_k.shape[3]
    n_eager = min(n_eager, npages)
    step4 = lambda b, pt: (b, 0, 0, 0)
    q_spec = pl.BlockSpec((1, nseq, n, KV_WIDTH), step4)
    new_spec = pl.BlockSpec((1, nseq, n_new, KV_WIDTH), step4)
    acc_spec = pl.BlockSpec((1, nseq, rows, KV_WIDTH), step4)
    carry_spec = pl.BlockSpec((1, nseq, rows, page), step4)
    u_spec = pl.BlockSpec(umat.shape, lambda b, pt: (0, 0))
    acc_shape = jax.ShapeDtypeStruct((nsteps, nseq, rows, KV_WIDTH), F32)
    carry_shape = jax.ShapeDtypeStruct((nsteps, nseq, rows, page), F32)

    def page_operands(slots):
        specs = [spec for s in range(nseq) for spec in _page_specs(layer, slots, page, s, nseq)]
        count = nseq * len(slots)
        return specs + specs, [cache_k] * count + [cache_v] * count

    specs, operands = page_operands(range(npages - n_eager, npages))
    acc, carry = pl.pallas_call(
        functools.partial(_sb_sample_head_kernel, npages=n_eager, nseq=nseq),
        grid_spec=pltpu.PrefetchScalarGridSpec(
            num_scalar_prefetch=1, grid=(nsteps,),
            in_specs=[q_spec, new_spec, new_spec, u_spec] + specs,
            out_specs=[acc_spec, carry_spec],
            scratch_shapes=[pltpu.VMEM((nseq, page, KV_WIDTH), F32), pltpu.VMEM((nseq, page, KV_WIDTH), F32)]),
        out_shape=[acc_shape, carry_shape],
        compiler_params=_cparams(1),
        name="sb_sample_head",
    )(page_table, q4, k_new, v_new, umat, *operands)
    if npages > n_eager:
        specs, operands = page_operands(range(npages - n_eager))

        def tail(acc_in):
            return pl.pallas_call(
                functools.partial(_sb_sample_tail_kernel, npages=npages - n_eager, nseq=nseq),
                grid_spec=pltpu.PrefetchScalarGridSpec(
                    num_scalar_prefetch=1, grid=(nsteps,),
                    in_specs=[q_spec, acc_spec, carry_spec, u_spec] + specs,
                    out_specs=acc_spec),
                out_shape=acc_shape,
                compiler_params=_cparams(1),
                name="sb_sample_tail",
            )(page_table, q4, acc_in, carry, umat, *operands)

        acc = lax.cond(jnp.min(carry) < SB_DROP_DONE, tail, lambda acc_in: acc_in, acc)
    a6 = acc.reshape(nsteps, nseq, KV_HEADS, n, KV_HEADS, HEAD_DIM)
    return jnp.stack([a6[:, :, h, :, h, :] for h in range(KV_HEADS)], axis=3).reshape(nsteps, nseq, n, KV_WIDTH)


def _suffix_matrix(tk):
    r = np.arange(tk)
    strict = (r[:, None] > r[None, :]).astype(np.float32)
    return jnp.asarray(np.concatenate([strict, np.ones((tk, tk), np.float32)], axis=1), dtype=BF16)


def _topk_rows(scores, n_valid, nb):
    blk = lax.broadcasted_iota(jnp.int32, scores.shape, 0)
    valid = blk < n_valid
    s = jnp.where(valid, scores, NEG_INF)
    rank = jnp.zeros(scores.shape, jnp.int32)
    for m in range(nb):
        other = s[m:m + 1, :]
        beats = (other > s) | ((other == s) & (blk > m))
        rank = rank + jnp.where(beats, 1, 0)
    return jnp.where(valid & (rank < MOBA_TOPK), 1.0, 0.0)


def _topk_lanes(scores, n_valid, nb):
    blk = lax.broadcasted_iota(jnp.int32, scores.shape, 1)
    valid = blk < n_valid
    s = jnp.where(valid, scores, NEG_INF)
    rank = jnp.zeros(scores.shape, jnp.int32)
    for m in range(nb):
        other = s[:, m:m + 1]
        beats = (other > s) | ((other == s) & (blk > m))
        rank = rank + jnp.where(beats, 1, 0)
    return jnp.where(valid & (rank < MOBA_TOPK), 1.0, 0.0)


def _moba_prompt_kernel(qtf_ref, qt_ref, k_ref, vt_ref, km_ref, o_ref,
                        qz_ref, sel_ref, m_ref, l_ref, acc_ref, *, nb):
    qblk = pl.program_id(1)
    tk = k_ref.shape[1]
    q2 = qz_ref.shape[2]
    tq = q2 // GROUP
    _load_head_queries(qt_ref, qz_ref)
    heads = range(KV_HEADS)
    qf = jnp.concatenate(
        [jnp.concatenate([qtf_ref[0, (kvh * GROUP + g) * HEAD_DIM:(kvh * GROUP + g + 1) * HEAD_DIM, :]
                          for kvh in heads], axis=0) for g in range(GROUP)], axis=1)
    q_hi, q_lo = _split2(qf)
    km = km_ref[0]
    lane_head = lax.broadcasted_iota(jnp.int32, km.shape, 1) >> 6
    km_parts = [_split2(jnp.where(lane_head == kvh, km, 0.0)) for kvh in heads]
    scores = [_mm(km_hi, q_hi) + (_mm(km_hi, q_lo) + _mm(km_lo, q_hi)) for km_hi, km_lo in km_parts]
    for kvh in heads:
        sel_ref[kvh] = _topk_rows(scores[kvh], qblk, nb)
    key = lax.broadcasted_iota(jnp.int32, (tk, q2), 0)
    qloc = lax.broadcasted_iota(jnp.int32, (tk, q2), 1) & (tq - 1)
    causal = key <= qloc
    kb = k_ref[qblk]
    st = [jnp.where(causal, _mm(kb, qz_ref[kvh]), NEG_INF) for kvh in heads]
    m0 = [jnp.max(s, axis=0, keepdims=True) for s in st]
    p = [jnp.exp(st[kvh] - m0[kvh]) for kvh in heads]
    for kvh in heads:
        m_ref[kvh] = m0[kvh]
        l_ref[kvh] = jnp.sum(p[kvh], axis=0, keepdims=True)
        acc_ref[kvh] = _mm(vt_ref[qblk, kvh * HEAD_DIM:(kvh + 1) * HEAD_DIM, :], p[kvh].astype(BF16))

    def body(n, c):
        kb = k_ref[n]
        bias = [jnp.where(sel_ref[kvh, pl.ds(n, 1), :] > 0.5, 0.0, NEG_INF) for kvh in heads]
        st = [_mm(kb, qz_ref[kvh]) for kvh in heads]
        m_old = [m_ref[kvh] for kvh in heads]
        m_new = [jnp.maximum(m_old[kvh], jnp.max(st[kvh], axis=0, keepdims=True) + bias[kvh]) for kvh in heads]
        p = [jnp.exp(st[kvh] - (m_new[kvh] - bias[kvh])) for kvh in heads]
        alpha = [jnp.exp(m_old[kvh] - m_new[kvh]) for kvh in heads]
        pv = [_mm(vt_ref[n, kvh * HEAD_DIM:(kvh + 1) * HEAD_DIM, :], p[kvh].astype(BF16)) for kvh in heads]
        for kvh in heads:
            l_ref[kvh] = alpha[kvh] * l_ref[kvh] + jnp.sum(p[kvh], axis=0, keepdims=True)
            acc_ref[kvh] = alpha[kvh] * acc_ref[kvh] + pv[kvh]
            m_ref[kvh] = m_new[kvh]
        return c

    lax.fori_loop(0, qblk, body, 0)
    _store_heads_out(o_ref, [acc_ref[kvh] / l_ref[kvh] for kvh in heads], tq)


def _moba_prompt(qtf, qt, k3, vt3, kmean, b, tiles):
    tq = k3.shape[1]
    q2 = GROUP * tq
    return pl.pallas_call(
        functools.partial(_moba_prompt_kernel, nb=tiles),
        grid=(b, tiles),
        in_specs=[pl.BlockSpec((1, ATT_WIDTH, tq), lambda i, j: (i * tiles + j, 0, 0))] + _att_specs(tiles, tq)
                 + [pl.BlockSpec((1, tiles, KV_WIDTH), lambda i, j: (i, 0, 0))],
        out_specs=pl.BlockSpec((1, tq, ATT_WIDTH), lambda i, j: (i, j, 0)),
        out_shape=jax.ShapeDtypeStruct((b, tiles * tq, ATT_WIDTH), F32),
        scratch_shapes=[pltpu.VMEM((KV_HEADS, KV_WIDTH, q2), BF16), pltpu.VMEM((KV_HEADS, tiles, q2), F32),
                        pltpu.VMEM((KV_HEADS, 1, q2), F32), pltpu.VMEM((KV_HEADS, 1, q2), F32),
                        pltpu.VMEM((KV_HEADS, HEAD_DIM, q2), F32)],
        compiler_params=_cparams(2),
        name="moba_prompt",
    )(qtf, qt, k3, vt3, kmean)


def _moba_sample_kernel(pt_ref, q_ref, kn_ref, vn_ref, *refs, npages, nseq):
    del pt_ref
    kp, vp, o_ref, padk_ref, padv_ref = _sample_refs(refs, npages, nseq)
    _load_new_rows(kn_ref, vn_ref, padk_ref, padv_ref, nseq)
    n_new = kn_ref.shape[2]
    page = padk_ref.shape[1]
    per_blk = MOBA_BLOCK // page
    nb_past = npages // per_blk
    rows = KV_HEADS * q_ref.shape[2]
    seqs = range(nseq)
    qf = [_sample_queries(q_ref[0, s]) for s in seqs]
    qz = [(q * ATT_SCALE).astype(BF16) for q in qf]
    ones = jnp.ones((page, LANES), BF16)
    lane = lax.broadcasted_iota(jnp.int32, (KV_WIDTH, LANES), 1)
    k_parts = [[_split2(kp[s][pg][...]) for pg in range(npages)] for s in seqs]
    k_sums = [[_mm(hi, ones) + _mm(lo, ones) for hi, lo in k_parts[s]] for s in seqs]
    km = []
    for s in seqs:
        km_s = jnp.zeros((KV_WIDTH, LANES), F32)
        for blk in range(nb_past):
            tot = k_sums[s][blk * per_blk]
            for r in range(1, per_blk):
                tot = tot + k_sums[s][blk * per_blk + r]
            km_s = jnp.where(lane == blk, tot * (1.0 / MOBA_BLOCK), km_s)
        km.append(km_s)
    sel = []
    for s in seqs:
        q_hi, q_lo = _split2(qf[s])
        km_hi, km_lo = _split2(km[s])
        scores = _mm(q_hi, km_hi) + (_mm(q_hi, km_lo) + _mm(q_lo, km_hi))
        sel.append(_topk_lanes(scores, nb_past, nb_past))
    row = lax.broadcasted_iota(jnp.int32, (rows, page), 0)
    col = lax.broadcasted_iota(jnp.int32, (rows, page), 1)
    visible = (col <= (row & (n_new - 1))) & (col < n_new)
    logits = [[jnp.where(sel[s][:, pg // per_blk:pg // per_blk + 1] > 0.5,
                         _mm(qz[s], kp[s][pg][...].astype(BF16)), NEG_INF) for pg in range(npages)]
              + [jnp.where(visible, _mm_nt(qz[s], padk_ref[s].astype(BF16)), NEG_INF)] for s in seqs]
    m = []
    for s in seqs:
        m_el = logits[s][0]
        for x in logits[s][1:]:
            m_el = jnp.maximum(m_el, x)
        m.append(jnp.max(m_el, axis=1, keepdims=True))
    acc = [jnp.zeros((rows, KV_WIDTH), F32) for _ in seqs]
    l_el = [jnp.zeros((rows, page), F32) for _ in seqs]
    for pg in range(npages + 1):
        for s in seqs:
            p = jnp.exp(logits[s][pg] - m[s])
            l_el[s] = l_el[s] + p
            if pg < npages:
                acc[s] = acc[s] + _mm_nt(p.astype(BF16), vp[s][pg][...].astype(BF16))
            else:
                acc[s] = acc[s] + _mm(p.astype(BF16), padv_ref[s].astype(BF16))
    for s in seqs:
        o_ref[0, s] = _sample_heads_out(acc[s] / jnp.sum(l_el[s], axis=1, keepdims=True))


def _merge_kernel(x_ref, a_ref, b_ref, c_ref, g_ref, wb_ref, wo_ref, o_ref):
    d = x_ref.shape[1]
    m = g_ref[:, 0:d] * _mm(a_ref[...].astype(BF16), wb_ref[0])
    m = m + g_ref[:, d:2 * d] * _mm(b_ref[...].astype(BF16), wb_ref[1])
    m = m + g_ref[:, 2 * d:3 * d] * _mm(c_ref[...].astype(BF16), wb_ref[2])
    o_ref[...] = x_ref[...] + _mm(m.astype(BF16), wo_ref[...])


def _merge(x, a, b, c, gate, wb, wo, tm, layer):
    n, d = x.shape
    row = lambda i: (i, 0)
    return pl.pallas_call(
        _merge_kernel,
        grid=(n // tm,),
        in_specs=[pl.BlockSpec((tm, d), row), pl.BlockSpec((tm, ATT_WIDTH), row), pl.BlockSpec((tm, ATT_WIDTH), row),
                  pl.BlockSpec((tm, ATT_WIDTH), row), pl.BlockSpec((tm, N_BRANCH * d), row),
                  _layer_spec(wb, layer), _layer_spec(wo, layer)],
        out_specs=pl.BlockSpec((tm, d), row),
        out_shape=jax.ShapeDtypeStruct((n, d), F32),
        compiler_params=_cparams(1, VMEM_LIMIT),
        name="merge",
    )(x, a, b, c, gate, wb, wo)


def _mlp_ple_kernel(x_ref, p_ref, gm_ref, wu_ref, wd_ref, gp_ref, wg_ref, wp_ref, o_ref, *, f_chunk):
    x = x_ref[...]
    ms = jnp.mean(x * x, axis=-1, keepdims=True)
    h = (x * lax.rsqrt(ms + RMS_EPS) * gm_ref[...]).astype(BF16)
    d_ff = wu_ref.shape[1]
    y = x
    for f0 in range(0, d_ff, f_chunk):
        hid = jnp.maximum(_mm(h, wu_ref[:, f0:f0 + f_chunk]), 0.0)
        y = y + _mm((hid * hid).astype(BF16), wd_ref[f0:f0 + f_chunk, :])
    ms2 = jnp.mean(y * y, axis=-1, keepdims=True)
    h2 = (y * lax.rsqrt(ms2 + RMS_EPS) * gp_ref[...]).astype(BF16)
    gate = _sigmoid(_mm(h2, wg_ref[...]))
    o_ref[...] = y + gate * _mm(p_ref[...].astype(BF16), wp_ref[...])


def _mlp_ple(x, p, gm, wu, wd, gp, wg, wp, tm, layer):
    n, d = x.shape
    row = lambda i: (i, 0)
    return pl.pallas_call(
        functools.partial(_mlp_ple_kernel, f_chunk=512),
        grid=(n // tm,),
        in_specs=[pl.BlockSpec((tm, d), row), pl.BlockSpec((None, tm, p.shape[2]), lambda i: (layer, i, 0)),
                  _layer_spec(gm, layer), _layer_spec(wu, layer), _layer_spec(wd, layer),
                  _layer_spec(gp, layer), _layer_spec(wg, layer), _layer_spec(wp, layer)],
        out_specs=pl.BlockSpec((tm, d), row),
        out_shape=jax.ShapeDtypeStruct((n, d), F32),
        compiler_params=_cparams(1, VMEM_LIMIT),
        name="mlp_ple",
    )(x, p, gm, wu, wd, gp, wg, wp)


def _rope_tables(pos):
    half = ROT_DIM // 2
    inv = ROPE_THETA ** (-jnp.arange(half, dtype=F32) / half)
    ang = pos.astype(F32)[:, None] * inv[None, :]
    cos, sin = jnp.cos(ang), jnp.sin(ang)
    t = pos.shape[0]
    c64 = jnp.concatenate([cos, cos, jnp.ones((t, HEAD_DIM - ROT_DIM), F32)], axis=1)
    s64 = jnp.concatenate([-sin, sin, jnp.zeros((t, HEAD_DIM - ROT_DIM), F32)], axis=1)
    reps = LANES // HEAD_DIM
    return jnp.tile(c64, (1, reps)), jnp.tile(s64, (1, reps))


def _head_group_matrix():
    r = np.arange(ATT_WIDTH) // HEAD_DIM
    return jnp.asarray((r[:, None] == r[None, :]).astype(np.float32), dtype=BF16)


def kernel(x_prompt, x_sample, p_prompt, p_sample, cache_sb_k, cache_sb_v, cache_moba_k, cache_moba_v,
           state_pool, page_table, norm_mix, w_in, pool_w, pool_scale, moba_q_norm, moba_k_norm,
           w_branch, w_out, norm_mlp, w_up, w_down, norm_ple, w_ple_gate, w_ple):
    depth = w_in.shape[0]
    bp, seq, d = x_prompt.shape
    bs, t_new, _ = x_sample.shape
    n_p, n_s = bp * seq, bs * t_new
    n_phys, page = cache_sb_k.shape[1], cache_sb_k.shape[2]
    past_len = page_table.shape[1] * page
    tiles = seq // ATT_TILE
    assert seq % ATT_TILE == 0 and past_len % MOBA_BLOCK == 0 and MOBA_BLOCK % page == 0 and page == LANES
    assert t_new & (t_new - 1) == 0 and GROUP * t_new == 8 and past_len >= POOL_HIST >= t_new
    assert past_len // MOBA_BLOCK <= LANES

    w_in_b, w_branch_b = w_in.astype(BF16), w_branch.astype(BF16)
    w_out_b, w_up_b, w_down_b = w_out.astype(BF16), w_up.astype(BF16), w_down.astype(BF16)
    w_pg_b, w_ple_b, pool_w_b = w_ple_gate.astype(BF16), w_ple.astype(BF16), pool_w.astype(BF16)
    vec = lambda a: a[:, None, :]
    norm_mix3, norm_mlp3, norm_ple3, pool_scale3 = vec(norm_mix), vec(norm_mlp), vec(norm_ple), vec(pool_scale)
    qn = vec(jnp.tile(moba_q_norm, (1, N_HEADS)))
    kn = vec(jnp.tile(moba_k_norm, (1, KV_HEADS)))
    gmat = _head_group_matrix()

    tm_p = ATT_TILE
    tm_s = 256 if n_s % 256 == 0 else n_s
    cos_p, sin_p = _rope_tables(jnp.arange(seq))
    cos_s, sin_s = _rope_tables(jnp.tile(past_len + jnp.arange(t_new), bs))
    caches = [c.transpose(0, 1, 3, 4, 2).reshape(depth, n_phys, KV_WIDTH, page)
              for c in (cache_sb_k, cache_sb_v, cache_moba_k, cache_moba_v)]
    pp = p_prompt.reshape(depth, n_p, -1)
    psm = p_sample.reshape(depth, n_s, -1)

    xp = x_prompt.reshape(n_p, d)
    xs = x_sample.reshape(n_s, d)
    kv_stacks = None
    stack_shape = (depth, bp, KV_WIDTH, seq)
    pool_p = []
    outs_s = [[] for _ in range(5)]
    nseq = 2 if bs % 2 == 0 else 1

    def to_q4(q):
        return q.reshape(bs, t_new, KV_HEADS, GROUP, HEAD_DIM).transpose(0, 3, 1, 2, 4).reshape(
            bs // nseq, nseq, GROUP * t_new, KV_WIDTH)

    def from_q4(r):
        return r.reshape(bs, GROUP, t_new, KV_HEADS, HEAD_DIM).transpose(0, 2, 3, 1, 4).reshape(n_s, ATT_WIDTH)

    def new_rows(a):
        return a.reshape(bs // nseq, nseq, t_new, KV_WIDTH)

    sb_umat = _suffix_matrix(page)

    for i in range(depth):
        layer_w = (norm_mix3, w_in_b, qn, kn)
        post_w = (norm_mlp3, w_up_b, w_down_b, norm_ple3, w_pg_b, w_ple_b)

        (u, qt_sb, k_sb_b, vt_sb, qt_mf, qt_mb, k_mb_b, vt_mb, kmean, gate, *kv_stacks) = _inproj(
            xp, *layer_w, cos_p, sin_p, gmat, tm_p, i, kv_stacks, stack_shape)
        k3 = lambda a: a.reshape(n_p // ATT_TILE, ATT_TILE, KV_WIDTH)
        u3 = u.reshape(bp, seq, ATT_WIDTH)
        a = _pool_prompt(u3, pool_w_b, pool_scale3, min(512, seq), i).reshape(n_p, ATT_WIDTH)
        b = _sb_prompt(qt_sb, k3(k_sb_b), vt_sb, bp, tiles).reshape(n_p, ATT_WIDTH)
        c = _moba_prompt(qt_mf, qt_mb, k3(k_mb_b), vt_mb, kmean.reshape(bp, tiles, KV_WIDTH), bp, tiles).reshape(
            n_p, ATT_WIDTH)
        xp = _merge(xp, a, b, c, gate, w_branch_b, w_out_b, tm_p, i)
        xp = _mlp_ple(xp, pp, *post_w, tm_p, i)
        pool_p.append(u3[:, seq - POOL_HIST:])

        (u, q_sb, k_sb, v_sb, q_mb, k_mb, v_mb, gate) = _inproj(xs, *layer_w, cos_s, sin_s, gmat, tm_s, i)
        r3 = lambda a: a.reshape(bs, t_new, a.shape[-1])
        st_t = state_pool[i].transpose(1, 0, 2)
        a = _pool_sample(st_t, r3(u).transpose(1, 0, 2), pool_w_b, pool_scale3, past_len, i).transpose(1, 0, 2).reshape(
            n_s, ATT_WIDTH)
        b = from_q4(_sb_sample(to_q4(q_sb), new_rows(k_sb), new_rows(v_sb), sb_umat, caches[0], caches[1],
                               page_table, i, SB_EAGER_PAGES))
        c = from_q4(_sample_call(_moba_sample_kernel, "moba_sample", to_q4(q_mb), new_rows(k_mb), new_rows(v_mb), [],
                                 caches[2], caches[3], page_table, i))
        xs = _merge(xs, a, b, c, gate, w_branch_b, w_out_b, tm_s, i)
        xs = _mlp_ple(xs, psm, *post_w, tm_s, i)
        pool_state = jnp.concatenate([state_pool[i][:, t_new:], r3(u)], axis=1)
        for lst, val in zip(outs_s, (k_sb, v_sb, k_mb, v_mb, pool_state)):
            lst.append(val)

    def kv_prompt(stack):
        return stack.reshape(depth, bp, KV_HEADS, HEAD_DIM, seq).transpose(0, 1, 4, 2, 3)

    def kv_sample(lst):
        return jnp.stack(lst).reshape(depth, bs, t_new, KV_HEADS, HEAD_DIM)

    return (xp.reshape(bp, seq, d), xs.reshape(bs, t_new, d),
            kv_prompt(kv_stacks[0]), kv_prompt(kv_stacks[1]), kv_prompt(kv_stacks[2]), kv_prompt(kv_stacks[3]),
            jnp.stack(pool_p),
            kv_sample(outs_s[0]), kv_sample(outs_s[1]), kv_sample(outs_s[2]), kv_sample(outs_s[3]),
            jnp.stack(outs_s[4]))
```

```python
import functools

import jax
import jax.numpy as jnp
import numpy as np
from jax import lax
from jax.experimental import pallas as pl
from jax.experimental.pallas import tpu as pltpu

F32 = jnp.float32
BF16 = jnp.bfloat16

HEAD_DIM = 64
KV_HEADS = 4
GROUP = 2
N_HEADS = KV_HEADS * GROUP
ATT_WIDTH = N_HEADS * HEAD_DIM
KV_WIDTH = KV_HEADS * HEAD_DIM
ATT_SCALE = HEAD_DIM ** -0.5
POOL_WINDOWS = (2, 4, 8, 16)
POOL_GROUP = 128
POOL_HIST = max(POOL_WINDOWS) - 1
ROPE_THETA = 500000.0
ROT_DIM = HEAD_DIM // 4
MOBA_BLOCK = 256
MOBA_TOPK = 3
N_BRANCH = 3
RMS_EPS = 1e-6
NEG_INF = float("-inf")
SB_DROP_DONE = 104.0
SB_EAGER_PAGES = 3

LANES = 128
BF16_ROWS = 16
ATT_TILE = MOBA_BLOCK
VMEM_LIMIT = 56 * 1024 * 1024


def _cparams(n_axes, vmem=None):
    return pltpu.CompilerParams(dimension_semantics=("arbitrary",) * n_axes, vmem_limit_bytes=vmem)


def _layer_spec(arr, layer):
    zeros = (0,) * (arr.ndim - 1)
    return pl.BlockSpec((None,) + arr.shape[1:], lambda *_: (layer,) + zeros)


def _mm(a, b):
    return jnp.dot(a, b, preferred_element_type=F32)


def _mm_nt(a, b):
    return lax.dot_general(a, b, (((1,), (1,)), ((), ())), preferred_element_type=F32)


def _split2(x):
    hi = x.astype(BF16)
    lo = (x - hi.astype(F32)).astype(BF16)
    return hi, lo


def _sigmoid(x):
    return 1.0 / (1.0 + jnp.exp(-x))


def _softplus(z):
    return jnp.maximum(z, 0.0) + jnp.log(1.0 + jnp.exp(jnp.minimum(z, -z)))


def _inproj_kernel(x_ref, gn_ref, w_ref, qn_ref, kn_ref, cos_ref, sin_ref, gmat_ref, *refs, prompt, n_alias):
    out_refs = refs[n_alias:]
    x = x_ref[...]
    ms = jnp.mean(x * x, axis=-1, keepdims=True)
    h = (x * lax.rsqrt(ms + RMS_EPS) * gn_ref[...]).astype(BF16)

    def proj(lo, hi):
        return _mm(h, w_ref[:, lo:hi])

    def head_norm_rope(y, g_ref):
        width = y.shape[1]
        gm = gmat_ref[0:width, 0:width]
        p0, p1 = _split2(y * y)
        ms_h = (_mm(p0, gm) + _mm(p1, gm)) * (1.0 / HEAD_DIM)
        yn = y * lax.rsqrt(ms_h + RMS_EPS) * g_ref[...]
        reps = width // LANES
        c = jnp.concatenate([cos_ref[...]] * reps, axis=1)
        s = jnp.concatenate([sin_ref[...]] * reps, axis=1)
        lane = lax.broadcasted_iota(jnp.int32, yn.shape, 1)
        first_half = (lane & (HEAD_DIM - 1)) < (ROT_DIM // 2)
        partner = jnp.where(first_half, pltpu.roll(yn, width - ROT_DIM // 2, 1), pltpu.roll(yn, ROT_DIM // 2, 1))
        return yn * c + partner * s

    o = 0
    u = proj(o, o + ATT_WIDTH)
    o += ATT_WIDTH
    q_sb = proj(o, o + ATT_WIDTH) * ATT_SCALE
    o += ATT_WIDTH
    k_sb = proj(o, o + KV_WIDTH)
    o += KV_WIDTH
    v_sb = proj(o, o + KV_WIDTH)
    o += KV_WIDTH
    q_mb = head_norm_rope(proj(o, o + ATT_WIDTH), qn_ref)
    o += ATT_WIDTH
    k_mb = head_norm_rope(proj(o, o + KV_WIDTH), kn_ref)
    o += KV_WIDTH
    v_mb = proj(o, o + KV_WIDTH)
    o += KV_WIDTH
    if prompt:
        (u_ref, qsb_t, ksb_b, vsb_tb, qmb_tf, qmb_t, kmb_b, vmb_tb, kmean_ref, gate_ref,
         ksb_t, vsb_t, kmb_t, vmb_t) = out_refs
        qsb_t[0] = q_sb.T.astype(BF16)
        ksb_b[...] = k_sb.astype(BF16)
        ksb_t[...] = k_sb.T
        v_t = v_sb.T
        vsb_t[...] = v_t
        vsb_tb[0] = v_t.astype(BF16)
        q_mb_t = q_mb.T
        qmb_tf[0] = q_mb_t
        qmb_t[0] = (q_mb_t * ATT_SCALE).astype(BF16)
        kmb_b[...] = k_mb.astype(BF16)
        kmb_t[...] = k_mb.T
        kmean_ref[0] = jnp.sum(k_mb, axis=0, keepdims=True) * (1.0 / k_mb.shape[0])
        v_t = v_mb.T
        vmb_t[...] = v_t
        vmb_tb[0] = v_t.astype(BF16)
    else:
        (u_ref, qsb_ref, ksb_ref, vsb_ref, qmb_ref, kmb_ref, vmb_ref, gate_ref) = out_refs
        qsb_ref[...] = q_sb
        qmb_ref[...] = q_mb
        ksb_ref[...] = k_sb
        vsb_ref[...] = v_sb
        kmb_ref[...] = k_mb
        vmb_ref[...] = v_mb
    u_ref[...] = u
    gate_w = gate_ref.shape[1]
    chunk = 512
    for c0 in range(0, gate_w, chunk):
        gate_ref[:, c0:c0 + chunk] = _sigmoid(proj(o + c0, o + c0 + chunk)).astype(gate_ref.dtype)


def _inproj(x, gn, w, qn, kn, cos_t, sin_t, gmat, tm, layer, kv_stacks=None, stack_shape=None):
    prompt = stack_shape is not None
    n, d = x.shape
    in_w = w.shape[2]
    gate_w = in_w - (3 * ATT_WIDTH + 4 * KV_WIDTH)
    nper = cos_t.shape[0] // tm
    nt = n // tm
    row = lambda i: (i, 0)
    const = lambda i: (0, 0)

    def flat(width, dtype):
        return jax.ShapeDtypeStruct((n, width), dtype), pl.BlockSpec((tm, width), row)

    def transposed(width, dtype):
        return jax.ShapeDtypeStruct((nt, width, tm), dtype), pl.BlockSpec((1, width, tm), lambda i: (i, 0, 0))

    alias_in, aliases = [], {}
    if prompt:
        tiles = stack_shape[3] // tm
        stacked = (jax.ShapeDtypeStruct(stack_shape, F32),
                   pl.BlockSpec((None, None, KV_WIDTH, tm), lambda i: (layer, i // tiles, 0, i % tiles)))
        outs = [flat(ATT_WIDTH, F32), transposed(ATT_WIDTH, BF16), flat(KV_WIDTH, BF16), transposed(KV_WIDTH, BF16),
                transposed(ATT_WIDTH, F32), transposed(ATT_WIDTH, BF16), flat(KV_WIDTH, BF16),
                transposed(KV_WIDTH, BF16),
                (jax.ShapeDtypeStruct((nt, 1, KV_WIDTH), F32), pl.BlockSpec((1, 1, KV_WIDTH), lambda i: (i, 0, 0))),
                flat(gate_w, BF16), stacked, stacked, stacked, stacked]
        if kv_stacks is not None:
            alias_in = list(kv_stacks)
            aliases = {8 + k: len(outs) - 4 + k for k in range(4)}
    else:
        outs = [flat(ATT_WIDTH, F32), flat(ATT_WIDTH, F32), flat(KV_WIDTH, F32), flat(KV_WIDTH, F32),
                flat(ATT_WIDTH, F32), flat(KV_WIDTH, F32), flat(KV_WIDTH, F32), flat(gate_w, BF16)]
    return pl.pallas_call(
        functools.partial(_inproj_kernel, prompt=prompt, n_alias=len(alias_in)),
        grid=(nt,),
        in_specs=[pl.BlockSpec((tm, d), row), _layer_spec(gn, layer), _layer_spec(w, layer),
                  _layer_spec(qn, layer), _layer_spec(kn, layer),
                  pl.BlockSpec((tm, LANES), lambda i: (i % nper, 0)), pl.BlockSpec((tm, LANES), lambda i: (i % nper, 0)),
                  pl.BlockSpec((ATT_WIDTH, ATT_WIDTH), const)]
                 + [pl.BlockSpec(memory_space=pl.ANY)] * len(alias_in),
        out_specs=[s for _, s in outs],
        out_shape=[s for s, _ in outs],
        input_output_aliases=aliases,
        compiler_params=_cparams(1, VMEM_LIMIT),
        name="inproj_prompt" if prompt else "inproj_sample",
    )(x, gn, w, qn, kn, cos_t, sin_t, gmat, *alias_in)


def _pool_prompt_kernel(cur_ref, prev_ref, pw_ref, ps_ref, o_ref, ext_ref, *, ts):
    t = pl.program_id(1)
    hist = ext_ref.shape[0] - ts
    ext_ref[0:hist, :] = jnp.where(t == 0, 0.0, prev_ref[0])
    ext_ref[hist:, :] = cur_ref[0]
    pos = t * ts + lax.broadcasted_iota(jnp.int32, (ts, POOL_GROUP), 0)
    for g, w in enumerate(POOL_WINDOWS):
        lo, hi = g * POOL_GROUP, (g + 1) * POOL_GROUP
        ws = ext_ref[hist:hist + ts, lo:hi]
        for k in range(1, w):
            ws = ws + ext_ref[hist - k:hist - k + ts, lo:hi]
        cnt = jnp.minimum(pos + 1, w).astype(F32)
        dlt = ws / cnt - ext_ref[hist:hist + ts, lo:hi]
        o_ref[0, :, lo:hi] = (_mm(dlt.astype(BF16), pw_ref[g]) * ps_ref[:, lo:hi]).astype(o_ref.dtype)


def _pool_prompt(u, pw, ps, ts, layer):
    b, s, c = u.shape
    hist = 16
    ratio = ts // hist
    return pl.pallas_call(
        functools.partial(_pool_prompt_kernel, ts=ts),
        grid=(b, s // ts),
        in_specs=[pl.BlockSpec((1, ts, c), lambda i, t: (i, t, 0)),
                  pl.BlockSpec((1, hist, c), lambda i, t: (i, jnp.maximum(t * ratio - 1, 0), 0)),
                  _layer_spec(pw, layer), _layer_spec(ps, layer)],
        out_specs=pl.BlockSpec((1, ts, c), lambda i, t: (i, t, 0)),
        out_shape=jax.ShapeDtypeStruct((b, s, c), BF16),
        scratch_shapes=[pltpu.VMEM((ts + hist, c), F32)],
        compiler_params=_cparams(2),
        name="pool_prompt",
    )(u, u, pw, ps)


def _pool_sample_kernel(st_ref, u_ref, pw_ref, ps_ref, o_ref, *, start_pos):
    n_new = u_ref.shape[0]

    def row(idx, lo, hi):
        return st_ref[idx, :, lo:hi] if idx < POOL_HIST else u_ref[idx - POOL_HIST, :, lo:hi]

    for j in range(n_new):
        for g, w in enumerate(POOL_WINDOWS):
            lo, hi = g * POOL_GROUP, (g + 1) * POOL_GROUP
            ws = row(POOL_HIST + j, lo, hi)
            for k in range(1, w):
                ws = ws + row(POOL_HIST + j - k, lo, hi)
            cnt = float(min(start_pos + j + 1, w))
            dlt = ws / cnt - row(POOL_HIST + j, lo, hi)
            o_ref[j, :, lo:hi] = _mm(dlt.astype(BF16), pw_ref[g]) * ps_ref[:, lo:hi]


def _pool_sample(st_t, u_t, pw, ps, start_pos, layer):
    t, b, c = u_t.shape
    full = lambda shape: pl.BlockSpec(shape, lambda i: (0,) * len(shape))
    return pl.pallas_call(
        functools.partial(_pool_sample_kernel, start_pos=start_pos),
        grid=(1,),
        in_specs=[full(st_t.shape), full(u_t.shape), _layer_spec(pw, layer), _layer_spec(ps, layer)],
        out_specs=full((t, b, c)),
        out_shape=jax.ShapeDtypeStruct((t, b, c), F32),
        compiler_params=_cparams(1),
        name="pool_sample",
    )(st_t, u_t, pw, ps)


def _load_head_queries(qt_ref, qz_ref):
    tq = qt_ref.shape[2]

    @pl.when((pl.program_id(0) == 0) & (pl.program_id(1) == 0))
    def _():
        qz_ref[...] = jnp.zeros_like(qz_ref)

    for kvh in range(KV_HEADS):
        for g in range(GROUP):
            h = kvh * GROUP + g
            qz_ref[kvh, kvh * HEAD_DIM:(kvh + 1) * HEAD_DIM, g * tq:(g + 1) * tq] = (
                qt_ref[0, h * HEAD_DIM:(h + 1) * HEAD_DIM, :])


def _store_heads_out(o_ref, per_head, tq):
    out_t = jnp.concatenate(
        [per_head[kvh][:, g * tq:(g + 1) * tq] for kvh in range(KV_HEADS) for g in range(GROUP)], axis=0)
    o_ref[0] = out_t.T.astype(o_ref.dtype)


def _att_specs(tiles, tq):
    return [pl.BlockSpec((1, ATT_WIDTH, tq), lambda i, j: (i * tiles + j, 0, 0)),
            pl.BlockSpec((tiles, tq, KV_WIDTH), lambda i, j: (i, 0, 0)),
            pl.BlockSpec((tiles, KV_WIDTH, tq), lambda i, j: (i, 0, 0))]


def _suffix_matrix_t(tk):
    r = np.arange(tk)
    later = (r[None, :] > r[:, None]).astype(np.float32)
    return jnp.asarray(np.concatenate([later, np.ones((BF16_ROWS, tk), np.float32)], axis=0), dtype=BF16)


def _sb_prompt_kernel(qt_ref, k_ref, vt_ref, u_ref, o_ref, qz_ref, acc_ref, carry_ref):
    i = pl.program_id(1)
    tk = k_ref.shape[1]
    q2 = qz_ref.shape[2]
    tq = q2 // GROUP
    _load_head_queries(qt_ref, qz_ref)
    acc_ref[...] = jnp.zeros_like(acc_ref)
    carry_ref[...] = jnp.zeros_like(carry_ref)
    key = lax.broadcasted_iota(jnp.int32, (tk, q2), 0)
    qloc = lax.broadcasted_iota(jnp.int32, (tk, q2), 1) & (tq - 1)
    diag_valid = key < qloc
    u2 = u_ref[...]

    def block(j, valid):
        kb = k_ref[j]
        heads = range(KV_HEADS)
        z = [_mm(kb, qz_ref[kvh]) for kvh in heads]
        drop = [_softplus(z[kvh]) for kvh in heads]
        log_take = [z[kvh] - drop[kvh] for kvh in heads]
        if valid is not None:
            drop = [jnp.where(valid, d, 0.0) for d in drop]
        cs = [_mm(u2, d.astype(BF16)) for d in drop]
        w = [jnp.exp(log_take[kvh] - (cs[kvh][0:tk] + carry_ref[kvh, 0:1, :])) for kvh in heads]
        if valid is not None:
            w = [jnp.where(valid, x, 0.0) for x in w]
        for kvh in heads:
            acc_ref[kvh] += _mm(vt_ref[j, kvh * HEAD_DIM:(kvh + 1) * HEAD_DIM, :], w[kvh].astype(BF16))
            carry_ref[kvh] += cs[kvh][tk:tk + 8]

    block(i, diag_valid)

    def all_dropped():
        return (jnp.min(carry_ref[...]) >= SB_DROP_DONE).astype(jnp.int32)

    def cond(state):
        t, done = state
        return jnp.logical_and(t < i, done == 0)

    def body(state):
        t, _ = state
        block(i - 1 - t, None)
        return t + 1, all_dropped()

    lax.while_loop(cond, body, (jnp.int32(0), all_dropped()))
    _store_heads_out(o_ref, [acc_ref[kvh] for kvh in range(KV_HEADS)], tq)


def _sb_prompt(qt, k3, vt3, b, tiles):
    tq = k3.shape[1]
    q2 = GROUP * tq
    return pl.pallas_call(
        _sb_prompt_kernel,
        grid=(b, tiles),
        in_specs=_att_specs(tiles, tq) + [pl.BlockSpec((tq + BF16_ROWS, tq), lambda i, j: (0, 0))],
        out_specs=pl.BlockSpec((1, tq, ATT_WIDTH), lambda i, j: (i, j, 0)),
        out_shape=jax.ShapeDtypeStruct((b, tiles * tq, ATT_WIDTH), BF16),
        scratch_shapes=[pltpu.VMEM((KV_HEADS, KV_WIDTH, q2), BF16),
                        pltpu.VMEM((KV_HEADS, HEAD_DIM, q2), F32),
                        pltpu.VMEM((KV_HEADS, 8, q2), F32)],
        compiler_params=_cparams(2),
        name="sb_prompt",
    )(qt, k3, vt3, _suffix_matrix_t(tq))


def _page_specs(layer, slots, page, seq, nseq):
    def spec(p):
        return pl.BlockSpec((None, None, KV_WIDTH, page), lambda b, pt: (layer, pt[b * nseq + seq, p], 0, 0))
    return [spec(p) for p in slots]


def _sample_refs(refs, npages, nseq):
    kp = [refs[s * npages:(s + 1) * npages] for s in range(nseq)]
    vp = [refs[(nseq + s) * npages:(nseq + s + 1) * npages] for s in range(nseq)]
    o_ref, padk_ref, padv_ref = refs[2 * nseq * npages:]
    return kp, vp, o_ref, padk_ref, padv_ref


def _load_new_rows(kn_ref, vn_ref, padk_ref, padv_ref, nseq):
    n_new = kn_ref.shape[2]

    @pl.when(pl.program_id(0) == 0)
    def _():
        padk_ref[...] = jnp.zeros_like(padk_ref)
        padv_ref[...] = jnp.zeros_like(padv_ref)

    for s in range(nseq):
        padk_ref[s, 0:n_new, :] = kn_ref[0, s]
        padv_ref[s, 0:n_new, :] = vn_ref[0, s]


def _sample_queries(q8):
    rows = KV_HEADS * q8.shape[0]
    q = jnp.concatenate([q8] * KV_HEADS, axis=0)
    row = lax.broadcasted_iota(jnp.int32, (rows, KV_WIDTH), 0)
    lane = lax.broadcasted_iota(jnp.int32, (rows, KV_WIDTH), 1)
    return jnp.where((row >> 3) == (lane >> 6), q, 0.0)


def _sample_heads_out(acc):
    n = acc.shape[0] // KV_HEADS
    lane_head = lax.broadcasted_iota(jnp.int32, (n, KV_WIDTH), 1) >> 6
    out = jnp.zeros((n, KV_WIDTH), F32)
    for kvh in range(KV_HEADS):
        out = jnp.where(lane_head == kvh, acc[kvh * n:(kvh + 1) * n, :], out)
    return out


def _sample_call(body, name, q4, k_new, v_new, consts, cache_k, cache_v, page_table, layer):
    nsteps, nseq, rows, _ = q4.shape
    n_new = k_new.shape[2]
    npages = page_table.shape[1]
    page = cache_k.shape[3]
    pages = [spec for s in range(nseq) for spec in _page_specs(layer, range(npages), page, s, nseq)]
    grid_spec = pltpu.PrefetchScalarGridSpec(
        num_scalar_prefetch=1,
        grid=(nsteps,),
        in_specs=[pl.BlockSpec((1, nseq, rows, KV_WIDTH), lambda b, pt: (b, 0, 0, 0)),
                  pl.BlockSpec((1, nseq, n_new, KV_WIDTH), lambda b, pt: (b, 0, 0, 0)),
                  pl.BlockSpec((1, nseq, n_new, KV_WIDTH), lambda b, pt: (b, 0, 0, 0))]
                 + [pl.BlockSpec(c.shape, lambda b, pt: (0, 0)) for c in consts] + pages + pages,
        out_specs=pl.BlockSpec((1, nseq, rows, KV_WIDTH), lambda b, pt: (b, 0, 0, 0)),
        scratch_shapes=[pltpu.VMEM((nseq, page, KV_WIDTH), F32), pltpu.VMEM((nseq, page, KV_WIDTH), F32)])
    return pl.pallas_call(
        functools.partial(body, npages=npages, nseq=nseq),
        grid_spec=grid_spec,
        out_shape=jax.ShapeDtypeStruct(q4.shape, F32),
        compiler_params=_cparams(1),
        name=name,
    )(page_table, q4, k_new, v_new, *consts, *([cache_k] * (nseq * npages)), *([cache_v] * (nseq * npages)))


def _sb_sample_blocks(qz, kt, vt, new_kv, valid_new, umat, acc, carry):
    nseq, ncached = len(kt), len(kt[0])
    nblk = ncached + (new_kv is not None)
    rows, page = carry[0].shape
    work = [(s, i) for s in range(nseq) for i in range(nblk)]
    z = [_mm(qz[s], kt[s][i][...].astype(BF16)) if i < ncached else _mm_nt(qz[s], new_kv[s][0].astype(BF16))
         for s, i in work]
    drop = [_softplus(x) for x in z]
    take = [x - d for x, d in zip(z, drop)]
    drop = [d if i < ncached else jnp.where(valid_new, d, 0.0) for d, (s, i) in zip(drop, work)]
    parts = []
    for d in drop:
        parts.extend(_split2(d))
    cs = _mm(jnp.concatenate(parts, axis=0), umat)
    acc, carry = list(acc), list(carry)
    for i in reversed(range(nblk)):
        for s in range(nseq):
            j = work.index((s, i))
            c = cs[j * 2 * rows:(j + 1) * 2 * rows]
            both = c[0:rows] + c[rows:2 * rows]
            w = jnp.exp(take[j] - (both[:, 0:page] + carry[s]))
            if i < ncached:
                acc[s] = acc[s] + _mm_nt(w.astype(BF16), vt[s][i][...].astype(BF16))
            else:
                w = jnp.where(valid_new, w, 0.0)
                acc[s] = acc[s] + _mm(w.astype(BF16), new_kv[s][1].astype(BF16))
            carry[s] = carry[s] + both[:, page:]
    return acc, carry


def _sb_sample_head_kernel(pt_ref, q_ref, kn_ref, vn_ref, u_ref, *refs, npages, nseq):
    del pt_ref
    kp = [refs[s * npages:(s + 1) * npages] for s in range(nseq)]
    vp = [refs[(nseq + s) * npages:(nseq + s + 1) * npages] for s in range(nseq)]
    acc_ref, carry_ref, padk_ref, padv_ref = refs[2 * nseq * npages:]
    _load_new_rows(kn_ref, vn_ref, padk_ref, padv_ref, nseq)
    n_new = kn_ref.shape[2]
    page = u_ref.shape[0]
    rows = KV_HEADS * q_ref.shape[2]
    row = lax.broadcasted_iota(jnp.int32, (rows, page), 0)
    col = lax.broadcasted_iota(jnp.int32, (rows, page), 1)
    valid_new = col < (row & (n_new - 1))
    qz = [_sample_queries(q_ref[0, s]).astype(BF16) for s in range(nseq)]
    acc, carry = _sb_sample_blocks(
        qz, kp, vp, [(padk_ref[s], padv_ref[s]) for s in range(nseq)], valid_new, u_ref[...],
        [jnp.zeros((rows, KV_WIDTH), F32)] * nseq, [jnp.zeros((rows, page), F32)] * nseq)
    for s in range(nseq):
        acc_ref[0, s] = acc[s]
        carry_ref[0, s] = carry[s]


def _sb_sample_tail_kernel(pt_ref, q_ref, acc_in_ref, carry_in_ref, u_ref, *refs, npages, nseq):
    del pt_ref
    kp = [refs[s * npages:(s + 1) * npages] for s in range(nseq)]
    vp = [refs[(nseq + s) * npages:(nseq + s + 1) * npages] for s in range(nseq)]
    acc_ref = refs[2 * nseq * npages]
    qz = [_sample_queries(q_ref[0, s]).astype(BF16) for s in range(nseq)]
    acc, _ = _sb_sample_blocks(qz, kp, vp, None, None, u_ref[...],
                               [acc_in_ref[0, s] for s in range(nseq)], [carry_in_ref[0, s] for s in range(nseq)])
    for s in range(nseq):
        acc_ref[0, s] = acc[s]


def _sb_sample(q4, k_new, v_new, umat, cache_k, cache_v, page_table, layer, n_eager):
    nsteps, nseq, n, _ = q4.shape
    rows = KV_HEADS * n
    n_new = k_new.shape[2]
    npages = page_table.shape[1]
    page = cache_k.shape[3]
    n_eager = min(n_eager, npages)
    step4 = lambda b, pt: (b, 0, 0, 0)
    q_spec = pl.BlockSpec((1, nseq, n, KV_WIDTH), step4)
    new_spec = pl.BlockSpec((1, nseq, n_new, KV_WIDTH), step4)
    acc_spec = pl.BlockSpec((1, nseq, rows, KV_WIDTH), step4)
    carry_spec = pl.BlockSpec((1, nseq, rows, page), step4)
    u_spec = pl.BlockSpec(umat.shape, lambda b, pt: (0, 0))
    acc_shape = jax.ShapeDtypeStruct((nsteps, nseq, rows, KV_WIDTH), F32)
    carry_shape = jax.ShapeDtypeStruct((nsteps, nseq, rows, page), F32)

    def page_operands(slots):
        specs = [spec for s in range(nseq) for spec in _page_specs(layer, slots, page, s, nseq)]
        count = nseq * len(slots)
        return specs + specs, [cache_k] * count + [cache_v] * count

    specs, operands = page_operands(range(npages - n_eager, npages))
    acc, carry = pl.pallas_call(
        functools.partial(_sb_sample_head_kernel, npages=n_eager, nseq=nseq),
        grid_spec=pltpu.PrefetchScalarGridSpec(
            num_scalar_prefetch=1, grid=(nsteps,),
            in_specs=[q_spec, new_spec, new_spec, u_spec] + specs,
            out_specs=[acc_spec, carry_spec],
            scratch_shapes=[pltpu.VMEM((nseq, page, KV_WIDTH), F32), pltpu.VMEM((nseq, page, KV_WIDTH), F32)]),
        out_shape=[acc_shape, carry_shape],
        compiler_params=_cparams(1),
        name="sb_sample_head",
    )(page_table, q4, k_new, v_new, umat, *operands)
    if npages > n_eager:
        specs, operands = page_operands(range(npages - n_eager))

        def tail(acc_in):
            return pl.pallas_call(
                functools.partial(_sb_sample_tail_kernel, npages=npages - n_eager, nseq=nseq),
                grid_spec=pltpu.PrefetchScalarGridSpec(
                    num_scalar_prefetch=1, grid=(nsteps,),
                    in_specs=[q_spec, acc_spec, carry_spec, u_spec] + specs,
                    out_specs=acc_spec),
                out_shape=acc_shape,
                compiler_params=_cparams(1),
                name="sb_sample_tail",
            )(page_table, q4, acc_in, carry, umat, *operands)

        acc = lax.cond(jnp.min(carry) < SB_DROP_DONE, tail, lambda acc_in: acc_in, acc)
    a6 = acc.reshape(nsteps, nseq, KV_HEADS, n, KV_HEADS, HEAD_DIM)
    return jnp.stack([a6[:, :, h, :, h, :] for h in range(KV_HEADS)], axis=3).reshape(nsteps, nseq, n, KV_WIDTH)


def _suffix_matrix(tk):
    r = np.arange(tk)
    strict = (r[:, None] > r[None, :]).astype(np.float32)
    return jnp.asarray(np.concatenate([strict, np.ones((tk, tk), np.float32)], axis=1), dtype=BF16)


def _topk_rows(scores, n_valid, nb):
    blk = lax.broadcasted_iota(jnp.int32, scores.shape, 0)
    valid = blk < n_valid
    s = jnp.where(valid, scores, NEG_INF)
    rank = jnp.zeros(scores.shape, jnp.int32)
    for m in range(nb):
        other = s[m:m + 1, :]
        beats = (other > s) | ((other == s) & (blk > m))
        rank = rank + jnp.where(beats, 1, 0)
    return jnp.where(valid & (rank < MOBA_TOPK), 1.0, 0.0)


def _topk_lanes(scores, n_valid, nb):
    blk = lax.broadcasted_iota(jnp.int32, scores.shape, 1)
    valid = blk < n_valid
    s = jnp.where(valid, scores, NEG_INF)
    rank = jnp.zeros(scores.shape, jnp.int32)
    for m in range(nb):
        other = s[:, m:m + 1]
        beats = (other > s) | ((other == s) & (blk > m))
        rank = rank + jnp.where(beats, 1, 0)
    return jnp.where(valid & (rank < MOBA_TOPK), 1.0, 0.0)


def _moba_prompt_kernel(qtf_ref, qt_ref, k_ref, vt_ref, km_ref, o_ref,
                        qz_ref, sel_ref, m_ref, l_ref, acc_ref, *, nb):
    qblk = pl.program_id(1)
    tk = k_ref.shape[1]
    q2 = qz_ref.shape[2]
    tq = q2 // GROUP
    _load_head_queries(qt_ref, qz_ref)
    heads = range(KV_HEADS)
    qf = jnp.concatenate(
        [jnp.concatenate([qtf_ref[0, (kvh * GROUP + g) * HEAD_DIM:(kvh * GROUP + g + 1) * HEAD_DIM, :]
                          for kvh in heads], axis=0) for g in range(GROUP)], axis=1)
    q_hi, q_lo = _split2(qf)
    km = km_ref[0]
    lane_head = lax.broadcasted_iota(jnp.int32, km.shape, 1) >> 6
    km_parts = [_split2(jnp.where(lane_head == kvh, km, 0.0)) for kvh in heads]
    scores = [_mm(km_hi, q_hi) + (_mm(km_hi, q_lo) + _mm(km_lo, q_hi)) for km_hi, km_lo in km_parts]
    for kvh in heads:
        sel_ref[kvh] = _topk_rows(scores[kvh], qblk, nb)
    key = lax.broadcasted_iota(jnp.int32, (tk, q2), 0)
    qloc = lax.broadcasted_iota(jnp.int32, (tk, q2), 1) & (tq - 1)
    causal = key <= qloc
    kb = k_ref[qblk]
    st = [jnp.where(causal, _mm(kb, qz_ref[kvh]), NEG_INF) for kvh in heads]
    m0 = [jnp.max(s, axis=0, keepdims=True) for s in st]
    p = [jnp.exp(st[kvh] - m0[kvh]) for kvh in heads]
    for kvh in heads:
        m_ref[kvh] = m0[kvh]
        l_ref[kvh] = jnp.sum(p[kvh], axis=0, keepdims=True)
        acc_ref[kvh] = _mm(vt_ref[qblk, kvh * HEAD_DIM:(kvh + 1) * HEAD_DIM, :], p[kvh].astype(BF16))

    def body(n, c):
        kb = k_ref[n]
        bias = [jnp.where(sel_ref[kvh, pl.ds(n, 1), :] > 0.5, 0.0, NEG_INF) for kvh in heads]
        st = [_mm(kb, qz_ref[kvh]) for kvh in heads]
        m_old = [m_ref[kvh] for kvh in heads]
        m_new = [jnp.maximum(m_old[kvh], jnp.max(st[kvh], axis=0, keepdims=True) + bias[kvh]) for kvh in heads]
        p = [jnp.exp(st[kvh] - (m_new[kvh] - bias[kvh])) for kvh in heads]
        alpha = [jnp.exp(m_old[kvh] - m_new[kvh]) for kvh in heads]
        pv = [_mm(vt_ref[n, kvh * HEAD_DIM:(kvh + 1) * HEAD_DIM, :], p[kvh].astype(BF16)) for kvh in heads]
        for kvh in heads:
            l_ref[kvh] = alpha[kvh] * l_ref[kvh] + jnp.sum(p[kvh], axis=0, keepdims=True)
            acc_ref[kvh] = alpha[kvh] * acc_ref[kvh] + pv[kvh]
            m_ref[kvh] = m_new[kvh]
        return c

    lax.fori_loop(0, qblk, body, 0)
    _store_heads_out(o_ref, [acc_ref[kvh] / l_ref[kvh] for kvh in heads], tq)


def _moba_prompt(qtf, qt, k3, vt3, kmean, b, tiles):
    tq = k3.shape[1]
    q2 = GROUP * tq
    return pl.pallas_call(
        functools.partial(_moba_prompt_kernel, nb=tiles),
        grid=(b, tiles),
        in_specs=[pl.BlockSpec((1, ATT_WIDTH, tq), lambda i, j: (i * tiles + j, 0, 0))] + _att_specs(tiles, tq)
                 + [pl.BlockSpec((1, tiles, KV_WIDTH), lambda i, j: (i, 0, 0))],
        out_specs=pl.BlockSpec((1, tq, ATT_WIDTH), lambda i, j: (i, j, 0)),
        out_shape=jax.ShapeDtypeStruct((b, tiles * tq, ATT_WIDTH), BF16),
        scratch_shapes=[pltpu.VMEM((KV_HEADS, KV_WIDTH, q2), BF16), pltpu.VMEM((KV_HEADS, tiles, q2), F32),
                        pltpu.VMEM((KV_HEADS, 1, q2), F32), pltpu.VMEM((KV_HEADS, 1, q2), F32),
                        pltpu.VMEM((KV_HEADS, HEAD_DIM, q2), F32)],
        compiler_params=_cparams(2),
        name="moba_prompt",
    )(qtf, qt, k3, vt3, kmean)


def _moba_sample_kernel(pt_ref, q_ref, kn_ref, vn_ref, *refs, npages, nseq):
    del pt_ref
    kp, vp, o_ref, padk_ref, padv_ref = _sample_refs(refs, npages, nseq)
    _load_new_rows(kn_ref, vn_ref, padk_ref, padv_ref, nseq)
    n_new = kn_ref.shape[2]
    page = padk_ref.shape[1]
    per_blk = MOBA_BLOCK // page
    nb_past = npages // per_blk
    rows = KV_HEADS * q_ref.shape[2]
    seqs = range(nseq)
    qf = [_sample_queries(q_ref[0, s]) for s in seqs]
    qz = [(q * ATT_SCALE).astype(BF16) for q in qf]
    ones = jnp.ones((page, LANES), BF16)
    lane = lax.broadcasted_iota(jnp.int32, (KV_WIDTH, LANES), 1)
    k_parts = [[_split2(kp[s][pg][...]) for pg in range(npages)] for s in seqs]
    k_sums = [[_mm(hi, ones) + _mm(lo, ones) for hi, lo in k_parts[s]] for s in seqs]
    km = []
    for s in seqs:
        km_s = jnp.zeros((KV_WIDTH, LANES), F32)
        for blk in range(nb_past):
            tot = k_sums[s][blk * per_blk]
            for r in range(1, per_blk):
                tot = tot + k_sums[s][blk * per_blk + r]
            km_s = jnp.where(lane == blk, tot * (1.0 / MOBA_BLOCK), km_s)
        km.append(km_s)
    sel = []
    for s in seqs:
        q_hi, q_lo = _split2(qf[s])
        km_hi, km_lo = _split2(km[s])
        scores = _mm(q_hi, km_hi) + (_mm(q_hi, km_lo) + _mm(q_lo, km_hi))
        sel.append(_topk_lanes(scores, nb_past, nb_past))
    row = lax.broadcasted_iota(jnp.int32, (rows, page), 0)
    col = lax.broadcasted_iota(jnp.int32, (rows, page), 1)
    visible = (col <= (row & (n_new - 1))) & (col < n_new)
    logits = [[jnp.where(sel[s][:, pg // per_blk:pg // per_blk + 1] > 0.5,
                         _mm(qz[s], kp[s][pg][...].astype(BF16)), NEG_INF) for pg in range(npages)]
              + [jnp.where(visible, _mm_nt(qz[s], padk_ref[s].astype(BF16)), NEG_INF)] for s in seqs]
    m = []
    for s in seqs:
        m_el = logits[s][0]
        for x in logits[s][1:]:
            m_el = jnp.maximum(m_el, x)
        m.append(jnp.max(m_el, axis=1, keepdims=True))
    acc = [jnp.zeros((rows, KV_WIDTH), F32) for _ in seqs]
    l_el = [jnp.zeros((rows, page), F32) for _ in seqs]
    for pg in range(npages + 1):
        for s in seqs:
            p = jnp.exp(logits[s][pg] - m[s])
            l_el[s] = l_el[s] + p
            if pg < npages:
                acc[s] = acc[s] + _mm_nt(p.astype(BF16), vp[s][pg][...].astype(BF16))
            else:
                acc[s] = acc[s] + _mm(p.astype(BF16), padv_ref[s].astype(BF16))
    for s in seqs:
        o_ref[0, s] = _sample_heads_out(acc[s] / jnp.sum(l_el[s], axis=1, keepdims=True))


def _merge_kernel(x_ref, a_ref, b_ref, c_ref, g_ref, wb_ref, wo_ref, o_ref):
    d = x_ref.shape[1]
    m = g_ref[:, 0:d] * _mm(a_ref[...].astype(BF16), wb_ref[0])
    m = m + g_ref[:, d:2 * d] * _mm(b_ref[...].astype(BF16), wb_ref[1])
    m = m + g_ref[:, 2 * d:3 * d] * _mm(c_ref[...].astype(BF16), wb_ref[2])
    o_ref[...] = x_ref[...] + _mm(m.astype(BF16), wo_ref[...])


def _merge(x, a, b, c, gate, wb, wo, tm, layer):
    n, d = x.shape
    row = lambda i: (i, 0)
    return pl.pallas_call(
        _merge_kernel,
        grid=(n // tm,),
        in_specs=[pl.BlockSpec((tm, d), row), pl.BlockSpec((tm, ATT_WIDTH), row), pl.BlockSpec((tm, ATT_WIDTH), row),
                  pl.BlockSpec((tm, ATT_WIDTH), row), pl.BlockSpec((tm, N_BRANCH * d), row),
                  _layer_spec(wb, layer), _layer_spec(wo, layer)],
        out_specs=pl.BlockSpec((tm, d), row),
        out_shape=jax.ShapeDtypeStruct((n, d), F32),
        compiler_params=_cparams(1, VMEM_LIMIT),
        name="merge",
    )(x, a, b, c, gate, wb, wo)


def _mlp_ple_kernel(x_ref, p_ref, gm_ref, wu_ref, wd_ref, gp_ref, wg_ref, wp_ref, o_ref, *, f_chunk):
    x = x_ref[...]
    ms = jnp.mean(x * x, axis=-1, keepdims=True)
    h = (x * lax.rsqrt(ms + RMS_EPS) * gm_ref[...]).astype(BF16)
    d_ff = wu_ref.shape[1]
    y = x
    for f0 in range(0, d_ff, f_chunk):
        hid = jnp.maximum(_mm(h, wu_ref[:, f0:f0 + f_chunk]), 0.0)
        y = y + _mm((hid * hid).astype(BF16), wd_ref[f0:f0 + f_chunk, :])
    ms2 = jnp.mean(y * y, axis=-1, keepdims=True)
    h2 = (y * lax.rsqrt(ms2 + RMS_EPS) * gp_ref[...]).astype(BF16)
    gate = _sigmoid(_mm(h2, wg_ref[...]))
    o_ref[...] = y + gate * _mm(p_ref[...].astype(BF16), wp_ref[...])


def _mlp_ple(x, p, gm, wu, wd, gp, wg, wp, tm, layer):
    n, d = x.shape
    row = lambda i: (i, 0)
    return pl.pallas_call(
        functools.partial(_mlp_ple_kernel, f_chunk=512),
        grid=(n // tm,),
        in_specs=[pl.BlockSpec((tm, d), row), pl.BlockSpec((None, tm, p.shape[2]), lambda i: (layer, i, 0)),
                  _layer_spec(gm, layer), _layer_spec(wu, layer), _layer_spec(wd, layer),
                  _layer_spec(gp, layer), _layer_spec(wg, layer), _layer_spec(wp, layer)],
        out_specs=pl.BlockSpec((tm, d), row),
        out_shape=jax.ShapeDtypeStruct((n, d), F32),
        compiler_params=_cparams(1, VMEM_LIMIT),
        name="mlp_ple",
    )(x, p, gm, wu, wd, gp, wg, wp)


def _rope_tables(pos):
    half = ROT_DIM // 2
    inv = ROPE_THETA ** (-jnp.arange(half, dtype=F32) / half)
    ang = pos.astype(F32)[:, None] * inv[None, :]
    cos, sin = jnp.cos(ang), jnp.sin(ang)
    t = pos.shape[0]
    c64 = jnp.concatenate([cos, cos, jnp.ones((t, HEAD_DIM - ROT_DIM), F32)], axis=1)
    s64 = jnp.concatenate([-sin, sin, jnp.zeros((t, HEAD_DIM - ROT_DIM), F32)], axis=1)
    reps = LANES // HEAD_DIM
    return jnp.tile(c64, (1, reps)), jnp.tile(s64, (1, reps))


def _head_group_matrix():
    r = np.arange(ATT_WIDTH) // HEAD_DIM
    return jnp.asarray((r[:, None] == r[None, :]).astype(np.float32), dtype=BF16)


def kernel(x_prompt, x_sample, p_prompt, p_sample, cache_sb_k, cache_sb_v, cache_moba_k, cache_moba_v,
           state_pool, page_table, norm_mix, w_in, pool_w, pool_scale, moba_q_norm, moba_k_norm,
           w_branch, w_out, norm_mlp, w_up, w_down, norm_ple, w_ple_gate, w_ple):
    depth = w_in.shape[0]
    bp, seq, d = x_prompt.shape
    bs, t_new, _ = x_sample.shape
    n_p, n_s = bp * seq, bs * t_new
    n_phys, page = cache_sb_k.shape[1], cache_sb_k.shape[2]
    past_len = page_table.shape[1] * page
    tiles = seq // ATT_TILE
    assert seq % ATT_TILE == 0 and past_len % MOBA_BLOCK == 0 and MOBA_BLOCK % page == 0 and page == LANES
    assert t_new & (t_new - 1) == 0 and GROUP * t_new == 8 and past_len >= POOL_HIST >= t_new
    assert past_len // MOBA_BLOCK <= LANES

    w_in_b, w_branch_b = w_in.astype(BF16), w_branch.astype(BF16)
    w_out_b, w_up_b, w_down_b = w_out.astype(BF16), w_up.astype(BF16), w_down.astype(BF16)
    w_pg_b, w_ple_b, pool_w_b = w_ple_gate.astype(BF16), w_ple.astype(BF16), pool_w.astype(BF16)
    vec = lambda a: a[:, None, :]
    norm_mix3, norm_mlp3, norm_ple3, pool_scale3 = vec(norm_mix), vec(norm_mlp), vec(norm_ple), vec(pool_scale)
    qn = vec(jnp.tile(moba_q_norm, (1, N_HEADS)))
    kn = vec(jnp.tile(moba_k_norm, (1, KV_HEADS)))
    gmat = _head_group_matrix()

    tm_p = ATT_TILE
    tm_s = 256 if n_s % 256 == 0 else n_s
    cos_p, sin_p = _rope_tables(jnp.arange(seq))
    cos_s, sin_s = _rope_tables(jnp.tile(past_len + jnp.arange(t_new), bs))
    caches = [c.transpose(0, 1, 3, 4, 2).reshape(depth, n_phys, KV_WIDTH, page)
              for c in (cache_sb_k, cache_sb_v, cache_moba_k, cache_moba_v)]
    pp = p_prompt.reshape(depth, n_p, -1)
    psm = p_sample.reshape(depth, n_s, -1)

    xp = x_prompt.reshape(n_p, d)
    xs = x_sample.reshape(n_s, d)
    kv_stacks = None
    stack_shape = (depth, bp, KV_WIDTH, seq)
    pool_p = []
    outs_s = [[] for _ in range(5)]
    nseq = 2 if bs % 2 == 0 else 1

    def to_q4(q):
        return q.reshape(bs, t_new, KV_HEADS, GROUP, HEAD_DIM).transpose(0, 3, 1, 2, 4).reshape(
            bs // nseq, nseq, GROUP * t_new, KV_WIDTH)

    def from_q4(r):
        return r.reshape(bs, GROUP, t_new, KV_HEADS, HEAD_DIM).transpose(0, 2, 3, 1, 4).reshape(n_s, ATT_WIDTH)

    def new_rows(a):
        return a.reshape(bs // nseq, nseq, t_new, KV_WIDTH)

    sb_umat = _suffix_matrix(page)

    for i in range(depth):
        layer_w = (norm_mix3, w_in_b, qn, kn)
        post_w = (norm_mlp3, w_up_b, w_down_b, norm_ple3, w_pg_b, w_ple_b)

        (u, qt_sb, k_sb_b, vt_sb, qt_mf, qt_mb, k_mb_b, vt_mb, kmean, gate, *kv_stacks) = _inproj(
            xp, *layer_w, cos_p, sin_p, gmat, tm_p, i, kv_stacks, stack_shape)
        k3 = lambda a: a.reshape(n_p // ATT_TILE, ATT_TILE, KV_WIDTH)
        u3 = u.reshape(bp, seq, ATT_WIDTH)
        a = _pool_prompt(u3, pool_w_b, pool_scale3, min(512, seq), i).reshape(n_p, ATT_WIDTH)
        b = _sb_prompt(qt_sb, k3(k_sb_b), vt_sb, bp, tiles).reshape(n_p, ATT_WIDTH)
        c = _moba_prompt(qt_mf, qt_mb, k3(k_mb_b), vt_mb, kmean.reshape(bp, tiles, KV_WIDTH), bp, tiles).reshape(
            n_p, ATT_WIDTH)
        xp = _merge(xp, a, b, c, gate, w_branch_b, w_out_b, tm_p, i)
        xp = _mlp_ple(xp, pp, *post_w, tm_p, i)
        pool_p.append(u3[:, seq - POOL_HIST:])

        (u, q_sb, k_sb, v_sb, q_mb, k_mb, v_mb, gate) = _inproj(xs, *layer_w, cos_s, sin_s, gmat, tm_s, i)
        r3 = lambda a: a.reshape(bs, t_new, a.shape[-1])
        st_t = state_pool[i].transpose(1, 0, 2)
        a = _pool_sample(st_t, r3(u).transpose(1, 0, 2), pool_w_b, pool_scale3, past_len, i).transpose(1, 0, 2).reshape(
            n_s, ATT_WIDTH)
        b = from_q4(_sb_sample(to_q4(q_sb), new_rows(k_sb), new_rows(v_sb), sb_umat, caches[0], caches[1],
                               page_table, i, SB_EAGER_PAGES))
        c = from_q4(_sample_call(_moba_sample_kernel, "moba_sample", to_q4(q_mb), new_rows(k_mb), new_rows(v_mb), [],
                                 caches[2], caches[3], page_table, i))
        xs = _merge(xs, a, b, c, gate, w_branch_b, w_out_b, tm_s, i)
        xs = _mlp_ple(xs, psm, *post_w, tm_s, i)
        pool_state = jnp.concatenate([state_pool[i][:, t_new:], r3(u)], axis=1)
        for lst, val in zip(outs_s, (k_sb, v_sb, k_mb, v_mb, pool_state)):
            lst.append(val)

    def kv_prompt(stack):
        return stack.reshape(depth, bp, KV_HEADS, HEAD_DIM, seq).transpose(0, 1, 4, 2, 3)

    def kv_sample(lst):
        return jnp.stack(lst).reshape(depth, bs, t_new, KV_HEADS, HEAD_DIM)

    return (xp.reshape(bp, seq, d), xs.reshape(bs, t_new, d),
            kv_prompt(kv_stacks[0]), kv_prompt(kv_stacks[1]), kv_prompt(kv_stacks[2]), kv_prompt(kv_stacks[3]),
            jnp.stack(pool_p),
            kv_sample(outs_s[0]), kv_sample(outs_s[1]), kv_sample(outs_s[2]), kv_sample(outs_s[3]),
            jnp.stack(outs_s[4]))
```

```python
import functools

import jax
import jax.numpy as jnp
import numpy as np
from jax import lax
from jax.experimental import pallas as pl
from jax.experimental.pallas import tpu as pltpu

F32 = jnp.float32
BF16 = jnp.bfloat16

HEAD_DIM = 64
KV_HEADS = 4
GROUP = 2
N_HEADS = KV_HEADS * GROUP
ATT_WIDTH = N_HEADS * HEAD_DIM
KV_WIDTH = KV_HEADS * HEAD_DIM
ATT_SCALE = HEAD_DIM ** -0.5
LOG2_E = 1.4426950408889634
POOL_WINDOWS = (2, 4, 8, 16)
POOL_GROUP = 128
POOL_HIST = max(POOL_WINDOWS) - 1
ROPE_THETA = 500000.0
ROT_DIM = HEAD_DIM // 4
MOBA_BLOCK = 256
MOBA_TOPK = 3
N_BRANCH = 3
RMS_EPS = 1e-6
NEG_INF = float("-inf")
SB_DROP_DONE = 104.0
SB_EAGER_PAGES = 3

LANES = 128
BF16_ROWS = 16
ATT_TILE = MOBA_BLOCK
VMEM_LIMIT = 56 * 1024 * 1024


def _cparams(n_axes, vmem=None):
    return pltpu.CompilerParams(dimension_semantics=("arbitrary",) * n_axes, vmem_limit_bytes=vmem)


def _layer_spec(arr, layer):
    zeros = (0,) * (arr.ndim - 1)
    return pl.BlockSpec((None,) + arr.shape[1:], lambda *_: (layer,) + zeros)


def _mm(a, b):
    return jnp.dot(a, b, preferred_element_type=F32)


def _mm_nt(a, b):
    return lax.dot_general(a, b, (((1,), (1,)), ((), ())), preferred_element_type=F32)


def _split2(x):
    hi = x.astype(BF16)
    lo = (x - hi.astype(F32)).astype(BF16)
    return hi, lo


def _sigmoid(x):
    return 1.0 / (1.0 + jnp.exp(-x))


def _softplus(z):
    return jnp.maximum(z, 0.0) + jnp.log(1.0 + jnp.exp(jnp.minimum(z, -z)))


def _inproj_kernel(x_ref, gn_ref, w_ref, qn_ref, kn_ref, cos_ref, sin_ref, gmat_ref, *refs, prompt, n_alias):
    out_refs = refs[n_alias:]
    x = x_ref[...]
    ms = jnp.mean(x * x, axis=-1, keepdims=True)
    h = (x * lax.rsqrt(ms + RMS_EPS) * gn_ref[...]).astype(BF16)

    def proj(lo, hi):
        return _mm(h, w_ref[:, lo:hi])

    def head_norm_rope(y, g_ref):
        width = y.shape[1]
        gm = gmat_ref[0:width, 0:width]
        p0, p1 = _split2(y * y)
        ms_h = (_mm(p0, gm) + _mm(p1, gm)) * (1.0 / HEAD_DIM)
        yn = y * lax.rsqrt(ms_h + RMS_EPS) * g_ref[...]
        reps = width // LANES
        c = jnp.concatenate([cos_ref[...]] * reps, axis=1)
        s = jnp.concatenate([sin_ref[...]] * reps, axis=1)
        lane = lax.broadcasted_iota(jnp.int32, yn.shape, 1)
        first_half = (lane & (HEAD_DIM - 1)) < (ROT_DIM // 2)
        partner = jnp.where(first_half, pltpu.roll(yn, width - ROT_DIM // 2, 1), pltpu.roll(yn, ROT_DIM // 2, 1))
        return yn * c + partner * s

    o = 0
    u = proj(o, o + ATT_WIDTH)
    o += ATT_WIDTH
    q_sb = proj(o, o + ATT_WIDTH) * ATT_SCALE
    o += ATT_WIDTH
    k_sb = proj(o, o + KV_WIDTH)
    o += KV_WIDTH
    v_sb = proj(o, o + KV_WIDTH)
    o += KV_WIDTH
    q_mb = head_norm_rope(proj(o, o + ATT_WIDTH), qn_ref)
    o += ATT_WIDTH
    k_mb = head_norm_rope(proj(o, o + KV_WIDTH), kn_ref)
    o += KV_WIDTH
    v_mb = proj(o, o + KV_WIDTH)
    o += KV_WIDTH
    if prompt:
        (u_ref, qsb_t, ksb_b, vsb_tb, qmb_tf, qmb_t, kmb_b, vmb_tb, kmean_ref, gate_ref,
         ksb_t, vsb_t, kmb_t, vmb_t) = out_refs
        qsb_t[0] = q_sb.T.astype(BF16)
        ksb_b[...] = k_sb.astype(BF16)
        ksb_t[...] = k_sb.T
        v_t = v_sb.T
        vsb_t[...] = v_t
        vsb_tb[0] = v_t.astype(BF16)
        q_mb_t = q_mb.T
        qmb_tf[0] = q_mb_t
        qmb_t[0] = (q_mb_t * (ATT_SCALE * LOG2_E)).astype(BF16)
        kmb_b[...] = k_mb.astype(BF16)
        kmb_t[...] = k_mb.T
        kmean_ref[0] = jnp.sum(k_mb, axis=0, keepdims=True) * (1.0 / k_mb.shape[0])
        v_t = v_mb.T
        vmb_t[...] = v_t
        vmb_tb[0] = v_t.astype(BF16)
    else:
        (u_ref, qsb_ref, ksb_ref, vsb_ref, qmb_ref, kmb_ref, vmb_ref, gate_ref) = out_refs
        qsb_ref[...] = q_sb
        qmb_ref[...] = q_mb
        ksb_ref[...] = k_sb
        vsb_ref[...] = v_sb
        kmb_ref[...] = k_mb
        vmb_ref[...] = v_mb
    u_ref[...] = u
    gate_w = gate_ref.shape[1]
    chunk = 512
    for c0 in range(0, gate_w, chunk):
        gate_ref[:, c0:c0 + chunk] = _sigmoid(proj(o + c0, o + c0 + chunk)).astype(gate_ref.dtype)


def _inproj(x, gn, w, qn, kn, cos_t, sin_t, gmat, tm, layer, kv_stacks=None, stack_shape=None):
    prompt = stack_shape is not None
    n, d = x.shape
    in_w = w.shape[2]
    gate_w = in_w - (3 * ATT_WIDTH + 4 * KV_WIDTH)
    nper = cos_t.shape[0] // tm
    nt = n // tm
    row = lambda i: (i, 0)
    const = lambda i: (0, 0)

    def flat(width, dtype):
        return jax.ShapeDtypeStruct((n, width), dtype), pl.BlockSpec((tm, width), row)

    def transposed(width, dtype):
        return jax.ShapeDtypeStruct((nt, width, tm), dtype), pl.BlockSpec((1, width, tm), lambda i: (i, 0, 0))

    alias_in, aliases = [], {}
    if prompt:
        tiles = stack_shape[3] // tm
        stacked = (jax.ShapeDtypeStruct(stack_shape, F32),
                   pl.BlockSpec((None, None, KV_WIDTH, tm), lambda i: (layer, i // tiles, 0, i % tiles)))
        outs = [flat(ATT_WIDTH, F32), transposed(ATT_WIDTH, BF16), flat(KV_WIDTH, BF16), transposed(KV_WIDTH, BF16),
                transposed(ATT_WIDTH, F32), transposed(ATT_WIDTH, BF16), flat(KV_WIDTH, BF16),
                transposed(KV_WIDTH, BF16),
                (jax.ShapeDtypeStruct((nt, 1, KV_WIDTH), F32), pl.BlockSpec((1, 1, KV_WIDTH), lambda i: (i, 0, 0))),
                flat(gate_w, BF16), stacked, stacked, stacked, stacked]
        if kv_stacks is not None:
            alias_in = list(kv_stacks)
            aliases = {8 + k: len(outs) - 4 + k for k in range(4)}
    else:
        outs = [flat(ATT_WIDTH, F32), flat(ATT_WIDTH, F32), flat(KV_WIDTH, F32), flat(KV_WIDTH, F32),
                flat(ATT_WIDTH, F32), flat(KV_WIDTH, F32), flat(KV_WIDTH, F32), flat(gate_w, BF16)]
    return pl.pallas_call(
        functools.partial(_inproj_kernel, prompt=prompt, n_alias=len(alias_in)),
        grid=(nt,),
        in_specs=[pl.BlockSpec((tm, d), row), _layer_spec(gn, layer), _layer_spec(w, layer),
                  _layer_spec(qn, layer), _layer_spec(kn, layer),
                  pl.BlockSpec((tm, LANES), lambda i: (i % nper, 0)), pl.BlockSpec((tm, LANES), lambda i: (i % nper, 0)),
                  pl.BlockSpec((ATT_WIDTH, ATT_WIDTH), const)]
                 + [pl.BlockSpec(memory_space=pl.ANY)] * len(alias_in),
        out_specs=[s for _, s in outs],
        out_shape=[s for s, _ in outs],
        input_output_aliases=aliases,
        compiler_params=_cparams(1, VMEM_LIMIT),
        name="inproj_prompt" if prompt else "inproj_sample",
    )(x, gn, w, qn, kn, cos_t, sin_t, gmat, *alias_in)


def _pool_prompt_kernel(cur_ref, prev_ref, pw_ref, ps_ref, o_ref, ext_ref, *, ts):
    t = pl.program_id(1)
    hist = ext_ref.shape[0] - ts
    ext_ref[0:hist, :] = jnp.where(t == 0, 0.0, prev_ref[0])
    ext_ref[hist:, :] = cur_ref[0]
    pos = t * ts + lax.broadcasted_iota(jnp.int32, (ts, POOL_GROUP), 0)
    for g, w in enumerate(POOL_WINDOWS):
        lo, hi = g * POOL_GROUP, (g + 1) * POOL_GROUP
        ws = ext_ref[hist:hist + ts, lo:hi]
        for k in range(1, w):
            ws = ws + ext_ref[hist - k:hist - k + ts, lo:hi]
        cnt = jnp.minimum(pos + 1, w).astype(F32)
        dlt = ws / cnt - ext_ref[hist:hist + ts, lo:hi]
        o_ref[0, :, lo:hi] = (_mm(dlt.astype(BF16), pw_ref[g]) * ps_ref[:, lo:hi]).astype(o_ref.dtype)


def _pool_prompt(u, pw, ps, ts, layer):
    b, s, c = u.shape
    hist = 16
    ratio = ts // hist
    return pl.pallas_call(
        functools.partial(_pool_prompt_kernel, ts=ts),
        grid=(b, s // ts),
        in_specs=[pl.BlockSpec((1, ts, c), lambda i, t: (i, t, 0)),
                  pl.BlockSpec((1, hist, c), lambda i, t: (i, jnp.maximum(t * ratio - 1, 0), 0)),
                  _layer_spec(pw, layer), _layer_spec(ps, layer)],
        out_specs=pl.BlockSpec((1, ts, c), lambda i, t: (i, t, 0)),
        out_shape=jax.ShapeDtypeStruct((b, s, c), BF16),
        scratch_shapes=[pltpu.VMEM((ts + hist, c), F32)],
        compiler_params=_cparams(2),
        name="pool_prompt",
    )(u, u, pw, ps)


def _pool_sample_kernel(st_ref, u_ref, pw_ref, ps_ref, o_ref, *, start_pos):
    n_new = u_ref.shape[0]

    def row(idx, lo, hi):
        return st_ref[idx, :, lo:hi] if idx < POOL_HIST else u_ref[idx - POOL_HIST, :, lo:hi]

    for j in range(n_new):
        for g, w in enumerate(POOL_WINDOWS):
            lo, hi = g * POOL_GROUP, (g + 1) * POOL_GROUP
            ws = row(POOL_HIST + j, lo, hi)
            for k in range(1, w):
                ws = ws + row(POOL_HIST + j - k, lo, hi)
            cnt = float(min(start_pos + j + 1, w))
            dlt = ws / cnt - row(POOL_HIST + j, lo, hi)
            o_ref[j, :, lo:hi] = _mm(dlt.astype(BF16), pw_ref[g]) * ps_ref[:, lo:hi]


def _pool_sample(st_t, u_t, pw, ps, start_pos, layer):
    t, b, c = u_t.shape
    full = lambda shape: pl.BlockSpec(shape, lambda i: (0,) * len(shape))
    return pl.pallas_call(
        functools.partial(_pool_sample_kernel, start_pos=start_pos),
        grid=(1,),
        in_specs=[full(st_t.shape), full(u_t.shape), _layer_spec(pw, layer), _layer_spec(ps, layer)],
        out_specs=full((t, b, c)),
        out_shape=jax.ShapeDtypeStruct((t, b, c), F32),
        compiler_params=_cparams(1),
        name="pool_sample",
    )(st_t, u_t, pw, ps)


def _load_head_queries(qt_ref, qz_ref):
    tq = qt_ref.shape[2]

    @pl.when((pl.program_id(0) == 0) & (pl.program_id(1) == 0))
    def _():
        qz_ref[...] = jnp.zeros_like(qz_ref)

    for kvh in range(KV_HEADS):
        for g in range(GROUP):
            h = kvh * GROUP + g
            qz_ref[kvh, kvh * HEAD_DIM:(kvh + 1) * HEAD_DIM, g * tq:(g + 1) * tq] = (
                qt_ref[0, h * HEAD_DIM:(h + 1) * HEAD_DIM, :])


def _store_heads_out(o_ref, per_head, tq):
    out_t = jnp.concatenate(
        [per_head[kvh][:, g * tq:(g + 1) * tq] for kvh in range(KV_HEADS) for g in range(GROUP)], axis=0)
    o_ref[0] = out_t.T.astype(o_ref.dtype)


def _att_specs(tiles, tq):
    return [pl.BlockSpec((1, ATT_WIDTH, tq), lambda i, j: (i * tiles + j, 0, 0)),
            pl.BlockSpec((tiles, tq, KV_WIDTH), lambda i, j: (i, 0, 0)),
            pl.BlockSpec((tiles, KV_WIDTH, tq), lambda i, j: (i, 0, 0))]


def _suffix_matrix_t(tk):
    r = np.arange(tk)
    later = (r[None, :] > r[:, None]).astype(np.float32)
    return jnp.asarray(np.concatenate([later, np.ones((BF16_ROWS, tk), np.float32)], axis=0), dtype=BF16)


def _sb_prompt_kernel(qt_ref, k_ref, vt_ref, u_ref, o_ref, qz_ref, acc_ref, carry_ref):
    i = pl.program_id(1)
    tk = k_ref.shape[1]
    q2 = qz_ref.shape[2]
    tq = q2 // GROUP
    _load_head_queries(qt_ref, qz_ref)
    acc_ref[...] = jnp.zeros_like(acc_ref)
    carry_ref[...] = jnp.zeros_like(carry_ref)
    key = lax.broadcasted_iota(jnp.int32, (tk, q2), 0)
    qloc = lax.broadcasted_iota(jnp.int32, (tk, q2), 1) & (tq - 1)
    diag_valid = key < qloc
    u2 = u_ref[...]

    def block(j, valid):
        kb = k_ref[j]
        heads = range(KV_HEADS)
        z = [_mm(kb, qz_ref[kvh]) for kvh in heads]
        drop = [_softplus(z[kvh]) for kvh in heads]
        log_take = [z[kvh] - drop[kvh] for kvh in heads]
        if valid is not None:
            drop = [jnp.where(valid, d, 0.0) for d in drop]
        cs = [_mm(u2, d.astype(BF16)) for d in drop]
        w = [jnp.exp(log_take[kvh] - (cs[kvh][0:tk] + carry_ref[kvh, 0:1, :])) for kvh in heads]
        if valid is not None:
            w = [jnp.where(valid, x, 0.0) for x in w]
        for kvh in heads:
            acc_ref[kvh] += _mm(vt_ref[j, kvh * HEAD_DIM:(kvh + 1) * HEAD_DIM, :], w[kvh].astype(BF16))
            carry_ref[kvh] += cs[kvh][tk:tk + 8]

    block(i, diag_valid)

    def all_dropped():
        return (jnp.min(carry_ref[...]) >= SB_DROP_DONE).astype(jnp.int32)

    def cond(state):
        t, done = state
        return jnp.logical_and(t < i, done == 0)

    def body(state):
        t, _ = state
        block(i - 1 - t, None)
        return t + 1, all_dropped()

    lax.while_loop(cond, body, (jnp.int32(0), all_dropped()))
    _store_heads_out(o_ref, [acc_ref[kvh] for kvh in range(KV_HEADS)], tq)


def _sb_prompt(qt, k3, vt3, b, tiles):
    tq = k3.shape[1]
    q2 = GROUP * tq
    return pl.pallas_call(
        _sb_prompt_kernel,
        grid=(b, tiles),
        in_specs=_att_specs(tiles, tq) + [pl.BlockSpec((tq + BF16_ROWS, tq), lambda i, j: (0, 0))],
        out_specs=pl.BlockSpec((1, tq, ATT_WIDTH), lambda i, j: (i, j, 0)),
        out_shape=jax.ShapeDtypeStruct((b, tiles * tq, ATT_WIDTH), BF16),
        scratch_shapes=[pltpu.VMEM((KV_HEADS, KV_WIDTH, q2), BF16),
                        pltpu.VMEM((KV_HEADS, HEAD_DIM, q2), F32),
                        pltpu.VMEM((KV_HEADS, 8, q2), F32)],
        compiler_params=_cparams(2),
        name="sb_prompt",
    )(qt, k3, vt3, _suffix_matrix_t(tq))


def _page_specs(layer, slots, page, seq, nseq):
    def spec(p):
        return pl.BlockSpec((None, None, KV_WIDTH, page), lambda b, pt: (layer, pt[b * nseq + seq, p], 0, 0))
    return [spec(p) for p in slots]


def _sample_refs(refs, npages, nseq):
    kp = [refs[s * npages:(s + 1) * npages] for s in range(nseq)]
    vp = [refs[(nseq + s) * npages:(nseq + s + 1) * npages] for s in range(nseq)]
    o_ref, padk_ref, padv_ref = refs[2 * nseq * npages:]
    return kp, vp, o_ref, padk_ref, padv_ref


def _load_new_rows(kn_ref, vn_ref, padk_ref, padv_ref, nseq):
    n_new = kn_ref.shape[2]

    @pl.when(pl.program_id(0) == 0)
    def _():
        padk_ref[...] = jnp.zeros_like(padk_ref)
        padv_ref[...] = jnp.zeros_like(padv_ref)

    for s in range(nseq):
        padk_ref[s, 0:n_new, :] = kn_ref[0, s]
        padv_ref[s, 0:n_new, :] = vn_ref[0, s]


def _sample_queries(q8):
    rows = KV_HEADS * q8.shape[0]
    q = jnp.concatenate([q8] * KV_HEADS, axis=0)
    row = lax.broadcasted_iota(jnp.int32, (rows, KV_WIDTH), 0)
    lane = lax.broadcasted_iota(jnp.int32, (rows, KV_WIDTH), 1)
    return jnp.where((row >> 3) == (lane >> 6), q, 0.0)


def _sample_heads_out(acc):
    n = acc.shape[0] // KV_HEADS
    lane_head = lax.broadcasted_iota(jnp.int32, (n, KV_WIDTH), 1) >> 6
    out = jnp.zeros((n, KV_WIDTH), F32)
    for kvh in range(KV_HEADS):
        out = jnp.where(lane_head == kvh, acc[kvh * n:(kvh + 1) * n, :], out)
    return out


def _sample_call(body, name, q4, k_new, v_new, consts, cache_k, cache_v, page_table, layer):
    nsteps, nseq, rows, _ = q4.shape
    n_new = k_new.shape[2]
    npages = page_table.shape[1]
    page = cache_k.shape[3]
    pages = [spec for s in range(nseq) for spec in _page_specs(layer, range(npages), page, s, nseq)]
    grid_spec = pltpu.PrefetchScalarGridSpec(
        num_scalar_prefetch=1,
        grid=(nsteps,),
        in_specs=[pl.BlockSpec((1, nseq, rows, KV_WIDTH), lambda b, pt: (b, 0, 0, 0)),
                  pl.BlockSpec((1, nseq, n_new, KV_WIDTH), lambda b, pt: (b, 0, 0, 0)),
                  pl.BlockSpec((1, nseq, n_new, KV_WIDTH), lambda b, pt: (b, 0, 0, 0))]
                 + [pl.BlockSpec(c.shape, lambda b, pt: (0, 0)) for c in consts] + pages + pages,
        out_specs=pl.BlockSpec((1, nseq, rows, KV_WIDTH), lambda b, pt: (b, 0, 0, 0)),
        scratch_shapes=[pltpu.VMEM((nseq, page, KV_WIDTH), F32), pltpu.VMEM((nseq, page, KV_WIDTH), F32)])
    return pl.pallas_call(
        functools.partial(body, npages=npages, nseq=nseq),
        grid_spec=grid_spec,
        out_shape=jax.ShapeDtypeStruct(q4.shape, F32),
        compiler_params=_cparams(1),
        name=name,
    )(page_table, q4, k_new, v_new, *consts, *([cache_k] * (nseq * npages)), *([cache_v] * (nseq * npages)))


def _sb_sample_blocks(qz, kt, vt, new_kv, valid_new, umat, acc, carry):
    nseq, ncached = len(kt), len(kt[0])
    nblk = ncached + (new_kv is not None)
    rows, page = carry[0].shape
    work = [(s, i) for s in range(nseq) for i in range(nblk)]
    z = [_mm(qz[s], kt[s][i][...].astype(BF16)) if i < ncached else _mm_nt(qz[s], new_kv[s][0].astype(BF16))
         for s, i in work]
    drop = [_softplus(x) for x in z]
    take = [x - d for x, d in zip(z, drop)]
    drop = [d if i < ncached else jnp.where(valid_new, d, 0.0) for d, (s, i) in zip(drop, work)]
    parts = []
    for d in drop:
        parts.extend(_split2(d))
    cs = _mm(jnp.concatenate(parts, axis=0), umat)
    acc, carry = list(acc), list(carry)
    for i in reversed(range(nblk)):
        for s in range(nseq):
            j = work.index((s, i))
            c = cs[j * 2 * rows:(j + 1) * 2 * rows]
            both = c[0:rows] + c[rows:2 * rows]
            w = jnp.exp(take[j] - (both[:, 0:page] + carry[s]))
            if i < ncached:
                acc[s] = acc[s] + _mm_nt(w.astype(BF16), vt[s][i][...].astype(BF16))
            else:
                w = jnp.where(valid_new, w, 0.0)
                acc[s] = acc[s] + _mm(w.astype(BF16), new_kv[s][1].astype(BF16))
            carry[s] = carry[s] + both[:, page:]
    return acc, carry


def _sb_sample_head_kernel(pt_ref, q_ref, kn_ref, vn_ref, u_ref, *refs, npages, nseq):
    del pt_ref
    kp = [refs[s * npages:(s + 1) * npages] for s in range(nseq)]
    vp = [refs[(nseq + s) * npages:(nseq + s + 1) * npages] for s in range(nseq)]
    acc_ref, carry_ref, padk_ref, padv_ref = refs[2 * nseq * npages:]
    _load_new_rows(kn_ref, vn_ref, padk_ref, padv_ref, nseq)
    n_new = kn_ref.shape[2]
    page = u_ref.shape[0]
    rows = KV_HEADS * q_ref.shape[2]
    row = lax.broadcasted_iota(jnp.int32, (rows, page), 0)
    col = lax.broadcasted_iota(jnp.int32, (rows, page), 1)
    valid_new = col < (row & (n_new - 1))
    qz = [_sample_queries(q_ref[0, s]).astype(BF16) for s in range(nseq)]
    acc, carry = _sb_sample_blocks(
        qz, kp, vp, [(padk_ref[s], padv_ref[s]) for s in range(nseq)], valid_new, u_ref[...],
        [jnp.zeros((rows, KV_WIDTH), F32)] * nseq, [jnp.zeros((rows, page), F32)] * nseq)
    for s in range(nseq):
        acc_ref[0, s] = acc[s]
        carry_ref[0, s] = carry[s]


def _sb_sample_tail_kernel(pt_ref, q_ref, acc_in_ref, carry_in_ref, u_ref, *refs, npages, nseq):
    del pt_ref
    kp = [refs[s * npages:(s + 1) * npages] for s in range(nseq)]
    vp = [refs[(nseq + s) * npages:(nseq + s + 1) * npages] for s in range(nseq)]
    acc_ref = refs[2 * nseq * npages]
    qz = [_sample_queries(q_ref[0, s]).astype(BF16) for s in range(nseq)]
    acc, _ = _sb_sample_blocks(qz, kp, vp, None, None, u_ref[...],
                               [acc_in_ref[0, s] for s in range(nseq)], [carry_in_ref[0, s] for s in range(nseq)])
    for s in range(nseq):
        acc_ref[0, s] = acc[s]


def _sb_sample(q4, k_new, v_new, umat, cache_k, cache_v, page_table, layer, n_eager):
    nsteps, nseq, n, _ = q4.shape
    rows = KV_HEADS * n
    n_new = k_new.shape[2]
    npages = page_table.shape[1]
    page = cache_k.shape[3]
    n_eager = min(n_eager, npages)
    step4 = lambda b, pt: (b, 0, 0, 0)
    q_spec = pl.BlockSpec((1, nseq, n, KV_WIDTH), step4)
    new_spec = pl.BlockSpec((1, nseq, n_new, KV_WIDTH), step4)
    acc_spec = pl.BlockSpec((1, nseq, rows, KV_WIDTH), step4)
    carry_spec = pl.BlockSpec((1, nseq, rows, page), step4)
    u_spec = pl.BlockSpec(umat.shape, lambda b, pt: (0, 0))
    acc_shape = jax.ShapeDtypeStruct((nsteps, nseq, rows, KV_WIDTH), F32)
    carry_shape = jax.ShapeDtypeStruct((nsteps, nseq, rows, page), F32)

    def page_operands(slots):
        specs = [spec for s in range(nseq) for spec in _page_specs(layer, slots, page, s, nseq)]
        count = nseq * len(slots)
        return specs + specs, [cache_k] * count + [cache_v] * count

    specs, operands = page_operands(range(npages - n_eager, npages))
    acc, carry = pl.pallas_call(
        functools.partial(_sb_sample_head_kernel, npages=n_eager, nseq=nseq),
        grid_spec=pltpu.PrefetchScalarGridSpec(
            num_scalar_prefetch=1, grid=(nsteps,),
            in_specs=[q_spec, new_spec, new_spec, u_spec] + specs,
            out_specs=[acc_spec, carry_spec],
            scratch_shapes=[pltpu.VMEM((nseq, page, KV_WIDTH), F32), pltpu.VMEM((nseq, page, KV_WIDTH), F32)]),
        out_shape=[acc_shape, carry_shape],
        compiler_params=_cparams(1),
        name="sb_sample_head",
    )(page_table, q4, k_new, v_new, umat, *operands)
    if npages > n_eager:
        specs, operands = page_operands(range(npages - n_eager))

        def tail(acc_in):
            return pl.pallas_call(
                functools.partial(_sb_sample_tail_kernel, npages=npages - n_eager, nseq=nseq),
                grid_spec=pltpu.PrefetchScalarGridSpec(
                    num_scalar_prefetch=1, grid=(nsteps,),
                    in_specs=[q_spec, acc_spec, carry_spec, u_spec] + specs,
                    out_specs=acc_spec),
                out_shape=acc_shape,
                compiler_params=_cparams(1),
                name="sb_sample_tail",
            )(page_table, q4, acc_in, carry, umat, *operands)

        acc = lax.cond(jnp.min(carry) < SB_DROP_DONE, tail, lambda acc_in: acc_in, acc)
    a6 = acc.reshape(nsteps, nseq, KV_HEADS, n, KV_HEADS, HEAD_DIM)
    return jnp.stack([a6[:, :, h, :, h, :] for h in range(KV_HEADS)], axis=3).reshape(nsteps, nseq, n, KV_WIDTH)


def _suffix_matrix(tk):
    r = np.arange(tk)
    strict = (r[:, None] > r[None, :]).astype(np.float32)
    return jnp.asarray(np.concatenate([strict, np.ones((tk, tk), np.float32)], axis=1), dtype=BF16)


def _topk_rows(scores, n_valid, nb):
    blk = lax.broadcasted_iota(jnp.int32, scores.shape, 0)
    valid = blk < n_valid
    s = jnp.where(valid, scores, NEG_INF)
    rank = jnp.zeros(scores.shape, jnp.int32)
    for m in range(nb):
        other = s[m:m + 1, :]
        beats = (other > s) | ((other == s) & (blk > m))
        rank = rank + jnp.where(beats, 1, 0)
    return jnp.where(valid & (rank < MOBA_TOPK), 1.0, 0.0)


def _topk_lanes(scores, n_valid, nb):
    blk = lax.broadcasted_iota(jnp.int32, scores.shape, 1)
    valid = blk < n_valid
    s = jnp.where(valid, scores, NEG_INF)
    rank = jnp.zeros(scores.shape, jnp.int32)
    for m in range(nb):
        other = s[:, m:m + 1]
        beats = (other > s) | ((other == s) & (blk > m))
        rank = rank + jnp.where(beats, 1, 0)
    return jnp.where(valid & (rank < MOBA_TOPK), 1.0, 0.0)


def _moba_prompt_kernel(qtf_ref, qt_ref, k_ref, vt_ref, km_ref, o_ref,
                        qz_ref, sel_ref, m_ref, l_ref, acc_ref, *, nb):
    qblk = pl.program_id(1)
    tk = k_ref.shape[1]
    q2 = qz_ref.shape[2]
    tq = q2 // GROUP
    _load_head_queries(qt_ref, qz_ref)
    heads = range(KV_HEADS)
    qf = jnp.concatenate(
        [jnp.concatenate([qtf_ref[0, (kvh * GROUP + g) * HEAD_DIM:(kvh * GROUP + g + 1) * HEAD_DIM, :]
                          for kvh in heads], axis=0) for g in range(GROUP)], axis=1)
    q_hi, q_lo = _split2(qf)
    km = km_ref[0]
    lane_head = lax.broadcasted_iota(jnp.int32, km.shape, 1) >> 6
    km_parts = [_split2(jnp.where(lane_head == kvh, km, 0.0)) for kvh in heads]
    scores = [_mm(km_hi, q_hi) + (_mm(km_hi, q_lo) + _mm(km_lo, q_hi)) for km_hi, km_lo in km_parts]
    for kvh in heads:
        sel_ref[kvh] = _topk_rows(scores[kvh], qblk, nb)
    key = lax.broadcasted_iota(jnp.int32, (tk, q2), 0)
    qloc = lax.broadcasted_iota(jnp.int32, (tk, q2), 1) & (tq - 1)
    causal = key <= qloc
    kb = k_ref[qblk]
    st = [jnp.where(causal, _mm(kb, qz_ref[kvh]), NEG_INF) for kvh in heads]
    m0 = [jnp.max(s, axis=0, keepdims=True) for s in st]
    p = [jnp.exp2(st[kvh] - m0[kvh]) for kvh in heads]
    for kvh in heads:
        m_ref[kvh] = m0[kvh]
        l_ref[kvh] = jnp.sum(p[kvh], axis=0, keepdims=True)
        acc_ref[kvh] = _mm(vt_ref[qblk, kvh * HEAD_DIM:(kvh + 1) * HEAD_DIM, :], p[kvh].astype(BF16))

    def body(n, c):
        kb = k_ref[n]
        bias = [jnp.where(sel_ref[kvh, pl.ds(n, 1), :] > 0.5, 0.0, NEG_INF) for kvh in heads]
        st = [_mm(kb, qz_ref[kvh]) for kvh in heads]
        m_old = [m_ref[kvh] for kvh in heads]
        m_new = [jnp.maximum(m_old[kvh], jnp.max(st[kvh], axis=0, keepdims=True) + bias[kvh]) for kvh in heads]
        p = [jnp.exp2(st[kvh] - (m_new[kvh] - bias[kvh])) for kvh in heads]
        alpha = [jnp.exp2(m_old[kvh] - m_new[kvh]) for kvh in heads]
        pv = [_mm(vt_ref[n, kvh * HEAD_DIM:(kvh + 1) * HEAD_DIM, :], p[kvh].astype(BF16)) for kvh in heads]
        for kvh in heads:
            l_ref[kvh] = alpha[kvh] * l_ref[kvh] + jnp.sum(p[kvh], axis=0, keepdims=True)
            acc_ref[kvh] = alpha[kvh] * acc_ref[kvh] + pv[kvh]
            m_ref[kvh] = m_new[kvh]
        return c

    lax.fori_loop(0, qblk, body, 0)
    _store_heads_out(o_ref, [acc_ref[kvh] / l_ref[kvh] for kvh in heads], tq)


def _moba_prompt(qtf, qt, k3, vt3, kmean, b, tiles):
    tq = k3.shape[1]
    q2 = GROUP * tq
    return pl.pallas_call(
        functools.partial(_moba_prompt_kernel, nb=tiles),
        grid=(b, tiles),
        in_specs=[pl.BlockSpec((1, ATT_WIDTH, tq), lambda i, j: (i * tiles + j, 0, 0))] + _att_specs(tiles, tq)
                 + [pl.BlockSpec((1, tiles, KV_WIDTH), lambda i, j: (i, 0, 0))],
        out_specs=pl.BlockSpec((1, tq, ATT_WIDTH), lambda i, j: (i, j, 0)),
        out_shape=jax.ShapeDtypeStruct((b, tiles * tq, ATT_WIDTH), BF16),
        scratch_shapes=[pltpu.VMEM((KV_HEADS, KV_WIDTH, q2), BF16), pltpu.VMEM((KV_HEADS, tiles, q2), F32),
                        pltpu.VMEM((KV_HEADS, 1, q2), F32), pltpu.VMEM((KV_HEADS, 1, q2), F32),
                        pltpu.VMEM((KV_HEADS, HEAD_DIM, q2), F32)],
        compiler_params=_cparams(2),
        name="moba_prompt",
    )(qtf, qt, k3, vt3, kmean)


def _moba_sample_kernel(pt_ref, q_ref, kn_ref, vn_ref, *refs, npages, nseq):
    del pt_ref
    kp, vp, o_ref, padk_ref, padv_ref = _sample_refs(refs, npages, nseq)
    _load_new_rows(kn_ref, vn_ref, padk_ref, padv_ref, nseq)
    n_new = kn_ref.shape[2]
    page = padk_ref.shape[1]
    per_blk = MOBA_BLOCK // page
    nb_past = npages // per_blk
    rows = KV_HEADS * q_ref.shape[2]
    seqs = range(nseq)
    qf = [_sample_queries(q_ref[0, s]) for s in seqs]
    qz = [(q * ATT_SCALE).astype(BF16) for q in qf]
    ones = jnp.ones((page, LANES), BF16)
    lane = lax.broadcasted_iota(jnp.int32, (KV_WIDTH, LANES), 1)
    k_parts = [[_split2(kp[s][pg][...]) for pg in range(npages)] for s in seqs]
    k_sums = [[_mm(hi, ones) + _mm(lo, ones) for hi, lo in k_parts[s]] for s in seqs]
    km = []
    for s in seqs:
        km_s = jnp.zeros((KV_WIDTH, LANES), F32)
        for blk in range(nb_past):
            tot = k_sums[s][blk * per_blk]
            for r in range(1, per_blk):
                tot = tot + k_sums[s][blk * per_blk + r]
            km_s = jnp.where(lane == blk, tot * (1.0 / MOBA_BLOCK), km_s)
        km.append(km_s)
    sel = []
    for s in seqs:
        q_hi, q_lo = _split2(qf[s])
        km_hi, km_lo = _split2(km[s])
        scores = _mm(q_hi, km_hi) + (_mm(q_hi, km_lo) + _mm(q_lo, km_hi))
        sel.append(_topk_lanes(scores, nb_past, nb_past))
    row = lax.broadcasted_iota(jnp.int32, (rows, page), 0)
    col = lax.broadcasted_iota(jnp.int32, (rows, page), 1)
    visible = (col <= (row & (n_new - 1))) & (col < n_new)
    logits = [[jnp.where(sel[s][:, pg // per_blk:pg // per_blk + 1] > 0.5,
                         _mm(qz[s], kp[s][pg][...].astype(BF16)), NEG_INF) for pg in range(npages)]
              + [jnp.where(visible, _mm_nt(qz[s], padk_ref[s].astype(BF16)), NEG_INF)] for s in seqs]
    m = []
    for s in seqs:
        m_el = logits[s][0]
        for x in logits[s][1:]:
            m_el = jnp.maximum(m_el, x)
        m.append(jnp.max(m_el, axis=1, keepdims=True))
    acc = [jnp.zeros((rows, KV_WIDTH), F32) for _ in seqs]
    l_el = [jnp.zeros((rows, page), F32) for _ in seqs]
    for pg in range(npages + 1):
        for s in seqs:
            p = jnp.exp(logits[s][pg] - m[s])
            l_el[s] = l_el[s] + p
            if pg < npages:
                acc[s] = acc[s] + _mm_nt(p.astype(BF16), vp[s][pg][...].astype(BF16))
            else:
                acc[s] = acc[s] + _mm(p.astype(BF16), padv_ref[s].astype(BF16))
    for s in seqs:
        o_ref[0, s] = _sample_heads_out(acc[s] / jnp.sum(l_el[s], axis=1, keepdims=True))


def _merge_kernel(x_ref, a_ref, b_ref, c_ref, g_ref, wb_ref, wo_ref, o_ref):
    d = x_ref.shape[1]
    m = g_ref[:, 0:d] * _mm(a_ref[...].astype(BF16), wb_ref[0])
    m = m + g_ref[:, d:2 * d] * _mm(b_ref[...].astype(BF16), wb_ref[1])
    m = m + g_ref[:, 2 * d:3 * d] * _mm(c_ref[...].astype(BF16), wb_ref[2])
    o_ref[...] = x_ref[...] + _mm(m.astype(BF16), wo_ref[...])


def _merge(x, a, b, c, gate, wb, wo, tm, layer):
    n, d = x.shape
    row = lambda i: (i, 0)
    return pl.pallas_call(
        _merge_kernel,
        grid=(n // tm,),
        in_specs=[pl.BlockSpec((tm, d), row), pl.BlockSpec((tm, ATT_WIDTH), row), pl.BlockSpec((tm, ATT_WIDTH), row),
                  pl.BlockSpec((tm, ATT_WIDTH), row), pl.BlockSpec((tm, N_BRANCH * d), row),
                  _layer_spec(wb, layer), _layer_spec(wo, layer)],
        out_specs=pl.BlockSpec((tm, d), row),
        out_shape=jax.ShapeDtypeStruct((n, d), F32),
        compiler_params=_cparams(1, VMEM_LIMIT),
        name="merge",
    )(x, a, b, c, gate, wb, wo)


def _mlp_ple_kernel(x_ref, p_ref, gm_ref, wu_ref, wd_ref, gp_ref, wg_ref, wp_ref, o_ref, *, f_chunk):
    x = x_ref[...]
    ms = jnp.mean(x * x, axis=-1, keepdims=True)
    h = (x * lax.rsqrt(ms + RMS_EPS) * gm_ref[...]).astype(BF16)
    d_ff = wu_ref.shape[1]
    y = x
    for f0 in range(0, d_ff, f_chunk):
        hid = jnp.maximum(_mm(h, wu_ref[:, f0:f0 + f_chunk]), 0.0)
        y = y + _mm((hid * hid).astype(BF16), wd_ref[f0:f0 + f_chunk, :])
    ms2 = jnp.mean(y * y, axis=-1, keepdims=True)
    h2 = (y * lax.rsqrt(ms2 + RMS_EPS) * gp_ref[...]).astype(BF16)
    gate = _sigmoid(_mm(h2, wg_ref[...]))
    o_ref[...] = y + gate * _mm(p_ref[...].astype(BF16), wp_ref[...])


def _mlp_ple(x, p, gm, wu, wd, gp, wg, wp, tm, layer):
    n, d = x.shape
    row = lambda i: (i, 0)
    return pl.pallas_call(
        functools.partial(_mlp_ple_kernel, f_chunk=512),
        grid=(n // tm,),
        in_specs=[pl.BlockSpec((tm, d), row), pl.BlockSpec((None, tm, p.shape[2]), lambda i: (layer, i, 0)),
                  _layer_spec(gm, layer), _layer_spec(wu, layer), _layer_spec(wd, layer),
                  _layer_spec(gp, layer), _layer_spec(wg, layer), _layer_spec(wp, layer)],
        out_specs=pl.BlockSpec((tm, d), row),
        out_shape=jax.ShapeDtypeStruct((n, d), F32),
        compiler_params=_cparams(1, VMEM_LIMIT),
        name="mlp_ple",
    )(x, p, gm, wu, wd, gp, wg, wp)


def _rope_tables(pos):
    half = ROT_DIM // 2
    inv = ROPE_THETA ** (-jnp.arange(half, dtype=F32) / half)
    ang = pos.astype(F32)[:, None] * inv[None, :]
    cos, sin = jnp.cos(ang), jnp.sin(ang)
    t = pos.shape[0]
    c64 = jnp.concatenate([cos, cos, jnp.ones((t, HEAD_DIM - ROT_DIM), F32)], axis=1)
    s64 = jnp.concatenate([-sin, sin, jnp.zeros((t, HEAD_DIM - ROT_DIM), F32)], axis=1)
    reps = LANES // HEAD_DIM
    return jnp.tile(c64, (1, reps)), jnp.tile(s64, (1, reps))


def _head_group_matrix():
    r = np.arange(ATT_WIDTH) // HEAD_DIM
    return jnp.asarray((r[:, None] == r[None, :]).astype(np.float32), dtype=BF16)


def kernel(x_prompt, x_sample, p_prompt, p_sample, cache_sb_k, cache_sb_v, cache_moba_k, cache_moba_v,
           state_pool, page_table, norm_mix, w_in, pool_w, pool_scale, moba_q_norm, moba_k_norm,
           w_branch, w_out, norm_mlp, w_up, w_down, norm_ple, w_ple_gate, w_ple):
    depth = w_in.shape[0]
    bp, seq, d = x_prompt.shape
    bs, t_new, _ = x_sample.shape
    n_p, n_s = bp * seq, bs * t_new
    n_phys, page = cache_sb_k.shape[1], cache_sb_k.shape[2]
    past_len = page_table.shape[1] * page
    tiles = seq // ATT_TILE
    assert seq % ATT_TILE == 0 and past_len % MOBA_BLOCK == 0 and MOBA_BLOCK % page == 0 and page == LANES
    assert t_new & (t_new - 1) == 0 and GROUP * t_new == 8 and past_len >= POOL_HIST >= t_new
    assert past_len // MOBA_BLOCK <= LANES

    w_in_b, w_branch_b = w_in.astype(BF16), w_branch.astype(BF16)
    w_out_b, w_up_b, w_down_b = w_out.astype(BF16), w_up.astype(BF16), w_down.astype(BF16)
    w_pg_b, w_ple_b, pool_w_b = w_ple_gate.astype(BF16), w_ple.astype(BF16), pool_w.astype(BF16)
    vec = lambda a: a[:, None, :]
    norm_mix3, norm_mlp3, norm_ple3, pool_scale3 = vec(norm_mix), vec(norm_mlp), vec(norm_ple), vec(pool_scale)
    qn = vec(jnp.tile(moba_q_norm, (1, N_HEADS)))
    kn = vec(jnp.tile(moba_k_norm, (1, KV_HEADS)))
    gmat = _head_group_matrix()

    tm_p = ATT_TILE
    tm_s = 256 if n_s % 256 == 0 else n_s
    cos_p, sin_p = _rope_tables(jnp.arange(seq))
    cos_s, sin_s = _rope_tables(jnp.tile(past_len + jnp.arange(t_new), bs))
    caches = [c.transpose(0, 1, 3, 4, 2).reshape(depth, n_phys, KV_WIDTH, page)
              for c in (cache_sb_k, cache_sb_v, cache_moba_k, cache_moba_v)]
    pp = p_prompt.reshape(depth, n_p, -1)
    psm = p_sample.reshape(depth, n_s, -1)

    xp = x_prompt.reshape(n_p, d)
    xs = x_sample.reshape(n_s, d)
    kv_stacks = None
    stack_shape = (depth, bp, KV_WIDTH, seq)
    pool_p = []
    outs_s = [[] for _ in range(5)]
    nseq = 2 if bs % 2 == 0 else 1

    def to_q4(q):
        return q.reshape(bs, t_new, KV_HEADS, GROUP, HEAD_DIM).transpose(0, 3, 1, 2, 4).reshape(
            bs // nseq, nseq, GROUP * t_new, KV_WIDTH)

    def from_q4(r):
        return r.reshape(bs, GROUP, t_new, KV_HEADS, HEAD_DIM).transpose(0, 2, 3, 1, 4).reshape(n_s, ATT_WIDTH)

    def new_rows(a):
        return a.reshape(bs // nseq, nseq, t_new, KV_WIDTH)

    sb_umat = _suffix_matrix(page)

    for i in range(depth):
        layer_w = (norm_mix3, w_in_b, qn, kn)
        post_w = (norm_mlp3, w_up_b, w_down_b, norm_ple3, w_pg_b, w_ple_b)

        (u, qt_sb, k_sb_b, vt_sb, qt_mf, qt_mb, k_mb_b, vt_mb, kmean, gate, *kv_stacks) = _inproj(
            xp, *layer_w, cos_p, sin_p, gmat, tm_p, i, kv_stacks, stack_shape)
        k3 = lambda a: a.reshape(n_p // ATT_TILE, ATT_TILE, KV_WIDTH)
        u3 = u.reshape(bp, seq, ATT_WIDTH)
        a = _pool_prompt(u3, pool_w_b, pool_scale3, min(512, seq), i).reshape(n_p, ATT_WIDTH)
        b = _sb_prompt(qt_sb, k3(k_sb_b), vt_sb, bp, tiles).reshape(n_p, ATT_WIDTH)
        c = _moba_prompt(qt_mf, qt_mb, k3(k_mb_b), vt_mb, kmean.reshape(bp, tiles, KV_WIDTH), bp, tiles).reshape(
            n_p, ATT_WIDTH)
        xp = _merge(xp, a, b, c, gate, w_branch_b, w_out_b, tm_p, i)
        xp = _mlp_ple(xp, pp, *post_w, tm_p, i)
        pool_p.append(u3[:, seq - POOL_HIST:])

        (u, q_sb, k_sb, v_sb, q_mb, k_mb, v_mb, gate) = _inproj(xs, *layer_w, cos_s, sin_s, gmat, tm_s, i)
        r3 = lambda a: a.reshape(bs, t_new, a.shape[-1])
        st_t = state_pool[i].transpose(1, 0, 2)
        a = _pool_sample(st_t, r3(u).transpose(1, 0, 2), pool_w_b, pool_scale3, past_len, i).transpose(1, 0, 2).reshape(
            n_s, ATT_WIDTH)
        b = from_q4(_sb_sample(to_q4(q_sb), new_rows(k_sb), new_rows(v_sb), sb_umat, caches[0], caches[1],
                               page_table, i, SB_EAGER_PAGES))
        c = from_q4(_sample_call(_moba_sample_kernel, "moba_sample", to_q4(q_mb), new_rows(k_mb), new_rows(v_mb), [],
                                 caches[2], caches[3], page_table, i))
        xs = _merge(xs, a, b, c, gate, w_branch_b, w_out_b, tm_s, i)
        xs = _mlp_ple(xs, psm, *post_w, tm_s, i)
        pool_state = jnp.concatenate([state_pool[i][:, t_new:], r3(u)], axis=1)
        for lst, val in zip(outs_s, (k_sb, v_sb, k_mb, v_mb, pool_state)):
            lst.append(val)

    def kv_prompt(stack):
        return stack.reshape(depth, bp, KV_HEADS, HEAD_DIM, seq).transpose(0, 1, 4, 2, 3)

    def kv_sample(lst):
        return jnp.stack(lst).reshape(depth, bs, t_new, KV_HEADS, HEAD_DIM)

    return (xp.reshape(bp, seq, d), xs.reshape(bs, t_new, d),
            kv_prompt(kv_stacks[0]), kv_prompt(kv_stacks[1]), kv_prompt(kv_stacks[2]), kv_prompt(kv_stacks[3]),
            jnp.stack(pool_p),
            kv_sample(outs_s[0]), kv_sample(outs_s[1]), kv_sample(outs_s[2]), kv_sample(outs_s[3]),
            jnp.stack(outs_s[4]))
```

```python
import functools

import jax
import jax.numpy as jnp
import numpy as np
from jax import lax
from jax.experimental import pallas as pl
from jax.experimental.pallas import tpu as pltpu

F32 = jnp.float32
BF16 = jnp.bfloat16

HEAD_DIM = 64
KV_HEADS = 4
GROUP = 2
N_HEADS = KV_HEADS * GROUP
ATT_WIDTH = N_HEADS * HEAD_DIM
KV_WIDTH = KV_HEADS * HEAD_DIM
ATT_SCALE = HEAD_DIM ** -0.5
LOG2_E = 1.4426950408889634
POOL_WINDOWS = (2, 4, 8, 16)
POOL_GROUP = 128
POOL_HIST = max(POOL_WINDOWS) - 1
ROPE_THETA = 500000.0
ROT_DIM = HEAD_DIM // 4
MOBA_BLOCK = 256
MOBA_TOPK = 3
N_BRANCH = 3
RMS_EPS = 1e-6
NEG_INF = float("-inf")
SB_DROP_DONE = 104.0
SB_EAGER_PAGES = 3

LANES = 128
BF16_ROWS = 16
ATT_TILE = MOBA_BLOCK
VMEM_LIMIT = 56 * 1024 * 1024


def _cparams(n_axes, vmem=None):
    return pltpu.CompilerParams(dimension_semantics=("arbitrary",) * n_axes, vmem_limit_bytes=vmem)


def _layer_spec(arr, layer):
    zeros = (0,) * (arr.ndim - 1)
    return pl.BlockSpec((None,) + arr.shape[1:], lambda *_: (layer,) + zeros)


def _mm(a, b):
    return jnp.dot(a, b, preferred_element_type=F32)


def _mm_nt(a, b):
    return lax.dot_general(a, b, (((1,), (1,)), ((), ())), preferred_element_type=F32)


def _split2(x):
    hi = x.astype(BF16)
    lo = (x - hi.astype(F32)).astype(BF16)
    return hi, lo


def _sigmoid(x):
    return 1.0 / (1.0 + jnp.exp(-x))


def _softplus(z):
    return jnp.maximum(z, 0.0) + jnp.log(1.0 + jnp.exp(jnp.minimum(z, -z)))


def _inproj_kernel(x_ref, gn_ref, w_ref, qn_ref, kn_ref, cos_ref, sin_ref, gmat_ref, *refs, prompt, n_alias):
    out_refs = refs[n_alias:]
    x = x_ref[...]
    ms = jnp.mean(x * x, axis=-1, keepdims=True)
    h = (x * lax.rsqrt(ms + RMS_EPS) * gn_ref[...]).astype(BF16)

    def proj(lo, hi):
        return _mm(h, w_ref[:, lo:hi])

    def head_norm_rope(y, g_ref):
        width = y.shape[1]
        gm = gmat_ref[0:width, 0:width]
        p0, p1 = _split2(y * y)
        ms_h = (_mm(p0, gm) + _mm(p1, gm)) * (1.0 / HEAD_DIM)
        yn = y * lax.rsqrt(ms_h + RMS_EPS) * g_ref[...]
        reps = width // LANES
        c = jnp.concatenate([cos_ref[...]] * reps, axis=1)
        s = jnp.concatenate([sin_ref[...]] * reps, axis=1)
        lane = lax.broadcasted_iota(jnp.int32, yn.shape, 1)
        first_half = (lane & (HEAD_DIM - 1)) < (ROT_DIM // 2)
        partner = jnp.where(first_half, pltpu.roll(yn, width - ROT_DIM // 2, 1), pltpu.roll(yn, ROT_DIM // 2, 1))
        return yn * c + partner * s

    o = 0
    u = proj(o, o + ATT_WIDTH)
    o += ATT_WIDTH
    q_sb = proj(o, o + ATT_WIDTH) * ATT_SCALE
    o += ATT_WIDTH
    k_sb = proj(o, o + KV_WIDTH)
    o += KV_WIDTH
    v_sb = proj(o, o + KV_WIDTH)
    o += KV_WIDTH
    q_mb = head_norm_rope(proj(o, o + ATT_WIDTH), qn_ref)
    o += ATT_WIDTH
    k_mb = head_norm_rope(proj(o, o + KV_WIDTH), kn_ref)
    o += KV_WIDTH
    v_mb = proj(o, o + KV_WIDTH)
    o += KV_WIDTH
    if prompt:
        (u_ref, qsb_t, ksb_b, vsb_tb, qmb_tf, qmb_t, kmb_b, vmb_tb, kmean_ref, gate_ref,
         ksb_t, vsb_t, kmb_t, vmb_t) = out_refs
        qsb_t[0] = q_sb.T.astype(BF16)
        ksb_b[...] = k_sb.astype(BF16)
        ksb_t[...] = k_sb.T
        v_t = v_sb.T
        vsb_t[...] = v_t
        vsb_tb[0] = v_t.astype(BF16)
        q_mb_t = q_mb.T
        qmb_tf[0] = q_mb_t
        qmb_t[0] = (q_mb_t * (ATT_SCALE * LOG2_E)).astype(BF16)
        kmb_b[...] = k_mb.astype(BF16)
        kmb_t[...] = k_mb.T
        kmean_ref[0] = jnp.sum(k_mb, axis=0, keepdims=True) * (1.0 / k_mb.shape[0])
        v_t = v_mb.T
        vmb_t[...] = v_t
        vmb_tb[0] = v_t.astype(BF16)
    else:
        (u_ref, qsb_ref, ksb_ref, vsb_ref, qmb_ref, kmb_ref, vmb_ref, gate_ref) = out_refs
        qsb_ref[...] = q_sb
        qmb_ref[...] = q_mb
        ksb_ref[...] = k_sb
        vsb_ref[...] = v_sb
        kmb_ref[...] = k_mb
        vmb_ref[...] = v_mb
    u_ref[...] = u
    gate_w = gate_ref.shape[1]
    chunk = 512
    for c0 in range(0, gate_w, chunk):
        gate_ref[:, c0:c0 + chunk] = _sigmoid(proj(o + c0, o + c0 + chunk)).astype(gate_ref.dtype)


def _inproj(x, gn, w, qn, kn, cos_t, sin_t, gmat, tm, layer, kv_stacks=None, stack_shape=None):
    prompt = stack_shape is not None
    n, d = x.shape
    in_w = w.shape[2]
    gate_w = in_w - (3 * ATT_WIDTH + 4 * KV_WIDTH)
    nper = cos_t.shape[0] // tm
    nt = n // tm
    row = lambda i: (i, 0)
    const = lambda i: (0, 0)

    def flat(width, dtype):
        return jax.ShapeDtypeStruct((n, width), dtype), pl.BlockSpec((tm, width), row)

    def transposed(width, dtype):
        return jax.ShapeDtypeStruct((nt, width, tm), dtype), pl.BlockSpec((1, width, tm), lambda i: (i, 0, 0))

    alias_in, aliases = [], {}
    if prompt:
        tiles = stack_shape[3] // tm
        stacked = (jax.ShapeDtypeStruct(stack_shape, F32),
                   pl.BlockSpec((None, None, KV_WIDTH, tm), lambda i: (layer, i // tiles, 0, i % tiles)))
        outs = [flat(ATT_WIDTH, F32), transposed(ATT_WIDTH, BF16), flat(KV_WIDTH, BF16), transposed(KV_WIDTH, BF16),
                transposed(ATT_WIDTH, F32), transposed(ATT_WIDTH, BF16), flat(KV_WIDTH, BF16),
                transposed(KV_WIDTH, BF16),
                (jax.ShapeDtypeStruct((nt, 1, KV_WIDTH), F32), pl.BlockSpec((1, 1, KV_WIDTH), lambda i: (i, 0, 0))),
                flat(gate_w, BF16), stacked, stacked, stacked, stacked]
        if kv_stacks is not None:
            alias_in = list(kv_stacks)
            aliases = {8 + k: len(outs) - 4 + k for k in range(4)}
    else:
        outs = [flat(ATT_WIDTH, F32), flat(ATT_WIDTH, F32), flat(KV_WIDTH, F32), flat(KV_WIDTH, F32),
                flat(ATT_WIDTH, F32), flat(KV_WIDTH, F32), flat(KV_WIDTH, F32), flat(gate_w, BF16)]
    return pl.pallas_call(
        functools.partial(_inproj_kernel, prompt=prompt, n_alias=len(alias_in)),
        grid=(nt,),
        in_specs=[pl.BlockSpec((tm, d), row), _layer_spec(gn, layer), _layer_spec(w, layer),
                  _layer_spec(qn, layer), _layer_spec(kn, layer),
                  pl.BlockSpec((tm, LANES), lambda i: (i % nper, 0)), pl.BlockSpec((tm, LANES), lambda i: (i % nper, 0)),
                  pl.BlockSpec((ATT_WIDTH, ATT_WIDTH), const)]
                 + [pl.BlockSpec(memory_space=pl.ANY)] * len(alias_in),
        out_specs=[s for _, s in outs],
        out_shape=[s for s, _ in outs],
        input_output_aliases=aliases,
        compiler_params=_cparams(1, VMEM_LIMIT),
        name="inproj_prompt" if prompt else "inproj_sample",
    )(x, gn, w, qn, kn, cos_t, sin_t, gmat, *alias_in)


def _pool_prompt_kernel(cur_ref, prev_ref, pw_ref, ps_ref, o_ref, ext_ref, *, ts):
    t = pl.program_id(1)
    hist = ext_ref.shape[0] - ts
    ext_ref[0:hist, :] = jnp.where(t == 0, 0.0, prev_ref[0])
    ext_ref[hist:, :] = cur_ref[0]
    pos = t * ts + lax.broadcasted_iota(jnp.int32, (ts, POOL_GROUP), 0)
    for g, w in enumerate(POOL_WINDOWS):
        lo, hi = g * POOL_GROUP, (g + 1) * POOL_GROUP
        ws = ext_ref[hist:hist + ts, lo:hi]
        for k in range(1, w):
            ws = ws + ext_ref[hist - k:hist - k + ts, lo:hi]
        cnt = jnp.minimum(pos + 1, w).astype(F32)
        dlt = ws / cnt - ext_ref[hist:hist + ts, lo:hi]
        o_ref[0, :, lo:hi] = (_mm(dlt.astype(BF16), pw_ref[g]) * ps_ref[:, lo:hi]).astype(o_ref.dtype)


def _pool_prompt(u, pw, ps, ts, layer):
    b, s, c = u.shape
    hist = 16
    ratio = ts // hist
    return pl.pallas_call(
        functools.partial(_pool_prompt_kernel, ts=ts),
        grid=(b, s // ts),
        in_specs=[pl.BlockSpec((1, ts, c), lambda i, t: (i, t, 0)),
                  pl.BlockSpec((1, hist, c), lambda i, t: (i, jnp.maximum(t * ratio - 1, 0), 0)),
                  _layer_spec(pw, layer), _layer_spec(ps, layer)],
        out_specs=pl.BlockSpec((1, ts, c), lambda i, t: (i, t, 0)),
        out_shape=jax.ShapeDtypeStruct((b, s, c), BF16),
        scratch_shapes=[pltpu.VMEM((ts + hist, c), F32)],
        compiler_params=_cparams(2),
        name="pool_prompt",
    )(u, u, pw, ps)


def _pool_sample_kernel(st_ref, u_ref, pw_ref, ps_ref, o_ref, *, start_pos):
    n_new = u_ref.shape[0]

    def row(idx, lo, hi):
        return st_ref[idx, :, lo:hi] if idx < POOL_HIST else u_ref[idx - POOL_HIST, :, lo:hi]

    for j in range(n_new):
        for g, w in enumerate(POOL_WINDOWS):
            lo, hi = g * POOL_GROUP, (g + 1) * POOL_GROUP
            ws = row(POOL_HIST + j, lo, hi)
            for k in range(1, w):
                ws = ws + row(POOL_HIST + j - k, lo, hi)
            cnt = float(min(start_pos + j + 1, w))
            dlt = ws / cnt - row(POOL_HIST + j, lo, hi)
            o_ref[j, :, lo:hi] = _mm(dlt.astype(BF16), pw_ref[g]) * ps_ref[:, lo:hi]


def _pool_sample(st_t, u_t, pw, ps, start_pos, layer):
    t, b, c = u_t.shape
    full = lambda shape: pl.BlockSpec(shape, lambda i: (0,) * len(shape))
    return pl.pallas_call(
        functools.partial(_pool_sample_kernel, start_pos=start_pos),
        grid=(1,),
        in_specs=[full(st_t.shape), full(u_t.shape), _layer_spec(pw, layer), _layer_spec(ps, layer)],
        out_specs=full((t, b, c)),
        out_shape=jax.ShapeDtypeStruct((t, b, c), F32),
        compiler_params=_cparams(1),
        name="pool_sample",
    )(st_t, u_t, pw, ps)


def _load_head_queries(qt_ref, qz_ref):
    tq = qt_ref.shape[2]

    @pl.when((pl.program_id(0) == 0) & (pl.program_id(1) == 0))
    def _():
        qz_ref[...] = jnp.zeros_like(qz_ref)

    for kvh in range(KV_HEADS):
        for g in range(GROUP):
            h = kvh * GROUP + g
            qz_ref[kvh, kvh * HEAD_DIM:(kvh + 1) * HEAD_DIM, g * tq:(g + 1) * tq] = (
                qt_ref[0, h * HEAD_DIM:(h + 1) * HEAD_DIM, :])


def _store_heads_out(o_ref, per_head, tq):
    out_t = jnp.concatenate(
        [per_head[kvh][:, g * tq:(g + 1) * tq] for kvh in range(KV_HEADS) for g in range(GROUP)], axis=0)
    o_ref[0] = out_t.T.astype(o_ref.dtype)


def _att_specs(tiles, tq):
    return [pl.BlockSpec((1, ATT_WIDTH, tq), lambda i, j: (i * tiles + j, 0, 0)),
            pl.BlockSpec((tiles, tq, KV_WIDTH), lambda i, j: (i, 0, 0)),
            pl.BlockSpec((tiles, KV_WIDTH, tq), lambda i, j: (i, 0, 0))]


def _suffix_matrix_t(tk):
    r = np.arange(tk)
    later = (r[None, :] > r[:, None]).astype(np.float32)
    return jnp.asarray(np.concatenate([later, np.ones((BF16_ROWS, tk), np.float32)], axis=0), dtype=BF16)


def _sb_prompt_kernel(qt_ref, k_ref, vt_ref, u_ref, o_ref, qz_ref, acc_ref, carry_ref):
    i = pl.program_id(1)
    tk = k_ref.shape[1]
    q2 = qz_ref.shape[2]
    tq = q2 // GROUP
    _load_head_queries(qt_ref, qz_ref)
    acc_ref[...] = jnp.zeros_like(acc_ref)
    carry_ref[...] = jnp.zeros_like(carry_ref)
    key = lax.broadcasted_iota(jnp.int32, (tk, q2), 0)
    qloc = lax.broadcasted_iota(jnp.int32, (tk, q2), 1) & (tq - 1)
    diag_valid = key < qloc
    u2 = u_ref[...]

    def block(j, valid):
        kb = k_ref[j]
        heads = range(KV_HEADS)
        z = [_mm(kb, qz_ref[kvh]) for kvh in heads]
        drop = [_softplus(z[kvh]) for kvh in heads]
        log_take = [z[kvh] - drop[kvh] for kvh in heads]
        if valid is not None:
            drop = [jnp.where(valid, d, 0.0) for d in drop]
        cs = [_mm(u2, d.astype(BF16)) for d in drop]
        w = [jnp.exp(log_take[kvh] - (cs[kvh][0:tk] + carry_ref[kvh, 0:1, :])) for kvh in heads]
        if valid is not None:
            w = [jnp.where(valid, x, 0.0) for x in w]
        for kvh in heads:
            acc_ref[kvh] += _mm(vt_ref[j, kvh * HEAD_DIM:(kvh + 1) * HEAD_DIM, :], w[kvh].astype(BF16))
            carry_ref[kvh] += cs[kvh][tk:tk + 8]

    block(i, diag_valid)

    def all_dropped():
        return (jnp.min(carry_ref[...]) >= SB_DROP_DONE).astype(jnp.int32)

    def cond(state):
        t, done = state
        return jnp.logical_and(t < i, done == 0)

    def body(state):
        t, _ = state
        block(i - 1 - t, None)
        return t + 1, all_dropped()

    lax.while_loop(cond, body, (jnp.int32(0), all_dropped()))
    _store_heads_out(o_ref, [acc_ref[kvh] for kvh in range(KV_HEADS)], tq)


def _sb_prompt(qt, k3, vt3, b, tiles):
    tq = k3.shape[1]
    q2 = GROUP * tq
    return pl.pallas_call(
        _sb_prompt_kernel,
        grid=(b, tiles),
        in_specs=_att_specs(tiles, tq) + [pl.BlockSpec((tq + BF16_ROWS, tq), lambda i, j: (0, 0))],
        out_specs=pl.BlockSpec((1, tq, ATT_WIDTH), lambda i, j: (i, j, 0)),
        out_shape=jax.ShapeDtypeStruct((b, tiles * tq, ATT_WIDTH), BF16),
        scratch_shapes=[pltpu.VMEM((KV_HEADS, KV_WIDTH, q2), BF16),
                        pltpu.VMEM((KV_HEADS, HEAD_DIM, q2), F32),
                        pltpu.VMEM((KV_HEADS, 8, q2), F32)],
        compiler_params=_cparams(2),
        name="sb_prompt",
    )(qt, k3, vt3, _suffix_matrix_t(tq))


def _page_specs(layer, slots, page, seq, nseq):
    def spec(p):
        return pl.BlockSpec((None, None, KV_WIDTH, page), lambda b, pt: (layer, pt[b * nseq + seq, p], 0, 0))
    return [spec(p) for p in slots]


def _sample_refs(refs, npages, nseq):
    kp = [refs[s * npages:(s + 1) * npages] for s in range(nseq)]
    vp = [refs[(nseq + s) * npages:(nseq + s + 1) * npages] for s in range(nseq)]
    o_ref, padk_ref, padv_ref = refs[2 * nseq * npages:]
    return kp, vp, o_ref, padk_ref, padv_ref


def _load_new_rows(kn_ref, vn_ref, padk_ref, padv_ref, nseq):
    n_new = kn_ref.shape[2]

    @pl.when(pl.program_id(0) == 0)
    def _():
        padk_ref[...] = jnp.zeros_like(padk_ref)
        padv_ref[...] = jnp.zeros_like(padv_ref)

    for s in range(nseq):
        padk_ref[s, 0:n_new, :] = kn_ref[0, s]
        padv_ref[s, 0:n_new, :] = vn_ref[0, s]


def _sample_queries(q8):
    rows = KV_HEADS * q8.shape[0]
    q = jnp.concatenate([q8] * KV_HEADS, axis=0)
    row = lax.broadcasted_iota(jnp.int32, (rows, KV_WIDTH), 0)
    lane = lax.broadcasted_iota(jnp.int32, (rows, KV_WIDTH), 1)
    return jnp.where((row >> 3) == (lane >> 6), q, 0.0)


def _sample_heads_out(acc):
    n = acc.shape[0] // KV_HEADS
    lane_head = lax.broadcasted_iota(jnp.int32, (n, KV_WIDTH), 1) >> 6
    out = jnp.zeros((n, KV_WIDTH), F32)
    for kvh in range(KV_HEADS):
        out = jnp.where(lane_head == kvh, acc[kvh * n:(kvh + 1) * n, :], out)
    return out


def _sample_call(body, name, q4, k_new, v_new, consts, cache_k, cache_v, page_table, layer):
    nsteps, nseq, rows, _ = q4.shape
    n_new = k_new.shape[2]
    npages = page_table.shape[1]
    page = cache_k.shape[3]
    pages = [spec for s in range(nseq) for spec in _page_specs(layer, range(npages), page, s, nseq)]
    grid_spec = pltpu.PrefetchScalarGridSpec(
        num_scalar_prefetch=1,
        grid=(nsteps,),
        in_specs=[pl.BlockSpec((1, nseq, rows, KV_WIDTH), lambda b, pt: (b, 0, 0, 0)),
                  pl.BlockSpec((1, nseq, n_new, KV_WIDTH), lambda b, pt: (b, 0, 0, 0)),
                  pl.BlockSpec((1, nseq, n_new, KV_WIDTH), lambda b, pt: (b, 0, 0, 0))]
                 + [pl.BlockSpec(c.shape, lambda b, pt: (0, 0)) for c in consts] + pages + pages,
        out_specs=pl.BlockSpec((1, nseq, rows, KV_WIDTH), lambda b, pt: (b, 0, 0, 0)),
        scratch_shapes=[pltpu.VMEM((nseq, page, KV_WIDTH), F32), pltpu.VMEM((nseq, page, KV_WIDTH), F32)])
    return pl.pallas_call(
        functools.partial(body, npages=npages, nseq=nseq),
        grid_spec=grid_spec,
        out_shape=jax.ShapeDtypeStruct(q4.shape, F32),
        compiler_params=_cparams(1),
        name=name,
    )(page_table, q4, k_new, v_new, *consts, *([cache_k] * (nseq * npages)), *([cache_v] * (nseq * npages)))


def _sb_sample_blocks(qz, kt, vt, new_kv, valid_new, umat, acc, carry):
    nseq, ncached = len(kt), len(kt[0])
    nblk = ncached + (new_kv is not None)
    rows, page = carry[0].shape
    work = [(s, i) for s in range(nseq) for i in range(nblk)]
    z = [_mm(qz[s], kt[s][i][...].astype(BF16)) if i < ncached else _mm_nt(qz[s], new_kv[s][0].astype(BF16))
         for s, i in work]
    drop = [_softplus(x) for x in z]
    take = [x - d for x, d in zip(z, drop)]
    drop = [d if i < ncached else jnp.where(valid_new, d, 0.0) for d, (s, i) in zip(drop, work)]
    parts = []
    for d in drop:
        parts.extend(_split2(d))
    cs = _mm(jnp.concatenate(parts, axis=0), umat)
    acc, carry = list(acc), list(carry)
    for i in reversed(range(nblk)):
        for s in range(nseq):
            j = work.index((s, i))
            c = cs[j * 2 * rows:(j + 1) * 2 * rows]
            both = c[0:rows] + c[rows:2 * rows]
            w = jnp.exp(take[j] - (both[:, 0:page] + carry[s]))
            if i < ncached:
                acc[s] = acc[s] + _mm_nt(w.astype(BF16), vt[s][i][...].astype(BF16))
            else:
                w = jnp.where(valid_new, w, 0.0)
                acc[s] = acc[s] + _mm(w.astype(BF16), new_kv[s][1].astype(BF16))
            carry[s] = carry[s] + both[:, page:]
    return acc, carry


def _sb_sample_head_kernel(pt_ref, q_ref, kn_ref, vn_ref, u_ref, *refs, npages, nseq):
    del pt_ref
    kp = [refs[s * npages:(s + 1) * npages] for s in range(nseq)]
    vp = [refs[(nseq + s) * npages:(nseq + s + 1) * npages] for s in range(nseq)]
    acc_ref, carry_ref, padk_ref, padv_ref = refs[2 * nseq * npages:]
    _load_new_rows(kn_ref, vn_ref, padk_ref, padv_ref, nseq)
    n_new = kn_ref.shape[2]
    page = u_ref.shape[0]
    rows = KV_HEADS * q_ref.shape[2]
    row = lax.broadcasted_iota(jnp.int32, (rows, page), 0)
    col = lax.broadcasted_iota(jnp.int32, (rows, page), 1)
    valid_new = col < (row & (n_new - 1))
    qz = [_sample_queries(q_ref[0, s]).astype(BF16) for s in range(nseq)]
    acc, carry = _sb_sample_blocks(
        qz, kp, vp, [(padk_ref[s], padv_ref[s]) for s in range(nseq)], valid_new, u_ref[...],
        [jnp.zeros((rows, KV_WIDTH), F32)] * nseq, [jnp.zeros((rows, page), F32)] * nseq)
    for s in range(nseq):
        acc_ref[0, s] = acc[s]
        carry_ref[0, s] = carry[s]


def _sb_sample_tail_kernel(pt_ref, q_ref, acc_in_ref, carry_in_ref, u_ref, *refs, npages, nseq):
    del pt_ref
    kp = [refs[s * npages:(s + 1) * npages] for s in range(nseq)]
    vp = [refs[(nseq + s) * npages:(nseq + s + 1) * npages] for s in range(nseq)]
    acc_ref = refs[2 * nseq * npages]
    qz = [_sample_queries(q_ref[0, s]).astype(BF16) for s in range(nseq)]
    acc, _ = _sb_sample_blocks(qz, kp, vp, None, None, u_ref[...],
                               [acc_in_ref[0, s] for s in range(nseq)], [carry_in_ref[0, s] for s in range(nseq)])
    for s in range(nseq):
        acc_ref[0, s] = acc[s]


def _sb_sample(q4, k_new, v_new, umat, cache_k, cache_v, page_table, layer, n_eager):
    nsteps, nseq, n, _ = q4.shape
    rows = KV_HEADS * n
    n_new = k_new.shape[2]
    npages = page_table.shape[1]
    page = cache_k.shape[3]
    n_eager = min(n_eager, npages)
    step4 = lambda b, pt: (b, 0, 0, 0)
    q_spec = pl.BlockSpec((1, nseq, n, KV_WIDTH), step4)
    new_spec = pl.BlockSpec((1, nseq, n_new, KV_WIDTH), step4)
    acc_spec = pl.BlockSpec((1, nseq, rows, KV_WIDTH), step4)
    carry_spec = pl.BlockSpec((1, nseq, rows, page), step4)
    u_spec = pl.BlockSpec(umat.shape, lambda b, pt: (0, 0))
    acc_shape = jax.ShapeDtypeStruct((nsteps, nseq, rows, KV_WIDTH), F32)
    carry_shape = jax.ShapeDtypeStruct((nsteps, nseq, rows, page), F32)

    def page_operands(slots):
        specs = [spec for s in range(nseq) for spec in _page_specs(layer, slots, page, s, nseq)]
        count = nseq * len(slots)
        return specs + specs, [cache_k] * count + [cache_v] * count

    specs, operands = page_operands(range(npages - n_eager, npages))
    acc, carry = pl.pallas_call(
        functools.partial(_sb_sample_head_kernel, npages=n_eager, nseq=nseq),
        grid_spec=pltpu.PrefetchScalarGridSpec(
            num_scalar_prefetch=1, grid=(nsteps,),
            in_specs=[q_spec, new_spec, new_spec, u_spec] + specs,
            out_specs=[acc_spec, carry_spec],
            scratch_shapes=[pltpu.VMEM((nseq, page, KV_WIDTH), F32), pltpu.VMEM((nseq, page, KV_WIDTH), F32)]),
        out_shape=[acc_shape, carry_shape],
        compiler_params=_cparams(1),
        name="sb_sample_head",
    )(page_table, q4, k_new, v_new, umat, *operands)
    if npages > n_eager:
        specs, operands = page_operands(range(npages - n_eager))

        def tail(acc_in):
            return pl.pallas_call(
                functools.partial(_sb_sample_tail_kernel, npages=npages - n_eager, nseq=nseq),
                grid_spec=pltpu.PrefetchScalarGridSpec(
                    num_scalar_prefetch=1, grid=(nsteps,),
                    in_specs=[q_spec, acc_spec, carry_spec, u_spec] + specs,
                    out_specs=acc_spec),
                out_shape=acc_shape,
                compiler_params=_cparams(1),
                name="sb_sample_tail",
            )(page_table, q4, acc_in, carry, umat, *operands)

        acc = lax.cond(jnp.min(carry) < SB_DROP_DONE, tail, lambda acc_in: acc_in, acc)
    a6 = acc.reshape(nsteps, nseq, KV_HEADS, n, KV_HEADS, HEAD_DIM)
    return jnp.stack([a6[:, :, h, :, h, :] for h in range(KV_HEADS)], axis=3).reshape(nsteps, nseq, n, KV_WIDTH)


def _suffix_matrix(tk):
    r = np.arange(tk)
    strict = (r[:, None] > r[None, :]).astype(np.float32)
    return jnp.asarray(np.concatenate([strict, np.ones((tk, tk), np.float32)], axis=1), dtype=BF16)


def _topk_rows(scores, n_valid, nb):
    blk = lax.broadcasted_iota(jnp.int32, scores.shape, 0)
    valid = blk < n_valid
    s = jnp.where(valid, scores, NEG_INF)
    rank = jnp.zeros(scores.shape, jnp.int32)
    for m in range(nb):
        other = s[m:m + 1, :]
        beats = (other > s) | ((other == s) & (blk > m))
        rank = rank + jnp.where(beats, 1, 0)
    return jnp.where(valid & (rank < MOBA_TOPK), 1.0, 0.0)


def _topk_lanes(scores, n_valid, nb):
    blk = lax.broadcasted_iota(jnp.int32, scores.shape, 1)
    valid = blk < n_valid
    s = jnp.where(valid, scores, NEG_INF)
    rank = jnp.zeros(scores.shape, jnp.int32)
    for m in range(nb):
        other = s[:, m:m + 1]
        beats = (other > s) | ((other == s) & (blk > m))
        rank = rank + jnp.where(beats, 1, 0)
    return jnp.where(valid & (rank < MOBA_TOPK), 1.0, 0.0)


def _moba_prompt_kernel(qtf_ref, qt_ref, k_ref, vt_ref, km_ref, o_ref,
                        qz_ref, sel_ref, m_ref, l_ref, acc_ref, *, nb):
    qblk = pl.program_id(1)
    tk = k_ref.shape[1]
    q2 = qz_ref.shape[2]
    tq = q2 // GROUP
    _load_head_queries(qt_ref, qz_ref)
    heads = range(KV_HEADS)
    qf = jnp.concatenate(
        [jnp.concatenate([qtf_ref[0, (kvh * GROUP + g) * HEAD_DIM:(kvh * GROUP + g + 1) * HEAD_DIM, :]
                          for kvh in heads], axis=0) for g in range(GROUP)], axis=1)
    q_hi, q_lo = _split2(qf)
    km = km_ref[0]
    lane_head = lax.broadcasted_iota(jnp.int32, km.shape, 1) >> 6
    km_parts = [_split2(jnp.where(lane_head == kvh, km, 0.0)) for kvh in heads]
    scores = [_mm(km_hi, q_hi) + (_mm(km_hi, q_lo) + _mm(km_lo, q_hi)) for km_hi, km_lo in km_parts]
    for kvh in heads:
        sel_ref[kvh] = _topk_rows(scores[kvh], qblk, nb)
    key = lax.broadcasted_iota(jnp.int32, (tk, q2), 0)
    qloc = lax.broadcasted_iota(jnp.int32, (tk, q2), 1) & (tq - 1)
    causal = key <= qloc
    ones_rows = jnp.ones((BF16_ROWS, tk), BF16)

    def weighted(n, kvh, p):
        lhs = jnp.concatenate([vt_ref[n, kvh * HEAD_DIM:(kvh + 1) * HEAD_DIM, :], ones_rows], axis=0)
        res = _mm(lhs, p.astype(BF16))
        return res[0:HEAD_DIM], res[HEAD_DIM:HEAD_DIM + 1]

    kb = k_ref[qblk]
    st = [jnp.where(causal, _mm(kb, qz_ref[kvh]), NEG_INF) for kvh in heads]
    m0 = [jnp.max(s, axis=0, keepdims=True) for s in st]
    p = [jnp.exp2(st[kvh] - m0[kvh]) for kvh in heads]
    for kvh in heads:
        m_ref[kvh] = m0[kvh]
        acc_ref[kvh], l_ref[kvh] = weighted(qblk, kvh, p[kvh])

    def body(n, c):
        kb = k_ref[n]
        bias = [jnp.where(sel_ref[kvh, pl.ds(n, 1), :] > 0.5, 0.0, NEG_INF) for kvh in heads]
        st = [_mm(kb, qz_ref[kvh]) for kvh in heads]
        m_old = [m_ref[kvh] for kvh in heads]
        m_new = [jnp.maximum(m_old[kvh], jnp.max(st[kvh], axis=0, keepdims=True) + bias[kvh]) for kvh in heads]
        p = [jnp.exp2(st[kvh] - (m_new[kvh] - bias[kvh])) for kvh in heads]
        alpha = [jnp.exp2(m_old[kvh] - m_new[kvh]) for kvh in heads]
        pv = [weighted(n, kvh, p[kvh]) for kvh in heads]
        for kvh in heads:
            l_ref[kvh] = alpha[kvh] * l_ref[kvh] + pv[kvh][1]
            acc_ref[kvh] = alpha[kvh] * acc_ref[kvh] + pv[kvh][0]
            m_ref[kvh] = m_new[kvh]
        return c

    lax.fori_loop(0, qblk, body, 0)
    _store_heads_out(o_ref, [acc_ref[kvh] / l_ref[kvh] for kvh in heads], tq)


def _moba_prompt(qtf, qt, k3, vt3, kmean, b, tiles):
    tq = k3.shape[1]
    q2 = GROUP * tq
    return pl.pallas_call(
        functools.partial(_moba_prompt_kernel, nb=tiles),
        grid=(b, tiles),
        in_specs=[pl.BlockSpec((1, ATT_WIDTH, tq), lambda i, j: (i * tiles + j, 0, 0))] + _att_specs(tiles, tq)
                 + [pl.BlockSpec((1, tiles, KV_WIDTH), lambda i, j: (i, 0, 0))],
        out_specs=pl.BlockSpec((1, tq, ATT_WIDTH), lambda i, j: (i, j, 0)),
        out_shape=jax.ShapeDtypeStruct((b, tiles * tq, ATT_WIDTH), BF16),
        scratch_shapes=[pltpu.VMEM((KV_HEADS, KV_WIDTH, q2), BF16), pltpu.VMEM((KV_HEADS, tiles, q2), F32),
                        pltpu.VMEM((KV_HEADS, 1, q2), F32), pltpu.VMEM((KV_HEADS, 1, q2), F32),
                        pltpu.VMEM((KV_HEADS, HEAD_DIM, q2), F32)],
        compiler_params=_cparams(2),
        name="moba_prompt",
    )(qtf, qt, k3, vt3, kmean)


def _moba_sample_kernel(pt_ref, q_ref, kn_ref, vn_ref, *refs, npages, nseq):
    del pt_ref
    kp, vp, o_ref, padk_ref, padv_ref = _sample_refs(refs, npages, nseq)
    _load_new_rows(kn_ref, vn_ref, padk_ref, padv_ref, nseq)
    n_new = kn_ref.shape[2]
    page = padk_ref.shape[1]
    per_blk = MOBA_BLOCK // page
    nb_past = npages // per_blk
    rows = KV_HEADS * q_ref.shape[2]
    seqs = range(nseq)
    qf = [_sample_queries(q_ref[0, s]) for s in seqs]
    qz = [(q * ATT_SCALE).astype(BF16) for q in qf]
    ones = jnp.ones((page, LANES), BF16)
    lane = lax.broadcasted_iota(jnp.int32, (KV_WIDTH, LANES), 1)
    k_parts = [[_split2(kp[s][pg][...]) for pg in range(npages)] for s in seqs]
    k_sums = [[_mm(hi, ones) + _mm(lo, ones) for hi, lo in k_parts[s]] for s in seqs]
    km = []
    for s in seqs:
        km_s = jnp.zeros((KV_WIDTH, LANES), F32)
        for blk in range(nb_past):
            tot = k_sums[s][blk * per_blk]
            for r in range(1, per_blk):
                tot = tot + k_sums[s][blk * per_blk + r]
            km_s = jnp.where(lane == blk, tot * (1.0 / MOBA_BLOCK), km_s)
        km.append(km_s)
    sel = []
    for s in seqs:
        q_hi, q_lo = _split2(qf[s])
        km_hi, km_lo = _split2(km[s])
        scores = _mm(q_hi, km_hi) + (_mm(q_hi, km_lo) + _mm(q_lo, km_hi))
        sel.append(_topk_lanes(scores, nb_past, nb_past))
    row = lax.broadcasted_iota(jnp.int32, (rows, page), 0)
    col = lax.broadcasted_iota(jnp.int32, (rows, page), 1)
    visible = (col <= (row & (n_new - 1))) & (col < n_new)
    logits = [[jnp.where(sel[s][:, pg // per_blk:pg // per_blk + 1] > 0.5,
                         _mm(qz[s], kp[s][pg][...].astype(BF16)), NEG_INF) for pg in range(npages)]
              + [jnp.where(visible, _mm_nt(qz[s], padk_ref[s].astype(BF16)), NEG_INF)] for s in seqs]
    m = []
    for s in seqs:
        m_el = logits[s][0]
        for x in logits[s][1:]:
            m_el = jnp.maximum(m_el, x)
        m.append(jnp.max(m_el, axis=1, keepdims=True))
    acc = [jnp.zeros((rows, KV_WIDTH), F32) for _ in seqs]
    l_el = [jnp.zeros((rows, page), F32) for _ in seqs]
    for pg in range(npages + 1):
        for s in seqs:
            p = jnp.exp(logits[s][pg] - m[s])
            l_el[s] = l_el[s] + p
            if pg < npages:
                acc[s] = acc[s] + _mm_nt(p.astype(BF16), vp[s][pg][...].astype(BF16))
            else:
                acc[s] = acc[s] + _mm(p.astype(BF16), padv_ref[s].astype(BF16))
    for s in seqs:
        o_ref[0, s] = _sample_heads_out(acc[s] / jnp.sum(l_el[s], axis=1, keepdims=True))


def _merge_kernel(x_ref, a_ref, b_ref, c_ref, g_ref, wb_ref, wo_ref, o_ref):
    d = x_ref.shape[1]
    m = g_ref[:, 0:d] * _mm(a_ref[...].astype(BF16), wb_ref[0])
    m = m + g_ref[:, d:2 * d] * _mm(b_ref[...].astype(BF16), wb_ref[1])
    m = m + g_ref[:, 2 * d:3 * d] * _mm(c_ref[...].astype(BF16), wb_ref[2])
    o_ref[...] = x_ref[...] + _mm(m.astype(BF16), wo_ref[...])


def _merge(x, a, b, c, gate, wb, wo, tm, layer):
    n, d = x.shape
    row = lambda i: (i, 0)
    return pl.pallas_call(
        _merge_kernel,
        grid=(n // tm,),
        in_specs=[pl.BlockSpec((tm, d), row), pl.BlockSpec((tm, ATT_WIDTH), row), pl.BlockSpec((tm, ATT_WIDTH), row),
                  pl.BlockSpec((tm, ATT_WIDTH), row), pl.BlockSpec((tm, N_BRANCH * d), row),
                  _layer_spec(wb, layer), _layer_spec(wo, layer)],
        out_specs=pl.BlockSpec((tm, d), row),
        out_shape=jax.ShapeDtypeStruct((n, d), F32),
        compiler_params=_cparams(1, VMEM_LIMIT),
        name="merge",
    )(x, a, b, c, gate, wb, wo)


def _mlp_ple_kernel(x_ref, p_ref, gm_ref, wu_ref, wd_ref, gp_ref, wg_ref, wp_ref, o_ref, *, f_chunk):
    x = x_ref[...]
    ms = jnp.mean(x * x, axis=-1, keepdims=True)
    h = (x * lax.rsqrt(ms + RMS_EPS) * gm_ref[...]).astype(BF16)
    d_ff = wu_ref.shape[1]
    y = x
    for f0 in range(0, d_ff, f_chunk):
        hid = jnp.maximum(_mm(h, wu_ref[:, f0:f0 + f_chunk]), 0.0)
        y = y + _mm((hid * hid).astype(BF16), wd_ref[f0:f0 + f_chunk, :])
    ms2 = jnp.mean(y * y, axis=-1, keepdims=True)
    h2 = (y * lax.rsqrt(ms2 + RMS_EPS) * gp_ref[...]).astype(BF16)
    gate = _sigmoid(_mm(h2, wg_ref[...]))
    o_ref[...] = y + gate * _mm(p_ref[...].astype(BF16), wp_ref[...])


def _mlp_ple(x, p, gm, wu, wd, gp, wg, wp, tm, layer):
    n, d = x.shape
    row = lambda i: (i, 0)
    return pl.pallas_call(
        functools.partial(_mlp_ple_kernel, f_chunk=512),
        grid=(n // tm,),
        in_specs=[pl.BlockSpec((tm, d), row), pl.BlockSpec((None, tm, p.shape[2]), lambda i: (layer, i, 0)),
                  _layer_spec(gm, layer), _layer_spec(wu, layer), _layer_spec(wd, layer),
                  _layer_spec(gp, layer), _layer_spec(wg, layer), _layer_spec(wp, layer)],
        out_specs=pl.BlockSpec((tm, d), row),
        out_shape=jax.ShapeDtypeStruct((n, d), F32),
        compiler_params=_cparams(1, VMEM_LIMIT),
        name="mlp_ple",
    )(x, p, gm, wu, wd, gp, wg, wp)


def _rope_tables(pos):
    half = ROT_DIM // 2
    inv = ROPE_THETA ** (-jnp.arange(half, dtype=F32) / half)
    ang = pos.astype(F32)[:, None] * inv[None, :]
    cos, sin = jnp.cos(ang), jnp.sin(ang)
    t = pos.shape[0]
    c64 = jnp.concatenate([cos, cos, jnp.ones((t, HEAD_DIM - ROT_DIM), F32)], axis=1)
    s64 = jnp.concatenate([-sin, sin, jnp.zeros((t, HEAD_DIM - ROT_DIM), F32)], axis=1)
    reps = LANES // HEAD_DIM
    return jnp.tile(c64, (1, reps)), jnp.tile(s64, (1, reps))


def _head_group_matrix():
    r = np.arange(ATT_WIDTH) // HEAD_DIM
    return jnp.asarray((r[:, None] == r[None, :]).astype(np.float32), dtype=BF16)


def kernel(x_prompt, x_sample, p_prompt, p_sample, cache_sb_k, cache_sb_v, cache_moba_k, cache_moba_v,
           state_pool, page_table, norm_mix, w_in, pool_w, pool_scale, moba_q_norm, moba_k_norm,
           w_branch, w_out, norm_mlp, w_up, w_down, norm_ple, w_ple_gate, w_ple):
    depth = w_in.shape[0]
    bp, seq, d = x_prompt.shape
    bs, t_new, _ = x_sample.shape
    n_p, n_s = bp * seq, bs * t_new
    n_phys, page = cache_sb_k.shape[1], cache_sb_k.shape[2]
    past_len = page_table.shape[1] * page
    tiles = seq // ATT_TILE
    assert seq % ATT_TILE == 0 and past_len % MOBA_BLOCK == 0 and MOBA_BLOCK % page == 0 and page == LANES
    assert t_new & (t_new - 1) == 0 and GROUP * t_new == 8 and past_len >= POOL_HIST >= t_new
    assert past_len // MOBA_BLOCK <= LANES

    w_in_b, w_branch_b = w_in.astype(BF16), w_branch.astype(BF16)
    w_out_b, w_up_b, w_down_b = w_out.astype(BF16), w_up.astype(BF16), w_down.astype(BF16)
    w_pg_b, w_ple_b, pool_w_b = w_ple_gate.astype(BF16), w_ple.astype(BF16), pool_w.astype(BF16)
    vec = lambda a: a[:, None, :]
    norm_mix3, norm_mlp3, norm_ple3, pool_scale3 = vec(norm_mix), vec(norm_mlp), vec(norm_ple), vec(pool_scale)
    qn = vec(jnp.tile(moba_q_norm, (1, N_HEADS)))
    kn = vec(jnp.tile(moba_k_norm, (1, KV_HEADS)))
    gmat = _head_group_matrix()

    tm_p = ATT_TILE
    tm_s = 256 if n_s % 256 == 0 else n_s
    cos_p, sin_p = _rope_tables(jnp.arange(seq))
    cos_s, sin_s = _rope_tables(jnp.tile(past_len + jnp.arange(t_new), bs))
    caches = [c.transpose(0, 1, 3, 4, 2).reshape(depth, n_phys, KV_WIDTH, page)
              for c in (cache_sb_k, cache_sb_v, cache_moba_k, cache_moba_v)]
    pp = p_prompt.reshape(depth, n_p, -1)
    psm = p_sample.reshape(depth, n_s, -1)

    xp = x_prompt.reshape(n_p, d)
    xs = x_sample.reshape(n_s, d)
    kv_stacks = None
    stack_shape = (depth, bp, KV_WIDTH, seq)
    pool_p = []
    outs_s = [[] for _ in range(5)]
    nseq = 2 if bs % 2 == 0 else 1

    def to_q4(q):
        return q.reshape(bs, t_new, KV_HEADS, GROUP, HEAD_DIM).transpose(0, 3, 1, 2, 4).reshape(
            bs // nseq, nseq, GROUP * t_new, KV_WIDTH)

    def from_q4(r):
        return r.reshape(bs, GROUP, t_new, KV_HEADS, HEAD_DIM).transpose(0, 2, 3, 1, 4).reshape(n_s, ATT_WIDTH)

    def new_rows(a):
        return a.reshape(bs // nseq, nseq, t_new, KV_WIDTH)

    sb_umat = _suffix_matrix(page)

    for i in range(depth):
        layer_w = (norm_mix3, w_in_b, qn, kn)
        post_w = (norm_mlp3, w_up_b, w_down_b, norm_ple3, w_pg_b, w_ple_b)

        (u, qt_sb, k_sb_b, vt_sb, qt_mf, qt_mb, k_mb_b, vt_mb, kmean, gate, *kv_stacks) = _inproj(
            xp, *layer_w, cos_p, sin_p, gmat, tm_p, i, kv_stacks, stack_shape)
        k3 = lambda a: a.reshape(n_p // ATT_TILE, ATT_TILE, KV_WIDTH)
        u3 = u.reshape(bp, seq, ATT_WIDTH)
        a = _pool_prompt(u3, pool_w_b, pool_scale3, min(512, seq), i).reshape(n_p, ATT_WIDTH)
        b = _sb_prompt(qt_sb, k3(k_sb_b), vt_sb, bp, tiles).reshape(n_p, ATT_WIDTH)
        c = _moba_prompt(qt_mf, qt_mb, k3(k_mb_b), vt_mb, kmean.reshape(bp, tiles, KV_WIDTH), bp, tiles).reshape(
            n_p, ATT_WIDTH)
        xp = _merge(xp, a, b, c, gate, w_branch_b, w_out_b, tm_p, i)
        xp = _mlp_ple(xp, pp, *post_w, tm_p, i)
        pool_p.append(u3[:, seq - POOL_HIST:])

        (u, q_sb, k_sb, v_sb, q_mb, k_mb, v_mb, gate) = _inproj(xs, *layer_w, cos_s, sin_s, gmat, tm_s, i)
        r3 = lambda a: a.reshape(bs, t_new, a.shape[-1])
        st_t = state_pool[i].transpose(1, 0, 2)
        a = _pool_sample(st_t, r3(u).transpose(1, 0, 2), pool_w_b, pool_scale3, past_len, i).transpose(1, 0, 2).reshape(
            n_s, ATT_WIDTH)
        b = from_q4(_sb_sample(to_q4(q_sb), new_rows(k_sb), new_rows(v_sb), sb_umat, caches[0], caches[1],
                               page_table, i, SB_EAGER_PAGES))
        c = from_q4(_sample_call(_moba_sample_kernel, "moba_sample", to_q4(q_mb), new_rows(k_mb), new_rows(v_mb), [],
                                 caches[2], caches[3], page_table, i))
        xs = _merge(xs, a, b, c, gate, w_branch_b, w_out_b, tm_s, i)
        xs = _mlp_ple(xs, psm, *post_w, tm_s, i)
        pool_state = jnp.concatenate([state_pool[i][:, t_new:], r3(u)], axis=1)
        for lst, val in zip(outs_s, (k_sb, v_sb, k_mb, v_mb, pool_state)):
            lst.append(val)

    def kv_prompt(stack):
        return stack.reshape(depth, bp, KV_HEADS, HEAD_DIM, seq).transpose(0, 1, 4, 2, 3)

    def kv_sample(lst):
        return jnp.stack(lst).reshape(depth, bs, t_new, KV_HEADS, HEAD_DIM)

    return (xp.reshape(bp, seq, d), xs.reshape(bs, t_new, d),
            kv_prompt(kv_stacks[0]), kv_prompt(kv_stacks[1]), kv_prompt(kv_stacks[2]), kv_prompt(kv_stacks[3]),
            jnp.stack(pool_p),
            kv_sample(outs_s[0]), kv_sample(outs_s[1]), kv_sample(outs_s[2]), kv_sample(outs_s[3]),
            jnp.stack(outs_s[4]))
```

```python
import functools

import jax
import jax.numpy as jnp
import numpy as np
from jax import lax
from jax.experimental import pallas as pl
from jax.experimental.pallas import tpu as pltpu

F32 = jnp.float32
BF16 = jnp.bfloat16

HEAD_DIM = 64
KV_HEADS = 4
GROUP = 2
N_HEADS = KV_HEADS * GROUP
ATT_WIDTH = N_HEADS * HEAD_DIM
KV_WIDTH = KV_HEADS * HEAD_DIM
ATT_SCALE = HEAD_DIM ** -0.5
LOG2_E = 1.4426950408889634
POOL_WINDOWS = (2, 4, 8, 16)
POOL_GROUP = 128
POOL_HIST = max(POOL_WINDOWS) - 1
ROPE_THETA = 500000.0
ROT_DIM = HEAD_DIM // 4
MOBA_BLOCK = 256
MOBA_TOPK = 3
N_BRANCH = 3
RMS_EPS = 1e-6
NEG_INF = float("-inf")
SB_DROP_DONE = 104.0
SB_EAGER_PAGES = 3

LANES = 128
BF16_ROWS = 16
ATT_TILE = MOBA_BLOCK
VMEM_LIMIT = 56 * 1024 * 1024


def _cparams(n_axes, vmem=None):
    return pltpu.CompilerParams(dimension_semantics=("arbitrary",) * n_axes, vmem_limit_bytes=vmem)


def _layer_spec(arr, layer, pipeline_mode=None):
    zeros = (0,) * (arr.ndim - 1)
    return pl.BlockSpec((None,) + arr.shape[1:], lambda *_: (layer,) + zeros, pipeline_mode=pipeline_mode)


def _mm(a, b):
    return jnp.dot(a, b, preferred_element_type=F32)


def _mm_nt(a, b):
    return lax.dot_general(a, b, (((1,), (1,)), ((), ())), preferred_element_type=F32)


def _split2(x):
    hi = x.astype(BF16)
    lo = (x - hi.astype(F32)).astype(BF16)
    return hi, lo


def _sigmoid(x):
    return 1.0 / (1.0 + jnp.exp(-x))


def _softplus(z):
    return jnp.maximum(z, 0.0) + jnp.log(1.0 + jnp.exp(jnp.minimum(z, -z)))


def _inproj_kernel(x_ref, gn_ref, w_ref, qn_ref, kn_ref, cos_ref, sin_ref, gmat_ref, *refs, prompt, n_alias):
    out_refs = refs[n_alias:]
    x = x_ref[...]
    ms = jnp.mean(x * x, axis=-1, keepdims=True)
    h = (x * lax.rsqrt(ms + RMS_EPS) * gn_ref[...]).astype(BF16)

    def proj(lo, hi):
        return _mm(h, w_ref[:, lo:hi])

    def head_norm_rope(y, g_ref):
        width = y.shape[1]
        gm = gmat_ref[0:width, 0:width]
        p0, p1 = _split2(y * y)
        ms_h = (_mm(p0, gm) + _mm(p1, gm)) * (1.0 / HEAD_DIM)
        yn = y * lax.rsqrt(ms_h + RMS_EPS) * g_ref[...]
        reps = width // LANES
        c = jnp.concatenate([cos_ref[...]] * reps, axis=1)
        s = jnp.concatenate([sin_ref[...]] * reps, axis=1)
        lane = lax.broadcasted_iota(jnp.int32, yn.shape, 1)
        first_half = (lane & (HEAD_DIM - 1)) < (ROT_DIM // 2)
        partner = jnp.where(first_half, pltpu.roll(yn, width - ROT_DIM // 2, 1), pltpu.roll(yn, ROT_DIM // 2, 1))
        return yn * c + partner * s

    o = 0
    u = proj(o, o + ATT_WIDTH)
    o += ATT_WIDTH
    q_sb = proj(o, o + ATT_WIDTH) * ATT_SCALE
    o += ATT_WIDTH
    k_sb = proj(o, o + KV_WIDTH)
    o += KV_WIDTH
    v_sb = proj(o, o + KV_WIDTH)
    o += KV_WIDTH
    q_mb = head_norm_rope(proj(o, o + ATT_WIDTH), qn_ref)
    o += ATT_WIDTH
    k_mb = head_norm_rope(proj(o, o + KV_WIDTH), kn_ref)
    o += KV_WIDTH
    v_mb = proj(o, o + KV_WIDTH)
    o += KV_WIDTH
    if prompt:
        (u_ref, qsb_t, ksb_b, vsb_tb, qmb_tf, qmb_t, kmb_b, vmb_tb, kmean_ref, gate_ref,
         ksb_t, vsb_t, kmb_t, vmb_t) = out_refs
        qsb_t[0] = q_sb.T.astype(BF16)
        ksb_b[...] = k_sb.astype(BF16)
        ksb_t[...] = k_sb.T
        v_t = v_sb.T
        vsb_t[...] = v_t
        vsb_tb[0] = v_t.astype(BF16)
        q_mb_t = q_mb.T
        qmb_tf[0] = q_mb_t
        qmb_t[0] = (q_mb_t * (ATT_SCALE * LOG2_E)).astype(BF16)
        kmb_b[...] = k_mb.astype(BF16)
        kmb_t[...] = k_mb.T
        kmean_ref[0] = jnp.sum(k_mb, axis=0, keepdims=True) * (1.0 / k_mb.shape[0])
        v_t = v_mb.T
        vmb_t[...] = v_t
        vmb_tb[0] = v_t.astype(BF16)
    else:
        (u_ref, qsb_ref, ksb_ref, vsb_ref, qmb_ref, kmb_ref, vmb_ref, gate_ref) = out_refs
        qsb_ref[...] = q_sb
        qmb_ref[...] = q_mb
        ksb_ref[...] = k_sb
        vsb_ref[...] = v_sb
        kmb_ref[...] = k_mb
        vmb_ref[...] = v_mb
    u_ref[...] = u
    gate_w = gate_ref.shape[1]
    chunk = 512
    for c0 in range(0, gate_w, chunk):
        gate_ref[:, c0:c0 + chunk] = _sigmoid(proj(o + c0, o + c0 + chunk)).astype(gate_ref.dtype)


def _inproj(x, gn, w, qn, kn, cos_t, sin_t, gmat, tm, layer, kv_stacks=None, stack_shape=None):
    prompt = stack_shape is not None
    n, d = x.shape
    in_w = w.shape[2]
    gate_w = in_w - (3 * ATT_WIDTH + 4 * KV_WIDTH)
    nper = cos_t.shape[0] // tm
    nt = n // tm
    row = lambda i: (i, 0)
    const = lambda i: (0, 0)

    def flat(width, dtype):
        return jax.ShapeDtypeStruct((n, width), dtype), pl.BlockSpec((tm, width), row)

    def transposed(width, dtype):
        return jax.ShapeDtypeStruct((nt, width, tm), dtype), pl.BlockSpec((1, width, tm), lambda i: (i, 0, 0))

    alias_in, aliases = [], {}
    if prompt:
        tiles = stack_shape[3] // tm
        stacked = (jax.ShapeDtypeStruct(stack_shape, F32),
                   pl.BlockSpec((None, None, KV_WIDTH, tm), lambda i: (layer, i // tiles, 0, i % tiles)))
        outs = [flat(ATT_WIDTH, F32), transposed(ATT_WIDTH, BF16), flat(KV_WIDTH, BF16), transposed(KV_WIDTH, BF16),
                transposed(ATT_WIDTH, F32), transposed(ATT_WIDTH, BF16), flat(KV_WIDTH, BF16),
                transposed(KV_WIDTH, BF16),
                (jax.ShapeDtypeStruct((nt, 1, KV_WIDTH), F32), pl.BlockSpec((1, 1, KV_WIDTH), lambda i: (i, 0, 0))),
                flat(gate_w, BF16), stacked, stacked, stacked, stacked]
        if kv_stacks is not None:
            alias_in = list(kv_stacks)
            aliases = {8 + k: len(outs) - 4 + k for k in range(4)}
    else:
        outs = [flat(ATT_WIDTH, F32), flat(ATT_WIDTH, F32), flat(KV_WIDTH, F32), flat(KV_WIDTH, F32),
                flat(ATT_WIDTH, F32), flat(KV_WIDTH, F32), flat(KV_WIDTH, F32), flat(gate_w, BF16)]
    return pl.pallas_call(
        functools.partial(_inproj_kernel, prompt=prompt, n_alias=len(alias_in)),
        grid=(nt,),
        in_specs=[pl.BlockSpec((tm, d), row), _layer_spec(gn, layer), _layer_spec(w, layer),
                  _layer_spec(qn, layer), _layer_spec(kn, layer),
                  pl.BlockSpec((tm, LANES), lambda i: (i % nper, 0)), pl.BlockSpec((tm, LANES), lambda i: (i % nper, 0)),
                  pl.BlockSpec((ATT_WIDTH, ATT_WIDTH), const)]
                 + [pl.BlockSpec(memory_space=pl.ANY)] * len(alias_in),
        out_specs=[s for _, s in outs],
        out_shape=[s for s, _ in outs],
        input_output_aliases=aliases,
        compiler_params=_cparams(1, VMEM_LIMIT),
        name="inproj_prompt" if prompt else "inproj_sample",
    )(x, gn, w, qn, kn, cos_t, sin_t, gmat, *alias_in)


def _pool_prompt_kernel(cur_ref, prev_ref, pw_ref, ps_ref, o_ref, ext_ref, *, ts):
    t = pl.program_id(1)
    hist = ext_ref.shape[0] - ts
    ext_ref[0:hist, :] = jnp.where(t == 0, 0.0, prev_ref[0])
    ext_ref[hist:, :] = cur_ref[0]
    pos = t * ts + lax.broadcasted_iota(jnp.int32, (ts, POOL_GROUP), 0)
    for g, w in enumerate(POOL_WINDOWS):
        lo, hi = g * POOL_GROUP, (g + 1) * POOL_GROUP
        ws = ext_ref[hist:hist + ts, lo:hi]
        for k in range(1, w):
            ws = ws + ext_ref[hist - k:hist - k + ts, lo:hi]
        cnt = jnp.minimum(pos + 1, w).astype(F32)
        dlt = ws / cnt - ext_ref[hist:hist + ts, lo:hi]
        o_ref[0, :, lo:hi] = (_mm(dlt.astype(BF16), pw_ref[g]) * ps_ref[:, lo:hi]).astype(o_ref.dtype)


def _pool_prompt(u, pw, ps, ts, layer):
    b, s, c = u.shape
    hist = 16
    ratio = ts // hist
    return pl.pallas_call(
        functools.partial(_pool_prompt_kernel, ts=ts),
        grid=(b, s // ts),
        in_specs=[pl.BlockSpec((1, ts, c), lambda i, t: (i, t, 0)),
                  pl.BlockSpec((1, hist, c), lambda i, t: (i, jnp.maximum(t * ratio - 1, 0), 0)),
                  _layer_spec(pw, layer), _layer_spec(ps, layer)],
        out_specs=pl.BlockSpec((1, ts, c), lambda i, t: (i, t, 0)),
        out_shape=jax.ShapeDtypeStruct((b, s, c), BF16),
        scratch_shapes=[pltpu.VMEM((ts + hist, c), F32)],
        compiler_params=_cparams(2),
        name="pool_prompt",
    )(u, u, pw, ps)


def _pool_sample_kernel(st_ref, u_ref, pw_ref, ps_ref, o_ref, *, start_pos):
    n_new = u_ref.shape[0]

    def row(idx, lo, hi):
        return st_ref[idx, :, lo:hi] if idx < POOL_HIST else u_ref[idx - POOL_HIST, :, lo:hi]

    for j in range(n_new):
        for g, w in enumerate(POOL_WINDOWS):
            lo, hi = g * POOL_GROUP, (g + 1) * POOL_GROUP
            ws = row(POOL_HIST + j, lo, hi)
            for k in range(1, w):
                ws = ws + row(POOL_HIST + j - k, lo, hi)
            cnt = float(min(start_pos + j + 1, w))
            dlt = ws / cnt - row(POOL_HIST + j, lo, hi)
            o_ref[j, :, lo:hi] = _mm(dlt.astype(BF16), pw_ref[g]) * ps_ref[:, lo:hi]


def _pool_sample(st_t, u_t, pw, ps, start_pos, layer):
    t, b, c = u_t.shape
    full = lambda shape: pl.BlockSpec(shape, lambda i: (0,) * len(shape))
    return pl.pallas_call(
        functools.partial(_pool_sample_kernel, start_pos=start_pos),
        grid=(1,),
        in_specs=[full(st_t.shape), full(u_t.shape), _layer_spec(pw, layer), _layer_spec(ps, layer)],
        out_specs=full((t, b, c)),
        out_shape=jax.ShapeDtypeStruct((t, b, c), F32),
        compiler_params=_cparams(1),
        name="pool_sample",
    )(st_t, u_t, pw, ps)


def _load_head_queries(qt_ref, qz_ref):
    tq = qt_ref.shape[2]

    @pl.when((pl.program_id(0) == 0) & (pl.program_id(1) == 0))
    def _():
        qz_ref[...] = jnp.zeros_like(qz_ref)

    for kvh in range(KV_HEADS):
        for g in range(GROUP):
            h = kvh * GROUP + g
            qz_ref[kvh, kvh * HEAD_DIM:(kvh + 1) * HEAD_DIM, g * tq:(g + 1) * tq] = (
                qt_ref[0, h * HEAD_DIM:(h + 1) * HEAD_DIM, :])


def _store_heads_out(o_ref, per_head, tq):
    out_t = jnp.concatenate(
        [per_head[kvh][:, g * tq:(g + 1) * tq] for kvh in range(KV_HEADS) for g in range(GROUP)], axis=0)
    o_ref[0] = out_t.T.astype(o_ref.dtype)


def _att_specs(tiles, tq):
    return [pl.BlockSpec((1, ATT_WIDTH, tq), lambda i, j: (i * tiles + j, 0, 0)),
            pl.BlockSpec((tiles, tq, KV_WIDTH), lambda i, j: (i, 0, 0)),
            pl.BlockSpec((tiles, KV_WIDTH, tq), lambda i, j: (i, 0, 0))]


def _suffix_matrix_t(tk):
    r = np.arange(tk)
    later = (r[None, :] > r[:, None]).astype(np.float32)
    return jnp.asarray(np.concatenate([later, np.ones((BF16_ROWS, tk), np.float32)], axis=0), dtype=BF16)


def _sb_prompt_kernel(qt_ref, k_ref, vt_ref, u_ref, o_ref, qz_ref, acc_ref, carry_ref):
    i = pl.program_id(1)
    tk = k_ref.shape[1]
    q2 = qz_ref.shape[2]
    tq = q2 // GROUP
    _load_head_queries(qt_ref, qz_ref)
    acc_ref[...] = jnp.zeros_like(acc_ref)
    carry_ref[...] = jnp.zeros_like(carry_ref)
    key = lax.broadcasted_iota(jnp.int32, (tk, q2), 0)
    qloc = lax.broadcasted_iota(jnp.int32, (tk, q2), 1) & (tq - 1)
    diag_valid = key < qloc
    u2 = u_ref[...]

    def block(j, valid):
        kb = k_ref[j]
        heads = range(KV_HEADS)
        z = [_mm(kb, qz_ref[kvh]) for kvh in heads]
        drop = [_softplus(z[kvh]) for kvh in heads]
        log_take = [z[kvh] - drop[kvh] for kvh in heads]
        if valid is not None:
            drop = [jnp.where(valid, d, 0.0) for d in drop]
        cs = [_mm(u2, d.astype(BF16)) for d in drop]
        w = [jnp.exp(log_take[kvh] - (cs[kvh][0:tk] + carry_ref[kvh, 0:1, :])) for kvh in heads]
        if valid is not None:
            w = [jnp.where(valid, x, 0.0) for x in w]
        for kvh in heads:
            acc_ref[kvh] += _mm(vt_ref[j, kvh * HEAD_DIM:(kvh + 1) * HEAD_DIM, :], w[kvh].astype(BF16))
            carry_ref[kvh] += cs[kvh][tk:tk + 8]

    block(i, diag_valid)

    def all_dropped():
        return (jnp.min(carry_ref[...]) >= SB_DROP_DONE).astype(jnp.int32)

    def cond(state):
        t, done = state
        return jnp.logical_and(t < i, done == 0)

    def body(state):
        t, _ = state
        block(i - 1 - t, None)
        return t + 1, all_dropped()

    lax.while_loop(cond, body, (jnp.int32(0), all_dropped()))
    _store_heads_out(o_ref, [acc_ref[kvh] for kvh in range(KV_HEADS)], tq)


def _sb_prompt(qt, k3, vt3, b, tiles):
    tq = k3.shape[1]
    q2 = GROUP * tq
    return pl.pallas_call(
        _sb_prompt_kernel,
        grid=(b, tiles),
        in_specs=_att_specs(tiles, tq) + [pl.BlockSpec((tq + BF16_ROWS, tq), lambda i, j: (0, 0))],
        out_specs=pl.BlockSpec((1, tq, ATT_WIDTH), lambda i, j: (i, j, 0)),
        out_shape=jax.ShapeDtypeStruct((b, tiles * tq, ATT_WIDTH), BF16),
        scratch_shapes=[pltpu.VMEM((KV_HEADS, KV_WIDTH, q2), BF16),
                        pltpu.VMEM((KV_HEADS, HEAD_DIM, q2), F32),
                        pltpu.VMEM((KV_HEADS, 8, q2), F32)],
        compiler_params=_cparams(2),
        name="sb_prompt",
    )(qt, k3, vt3, _suffix_matrix_t(tq))


def _page_specs(layer, slots, page, seq, nseq):
    def spec(p):
        return pl.BlockSpec((None, None, KV_WIDTH, page), lambda b, pt: (layer, pt[b * nseq + seq, p], 0, 0))
    return [spec(p) for p in slots]


def _sample_refs(refs, npages, nseq):
    kp = [refs[s * npages:(s + 1) * npages] for s in range(nseq)]
    vp = [refs[(nseq + s) * npages:(nseq + s + 1) * npages] for s in range(nseq)]
    o_ref, padk_ref, padv_ref = refs[2 * nseq * npages:]
    return kp, vp, o_ref, padk_ref, padv_ref


def _load_new_rows(kn_ref, vn_ref, padk_ref, padv_ref, nseq):
    n_new = kn_ref.shape[2]

    @pl.when(pl.program_id(0) == 0)
    def _():
        padk_ref[...] = jnp.zeros_like(padk_ref)
        padv_ref[...] = jnp.zeros_like(padv_ref)

    for s in range(nseq):
        padk_ref[s, 0:n_new, :] = kn_ref[0, s]
        padv_ref[s, 0:n_new, :] = vn_ref[0, s]


def _sample_queries(q8):
    rows = KV_HEADS * q8.shape[0]
    q = jnp.concatenate([q8] * KV_HEADS, axis=0)
    row = lax.broadcasted_iota(jnp.int32, (rows, KV_WIDTH), 0)
    lane = lax.broadcasted_iota(jnp.int32, (rows, KV_WIDTH), 1)
    return jnp.where((row >> 3) == (lane >> 6), q, 0.0)


def _sample_heads_out(acc):
    n = acc.shape[0] // KV_HEADS
    lane_head = lax.broadcasted_iota(jnp.int32, (n, KV_WIDTH), 1) >> 6
    out = jnp.zeros((n, KV_WIDTH), F32)
    for kvh in range(KV_HEADS):
        out = jnp.where(lane_head == kvh, acc[kvh * n:(kvh + 1) * n, :], out)
    return out


def _sample_call(body, name, q4, k_new, v_new, consts, cache_k, cache_v, page_table, layer):
    nsteps, nseq, rows, _ = q4.shape
    n_new = k_new.shape[2]
    npages = page_table.shape[1]
    page = cache_k.shape[3]
    pages = [spec for s in range(nseq) for spec in _page_specs(layer, range(npages), page, s, nseq)]
    grid_spec = pltpu.PrefetchScalarGridSpec(
        num_scalar_prefetch=1,
        grid=(nsteps,),
        in_specs=[pl.BlockSpec((1, nseq, rows, KV_WIDTH), lambda b, pt: (b, 0, 0, 0)),
                  pl.BlockSpec((1, nseq, n_new, KV_WIDTH), lambda b, pt: (b, 0, 0, 0)),
                  pl.BlockSpec((1, nseq, n_new, KV_WIDTH), lambda b, pt: (b, 0, 0, 0))]
                 + [pl.BlockSpec(c.shape, lambda b, pt: (0, 0)) for c in consts] + pages + pages,
        out_specs=pl.BlockSpec((1, nseq, rows, KV_WIDTH), lambda b, pt: (b, 0, 0, 0)),
        scratch_shapes=[pltpu.VMEM((nseq, page, KV_WIDTH), F32), pltpu.VMEM((nseq, page, KV_WIDTH), F32)])
    return pl.pallas_call(
        functools.partial(body, npages=npages, nseq=nseq),
        grid_spec=grid_spec,
        out_shape=jax.ShapeDtypeStruct(q4.shape, F32),
        compiler_params=_cparams(1),
        name=name,
    )(page_table, q4, k_new, v_new, *consts, *([cache_k] * (nseq * npages)), *([cache_v] * (nseq * npages)))


def _sb_sample_blocks(qz, kt, vt, new_kv, valid_new, umat, acc, carry):
    nseq, ncached = len(kt), len(kt[0])
    nblk = ncached + (new_kv is not None)
    rows, page = carry[0].shape
    work = [(s, i) for s in range(nseq) for i in range(nblk)]
    z = [_mm(qz[s], kt[s][i][...].astype(BF16)) if i < ncached else _mm_nt(qz[s], new_kv[s][0].astype(BF16))
         for s, i in work]
    drop = [_softplus(x) for x in z]
    take = [x - d for x, d in zip(z, drop)]
    drop = [d if i < ncached else jnp.where(valid_new, d, 0.0) for d, (s, i) in zip(drop, work)]
    parts = []
    for d in drop:
        parts.extend(_split2(d))
    cs = _mm(jnp.concatenate(parts, axis=0), umat)
    acc, carry = list(acc), list(carry)
    for i in reversed(range(nblk)):
        for s in range(nseq):
            j = work.index((s, i))
            c = cs[j * 2 * rows:(j + 1) * 2 * rows]
            both = c[0:rows] + c[rows:2 * rows]
            w = jnp.exp(take[j] - (both[:, 0:page] + carry[s]))
            if i < ncached:
                acc[s] = acc[s] + _mm_nt(w.astype(BF16), vt[s][i][...].astype(BF16))
            else:
                w = jnp.where(valid_new, w, 0.0)
                acc[s] = acc[s] + _mm(w.astype(BF16), new_kv[s][1].astype(BF16))
            carry[s] = carry[s] + both[:, page:]
    return acc, carry


def _sb_sample_head_kernel(pt_ref, q_ref, kn_ref, vn_ref, u_ref, *refs, npages, nseq):
    del pt_ref
    kp = [refs[s * npages:(s + 1) * npages] for s in range(nseq)]
    vp = [refs[(nseq + s) * npages:(nseq + s + 1) * npages] for s in range(nseq)]
    acc_ref, carry_ref, padk_ref, padv_ref = refs[2 * nseq * npages:]
    _load_new_rows(kn_ref, vn_ref, padk_ref, padv_ref, nseq)
    n_new = kn_ref.shape[2]
    page = u_ref.shape[0]
    rows = KV_HEADS * q_ref.shape[2]
    row = lax.broadcasted_iota(jnp.int32, (rows, page), 0)
    col = lax.broadcasted_iota(jnp.int32, (rows, page), 1)
    valid_new = col < (row & (n_new - 1))
    qz = [_sample_queries(q_ref[0, s]).astype(BF16) for s in range(nseq)]
    acc, carry = _sb_sample_blocks(
        qz, kp, vp, [(padk_ref[s], padv_ref[s]) for s in range(nseq)], valid_new, u_ref[...],
        [jnp.zeros((rows, KV_WIDTH), F32)] * nseq, [jnp.zeros((rows, page), F32)] * nseq)
    for s in range(nseq):
        acc_ref[0, s] = acc[s]
        carry_ref[0, s] = carry[s]


def _sb_sample_tail_kernel(pt_ref, q_ref, acc_in_ref, carry_in_ref, u_ref, *refs, npages, nseq):
    del pt_ref
    kp = [refs[s * npages:(s + 1) * npages] for s in range(nseq)]
    vp = [refs[(nseq + s) * npages:(nseq + s + 1) * npages] for s in range(nseq)]
    acc_ref = refs[2 * nseq * npages]
    qz = [_sample_queries(q_ref[0, s]).astype(BF16) for s in range(nseq)]
    acc, _ = _sb_sample_blocks(qz, kp, vp, None, None, u_ref[...],
                               [acc_in_ref[0, s] for s in range(nseq)], [carry_in_ref[0, s] for s in range(nseq)])
    for s in range(nseq):
        acc_ref[0, s] = acc[s]


def _sb_sample(q4, k_new, v_new, umat, cache_k, cache_v, page_table, layer, n_eager):
    nsteps, nseq, n, _ = q4.shape
    rows = KV_HEADS * n
    n_new = k_new.shape[2]
    npages = page_table.shape[1]
    page = cache_k.shape[3]
    n_eager = min(n_eager, npages)
    step4 = lambda b, pt: (b, 0, 0, 0)
    q_spec = pl.BlockSpec((1, nseq, n, KV_WIDTH), step4)
    new_spec = pl.BlockSpec((1, nseq, n_new, KV_WIDTH), step4)
    acc_spec = pl.BlockSpec((1, nseq, rows, KV_WIDTH), step4)
    carry_spec = pl.BlockSpec((1, nseq, rows, page), step4)
    u_spec = pl.BlockSpec(umat.shape, lambda b, pt: (0, 0))
    acc_shape = jax.ShapeDtypeStruct((nsteps, nseq, rows, KV_WIDTH), F32)
    carry_shape = jax.ShapeDtypeStruct((nsteps, nseq, rows, page), F32)

    def page_operands(slots):
        specs = [spec for s in range(nseq) for spec in _page_specs(layer, slots, page, s, nseq)]
        count = nseq * len(slots)
        return specs + specs, [cache_k] * count + [cache_v] * count

    specs, operands = page_operands(range(npages - n_eager, npages))
    acc, carry = pl.pallas_call(
        functools.partial(_sb_sample_head_kernel, npages=n_eager, nseq=nseq),
        grid_spec=pltpu.PrefetchScalarGridSpec(
            num_scalar_prefetch=1, grid=(nsteps,),
            in_specs=[q_spec, new_spec, new_spec, u_spec] + specs,
            out_specs=[acc_spec, carry_spec],
            scratch_shapes=[pltpu.VMEM((nseq, page, KV_WIDTH), F32), pltpu.VMEM((nseq, page, KV_WIDTH), F32)]),
        out_shape=[acc_shape, carry_shape],
        compiler_params=_cparams(1),
        name="sb_sample_head",
    )(page_table, q4, k_new, v_new, umat, *operands)
    if npages > n_eager:
        specs, operands = page_operands(range(npages - n_eager))

        def tail(acc_in):
            return pl.pallas_call(
                functools.partial(_sb_sample_tail_kernel, npages=npages - n_eager, nseq=nseq),
                grid_spec=pltpu.PrefetchScalarGridSpec(
                    num_scalar_prefetch=1, grid=(nsteps,),
                    in_specs=[q_spec, acc_spec, carry_spec, u_spec] + specs,
                    out_specs=acc_spec),
                out_shape=acc_shape,
                compiler_params=_cparams(1),
                name="sb_sample_tail",
            )(page_table, q4, acc_in, carry, umat, *operands)

        acc = lax.cond(jnp.min(carry) < SB_DROP_DONE, tail, lambda acc_in: acc_in, acc)
    a6 = acc.reshape(nsteps, nseq, KV_HEADS, n, KV_HEADS, HEAD_DIM)
    return jnp.stack([a6[:, :, h, :, h, :] for h in range(KV_HEADS)], axis=3).reshape(nsteps, nseq, n, KV_WIDTH)


def _suffix_matrix(tk):
    r = np.arange(tk)
    strict = (r[:, None] > r[None, :]).astype(np.float32)
    return jnp.asarray(np.concatenate([strict, np.ones((tk, tk), np.float32)], axis=1), dtype=BF16)


def _topk_rows(scores, n_valid, nb):
    blk = lax.broadcasted_iota(jnp.int32, scores.shape, 0)
    valid = blk < n_valid
    s = jnp.where(valid, scores, NEG_INF)
    rank = jnp.zeros(scores.shape, jnp.int32)
    for m in range(nb):
        other = s[m:m + 1, :]
        beats = (other > s) | ((other == s) & (blk > m))
        rank = rank + jnp.where(beats, 1, 0)
    return jnp.where(valid & (rank < MOBA_TOPK), 1.0, 0.0)


def _topk_lanes(scores, n_valid, nb):
    blk = lax.broadcasted_iota(jnp.int32, scores.shape, 1)
    valid = blk < n_valid
    s = jnp.where(valid, scores, NEG_INF)
    rank = jnp.zeros(scores.shape, jnp.int32)
    for m in range(nb):
        other = s[:, m:m + 1]
        beats = (other > s) | ((other == s) & (blk > m))
        rank = rank + jnp.where(beats, 1, 0)
    return jnp.where(valid & (rank < MOBA_TOPK), 1.0, 0.0)


def _moba_prompt_kernel(qtf_ref, qt_ref, k_ref, vt_ref, km_ref, o_ref,
                        qz_ref, sel_ref, m_ref, l_ref, acc_ref, *, nb):
    qblk = pl.program_id(1)
    tk = k_ref.shape[1]
    q2 = qz_ref.shape[2]
    tq = q2 // GROUP
    _load_head_queries(qt_ref, qz_ref)
    heads = range(KV_HEADS)
    qf = jnp.concatenate(
        [jnp.concatenate([qtf_ref[0, (kvh * GROUP + g) * HEAD_DIM:(kvh * GROUP + g + 1) * HEAD_DIM, :]
                          for kvh in heads], axis=0) for g in range(GROUP)], axis=1)
    q_hi, q_lo = _split2(qf)
    km = km_ref[0]
    lane_head = lax.broadcasted_iota(jnp.int32, km.shape, 1) >> 6
    km_parts = [_split2(jnp.where(lane_head == kvh, km, 0.0)) for kvh in heads]
    scores = [_mm(km_hi, q_hi) + (_mm(km_hi, q_lo) + _mm(km_lo, q_hi)) for km_hi, km_lo in km_parts]
    for kvh in heads:
        sel_ref[kvh] = _topk_rows(scores[kvh], qblk, nb)
    key = lax.broadcasted_iota(jnp.int32, (tk, q2), 0)
    qloc = lax.broadcasted_iota(jnp.int32, (tk, q2), 1) & (tq - 1)
    causal = key <= qloc
    ones_rows = jnp.ones((BF16_ROWS, tk), BF16)

    def weighted(n, kvh, p):
        lhs = jnp.concatenate([vt_ref[n, kvh * HEAD_DIM:(kvh + 1) * HEAD_DIM, :], ones_rows], axis=0)
        res = _mm(lhs, p.astype(BF16))
        return res[0:HEAD_DIM], res[HEAD_DIM:HEAD_DIM + 1]

    kb = k_ref[qblk]
    st = [jnp.where(causal, _mm(kb, qz_ref[kvh]), NEG_INF) for kvh in heads]
    m0 = [jnp.max(s, axis=0, keepdims=True) for s in st]
    p = [jnp.exp2(st[kvh] - m0[kvh]) for kvh in heads]
    for kvh in heads:
        m_ref[kvh] = m0[kvh]
        acc_ref[kvh], l_ref[kvh] = weighted(qblk, kvh, p[kvh])

    def body(n, c):
        kb = k_ref[n]
        bias = [jnp.where(sel_ref[kvh, pl.ds(n, 1), :] > 0.5, 0.0, NEG_INF) for kvh in heads]
        st = [_mm(kb, qz_ref[kvh]) for kvh in heads]
        m_old = [m_ref[kvh] for kvh in heads]
        m_new = [jnp.maximum(m_old[kvh], jnp.max(st[kvh], axis=0, keepdims=True) + bias[kvh]) for kvh in heads]
        p = [jnp.exp2(st[kvh] - (m_new[kvh] - bias[kvh])) for kvh in heads]
        alpha = [jnp.exp2(m_old[kvh] - m_new[kvh]) for kvh in heads]
        pv = [weighted(n, kvh, p[kvh]) for kvh in heads]
        for kvh in heads:
            l_ref[kvh] = alpha[kvh] * l_ref[kvh] + pv[kvh][1]
            acc_ref[kvh] = alpha[kvh] * acc_ref[kvh] + pv[kvh][0]
            m_ref[kvh] = m_new[kvh]
        return c

    lax.fori_loop(0, qblk, body, 0)
    _store_heads_out(o_ref, [acc_ref[kvh] / l_ref[kvh] for kvh in heads], tq)


def _moba_prompt(qtf, qt, k3, vt3, kmean, b, tiles):
    tq = k3.shape[1]
    q2 = GROUP * tq
    return pl.pallas_call(
        functools.partial(_moba_prompt_kernel, nb=tiles),
        grid=(b, tiles),
        in_specs=[pl.BlockSpec((1, ATT_WIDTH, tq), lambda i, j: (i * tiles + j, 0, 0))] + _att_specs(tiles, tq)
                 + [pl.BlockSpec((1, tiles, KV_WIDTH), lambda i, j: (i, 0, 0))],
        out_specs=pl.BlockSpec((1, tq, ATT_WIDTH), lambda i, j: (i, j, 0)),
        out_shape=jax.ShapeDtypeStruct((b, tiles * tq, ATT_WIDTH), BF16),
        scratch_shapes=[pltpu.VMEM((KV_HEADS, KV_WIDTH, q2), BF16), pltpu.VMEM((KV_HEADS, tiles, q2), F32),
                        pltpu.VMEM((KV_HEADS, 1, q2), F32), pltpu.VMEM((KV_HEADS, 1, q2), F32),
                        pltpu.VMEM((KV_HEADS, HEAD_DIM, q2), F32)],
        compiler_params=_cparams(2),
        name="moba_prompt",
    )(qtf, qt, k3, vt3, kmean)


def _moba_sample_kernel(pt_ref, q_ref, kn_ref, vn_ref, *refs, npages, nseq):
    del pt_ref
    kp, vp, o_ref, padk_ref, padv_ref = _sample_refs(refs, npages, nseq)
    _load_new_rows(kn_ref, vn_ref, padk_ref, padv_ref, nseq)
    n_new = kn_ref.shape[2]
    page = padk_ref.shape[1]
    per_blk = MOBA_BLOCK // page
    nb_past = npages // per_blk
    rows = KV_HEADS * q_ref.shape[2]
    seqs = range(nseq)
    qf = [_sample_queries(q_ref[0, s]) for s in seqs]
    qz = [(q * ATT_SCALE).astype(BF16) for q in qf]
    ones = jnp.ones((page, LANES), BF16)
    lane = lax.broadcasted_iota(jnp.int32, (KV_WIDTH, LANES), 1)
    k_parts = [[_split2(kp[s][pg][...]) for pg in range(npages)] for s in seqs]
    k_sums = [[_mm(hi, ones) + _mm(lo, ones) for hi, lo in k_parts[s]] for s in seqs]
    km = []
    for s in seqs:
        km_s = jnp.zeros((KV_WIDTH, LANES), F32)
        for blk in range(nb_past):
            tot = k_sums[s][blk * per_blk]
            for r in range(1, per_blk):
                tot = tot + k_sums[s][blk * per_blk + r]
            km_s = jnp.where(lane == blk, tot * (1.0 / MOBA_BLOCK), km_s)
        km.append(km_s)
    sel = []
    for s in seqs:
        q_hi, q_lo = _split2(qf[s])
        km_hi, km_lo = _split2(km[s])
        scores = _mm(q_hi, km_hi) + (_mm(q_hi, km_lo) + _mm(q_lo, km_hi))
        sel.append(_topk_lanes(scores, nb_past, nb_past))
    row = lax.broadcasted_iota(jnp.int32, (rows, page), 0)
    col = lax.broadcasted_iota(jnp.int32, (rows, page), 1)
    visible = (col <= (row & (n_new - 1))) & (col < n_new)
    logits = [[jnp.where(sel[s][:, pg // per_blk:pg // per_blk + 1] > 0.5,
                         _mm(qz[s], kp[s][pg][...].astype(BF16)), NEG_INF) for pg in range(npages)]
              + [jnp.where(visible, _mm_nt(qz[s], padk_ref[s].astype(BF16)), NEG_INF)] for s in seqs]
    m = []
    for s in seqs:
        m_el = logits[s][0]
        for x in logits[s][1:]:
            m_el = jnp.maximum(m_el, x)
        m.append(jnp.max(m_el, axis=1, keepdims=True))
    acc = [jnp.zeros((rows, KV_WIDTH), F32) for _ in seqs]
    l_el = [jnp.zeros((rows, page), F32) for _ in seqs]
    for pg in range(npages + 1):
        for s in seqs:
            p = jnp.exp(logits[s][pg] - m[s])
            l_el[s] = l_el[s] + p
            if pg < npages:
                acc[s] = acc[s] + _mm_nt(p.astype(BF16), vp[s][pg][...].astype(BF16))
            else:
                acc[s] = acc[s] + _mm(p.astype(BF16), padv_ref[s].astype(BF16))
    for s in seqs:
        o_ref[0, s] = _sample_heads_out(acc[s] / jnp.sum(l_el[s], axis=1, keepdims=True))


def _post_kernel(x_ref, a_ref, b_ref, c_ref, g_ref, wb_ref, wo_ref, p_ref, gm_ref, wu_ref, wd_ref, gp_ref, wg_ref,
                 wp_ref, o_ref, *, f_chunk):
    d = x_ref.shape[1]
    m = g_ref[:, 0:d] * _mm(a_ref[...].astype(BF16), wb_ref[0])
    m = m + g_ref[:, d:2 * d] * _mm(b_ref[...].astype(BF16), wb_ref[1])
    m = m + g_ref[:, 2 * d:3 * d] * _mm(c_ref[...].astype(BF16), wb_ref[2])
    x = x_ref[...] + _mm(m.astype(BF16), wo_ref[...])
    ms = jnp.mean(x * x, axis=-1, keepdims=True)
    h = (x * lax.rsqrt(ms + RMS_EPS) * gm_ref[...]).astype(BF16)
    d_ff = wu_ref.shape[1]
    y = x
    for f0 in range(0, d_ff, f_chunk):
        hid = jnp.maximum(_mm(h, wu_ref[:, f0:f0 + f_chunk]), 0.0)
        y = y + _mm((hid * hid).astype(BF16), wd_ref[f0:f0 + f_chunk, :])
    ms2 = jnp.mean(y * y, axis=-1, keepdims=True)
    h2 = (y * lax.rsqrt(ms2 + RMS_EPS) * gp_ref[...]).astype(BF16)
    gate = _sigmoid(_mm(h2, wg_ref[...]))
    o_ref[...] = y + gate * _mm(p_ref[...].astype(BF16), wp_ref[...])


def _post(x, a, b, c, gate, wb, wo, p, gm, wu, wd, gp, wg, wp, tm, layer):
    n, d = x.shape
    row = lambda i: (i, 0)
    resident = lambda arr: _layer_spec(arr, layer, pl.Buffered(1))
    return pl.pallas_call(
        functools.partial(_post_kernel, f_chunk=512),
        grid=(n // tm,),
        in_specs=[pl.BlockSpec((tm, d), row), pl.BlockSpec((tm, ATT_WIDTH), row), pl.BlockSpec((tm, ATT_WIDTH), row),
                  pl.BlockSpec((tm, ATT_WIDTH), row), pl.BlockSpec((tm, N_BRANCH * d), row),
                  resident(wb), resident(wo), pl.BlockSpec((None, tm, p.shape[2]), lambda i: (layer, i, 0)),
                  resident(gm), resident(wu), resident(wd), resident(gp), resident(wg), resident(wp)],
        out_specs=pl.BlockSpec((tm, d), row),
        out_shape=jax.ShapeDtypeStruct((n, d), F32),
        compiler_params=_cparams(1, VMEM_LIMIT),
        name="merge_mlp_ple",
    )(x, a, b, c, gate, wb, wo, p, gm, wu, wd, gp, wg, wp)


def _rope_tables(pos):
    half = ROT_DIM // 2
    inv = ROPE_THETA ** (-jnp.arange(half, dtype=F32) / half)
    ang = pos.astype(F32)[:, None] * inv[None, :]
    cos, sin = jnp.cos(ang), jnp.sin(ang)
    t = pos.shape[0]
    c64 = jnp.concatenate([cos, cos, jnp.ones((t, HEAD_DIM - ROT_DIM), F32)], axis=1)
    s64 = jnp.concatenate([-sin, sin, jnp.zeros((t, HEAD_DIM - ROT_DIM), F32)], axis=1)
    reps = LANES // HEAD_DIM
    return jnp.tile(c64, (1, reps)), jnp.tile(s64, (1, reps))


def _head_group_matrix():
    r = np.arange(ATT_WIDTH) // HEAD_DIM
    return jnp.asarray((r[:, None] == r[None, :]).astype(np.float32), dtype=BF16)


def kernel(x_prompt, x_sample, p_prompt, p_sample, cache_sb_k, cache_sb_v, cache_moba_k, cache_moba_v,
           state_pool, page_table, norm_mix, w_in, pool_w, pool_scale, moba_q_norm, moba_k_norm,
           w_branch, w_out, norm_mlp, w_up, w_down, norm_ple, w_ple_gate, w_ple):
    depth = w_in.shape[0]
    bp, seq, d = x_prompt.shape
    bs, t_new, _ = x_sample.shape
    n_p, n_s = bp * seq, bs * t_new
    n_phys, page = cache_sb_k.shape[1], cache_sb_k.shape[2]
    past_len = page_table.shape[1] * page
    tiles = seq // ATT_TILE
    assert seq % ATT_TILE == 0 and past_len % MOBA_BLOCK == 0 and MOBA_BLOCK % page == 0 and page == LANES
    assert t_new & (t_new - 1) == 0 and GROUP * t_new == 8 and past_len >= POOL_HIST >= t_new
    assert past_len // MOBA_BLOCK <= LANES

    w_in_b, w_branch_b = w_in.astype(BF16), w_branch.astype(BF16)
    w_out_b, w_up_b, w_down_b = w_out.astype(BF16), w_up.astype(BF16), w_down.astype(BF16)
    w_pg_b, w_ple_b, pool_w_b = w_ple_gate.astype(BF16), w_ple.astype(BF16), pool_w.astype(BF16)
    vec = lambda a: a[:, None, :]
    norm_mix3, norm_mlp3, norm_ple3, pool_scale3 = vec(norm_mix), vec(norm_mlp), vec(norm_ple), vec(pool_scale)
    qn = vec(jnp.tile(moba_q_norm, (1, N_HEADS)))
    kn = vec(jnp.tile(moba_k_norm, (1, KV_HEADS)))
    gmat = _head_group_matrix()

    tm_p = ATT_TILE
    tm_s = 256 if n_s % 256 == 0 else n_s
    cos_p, sin_p = _rope_tables(jnp.arange(seq))
    cos_s, sin_s = _rope_tables(jnp.tile(past_len + jnp.arange(t_new), bs))
    caches = [c.transpose(0, 1, 3, 4, 2).reshape(depth, n_phys, KV_WIDTH, page)
              for c in (cache_sb_k, cache_sb_v, cache_moba_k, cache_moba_v)]
    pp = p_prompt.reshape(depth, n_p, -1)
    psm = p_sample.reshape(depth, n_s, -1)

    xp = x_prompt.reshape(n_p, d)
    xs = x_sample.reshape(n_s, d)
    kv_stacks = None
    stack_shape = (depth, bp, KV_WIDTH, seq)
    pool_p = []
    outs_s = [[] for _ in range(5)]
    nseq = 2 if bs % 2 == 0 else 1

    def to_q4(q):
        return q.reshape(bs, t_new, KV_HEADS, GROUP, HEAD_DIM).transpose(0, 3, 1, 2, 4).reshape(
            bs // nseq, nseq, GROUP * t_new, KV_WIDTH)

    def from_q4(r):
        return r.reshape(bs, GROUP, t_new, KV_HEADS, HEAD_DIM).transpose(0, 2, 3, 1, 4).reshape(n_s, ATT_WIDTH)

    def new_rows(a):
        return a.reshape(bs // nseq, nseq, t_new, KV_WIDTH)

    sb_umat = _suffix_matrix(page)

    for i in range(depth):
        layer_w = (norm_mix3, w_in_b, qn, kn)
        post_w = (norm_mlp3, w_up_b, w_down_b, norm_ple3, w_pg_b, w_ple_b)

        (u, qt_sb, k_sb_b, vt_sb, qt_mf, qt_mb, k_mb_b, vt_mb, kmean, gate, *kv_stacks) = _inproj(
            xp, *layer_w, cos_p, sin_p, gmat, tm_p, i, kv_stacks, stack_shape)
        k3 = lambda a: a.reshape(n_p // ATT_TILE, ATT_TILE, KV_WIDTH)
        u3 = u.reshape(bp, seq, ATT_WIDTH)
        a = _pool_prompt(u3, pool_w_b, pool_scale3, min(512, seq), i).reshape(n_p, ATT_WIDTH)
        b = _sb_prompt(qt_sb, k3(k_sb_b), vt_sb, bp, tiles).reshape(n_p, ATT_WIDTH)
        c = _moba_prompt(qt_mf, qt_mb, k3(k_mb_b), vt_mb, kmean.reshape(bp, tiles, KV_WIDTH), bp, tiles).reshape(
            n_p, ATT_WIDTH)
        xp = _post(xp, a, b, c, gate, w_branch_b, w_out_b, pp, *post_w, tm_p, i)
        pool_p.append(u3[:, seq - POOL_HIST:])

        (u, q_sb, k_sb, v_sb, q_mb, k_mb, v_mb, gate) = _inproj(xs, *layer_w, cos_s, sin_s, gmat, tm_s, i)
        r3 = lambda a: a.reshape(bs, t_new, a.shape[-1])
        st_t = state_pool[i].transpose(1, 0, 2)
        a = _pool_sample(st_t, r3(u).transpose(1, 0, 2), pool_w_b, pool_scale3, past_len, i).transpose(1, 0, 2).reshape(
            n_s, ATT_WIDTH)
        b = from_q4(_sb_sample(to_q4(q_sb), new_rows(k_sb), new_rows(v_sb), sb_umat, caches[0], caches[1],
                               page_table, i, SB_EAGER_PAGES))
        c = from_q4(_sample_call(_moba_sample_kernel, "moba_sample", to_q4(q_mb), new_rows(k_mb), new_rows(v_mb), [],
                                 caches[2], caches[3], page_table, i))
        xs = _post(xs, a, b, c, gate, w_branch_b, w_out_b, psm, *post_w, tm_s, i)
        pool_state = jnp.concatenate([state_pool[i][:, t_new:], r3(u)], axis=1)
        for lst, val in zip(outs_s, (k_sb, v_sb, k_mb, v_mb, pool_state)):
            lst.append(val)

    def kv_prompt(stack):
        return stack.reshape(depth, bp, KV_HEADS, HEAD_DIM, seq).transpose(0, 1, 4, 2, 3)

    def kv_sample(lst):
        return jnp.stack(lst).reshape(depth, bs, t_new, KV_HEADS, HEAD_DIM)

    return (xp.reshape(bp, seq, d), xs.reshape(bs, t_new, d),
            kv_prompt(kv_stacks[0]), kv_prompt(kv_stacks[1]), kv_prompt(kv_stacks[2]), kv_prompt(kv_stacks[3]),
            jnp.stack(pool_p),
            kv_sample(outs_s[0]), kv_sample(outs_s[1]), kv_sample(outs_s[2]), kv_sample(outs_s[3]),
            jnp.stack(outs_s[4]))
```
